```python
import jax, jax.numpy as jnp
from jax import lax
import numpy as np

D_MODEL = 1024
BATCH = 8
SEQ = 4096
DEPTH = 1

MIX_WIDTH = D_MODEL
POOL_WIDTH = MIX_WIDTH // 2
POOL_WINDOWS = (2, 4, 8, 16)
N_POOL_GROUPS = len(POOL_WINDOWS)
POOL_GROUP_DIM = POOL_WIDTH // N_POOL_GROUPS
ATTN_WIDTH = MIX_WIDTH - POOL_WIDTH
HEAD_DIM = 64
N_Q_HEADS = ATTN_WIDTH // HEAD_DIM
N_KV_HEADS = 2
GQA_GROUP = N_Q_HEADS // N_KV_HEADS
WINDOW = 128
BLOCK = 128
N_BUCKETS = 32
MAX_EXACT = N_BUCKETS // 2
MAX_DISTANCE = 128
IN_WIDTH = POOL_WIDTH + N_Q_HEADS * HEAD_DIM + 2 * N_KV_HEADS * HEAD_DIM
D_FF = -(-(8 * D_MODEL) // (3 * 256)) * 256
EPS = 1e-6
NEG_INF = -1e30

kernel_name = "hybrid_pool_swa_sink_t5bias_block"


def rmsnorm(x, g):
    xf = x.astype(jnp.float32)
    xf = xf * lax.rsqrt(jnp.mean(xf * xf, axis=-1, keepdims=True) + EPS)
    return xf.astype(x.dtype) * g


def multiscale_pool(u, w_pool, pool_scale):
    b, s, _ = u.shape
    ug = u.reshape(b, s, N_POOL_GROUPS, POOL_GROUP_DIM)
    csum = jnp.cumsum(ug.astype(jnp.float32), axis=1)
    pos = jnp.arange(1, s + 1, dtype=jnp.float32)
    pooled = []
    for g, w in enumerate(POOL_WINDOWS):
        c = csum[:, :, g]
        lagged = jnp.pad(c, ((0, 0), (w, 0), (0, 0)))[:, :s]
        count = jnp.minimum(pos, float(w))[None, :, None]
        pooled.append((c - lagged) / count)
    pooled = jnp.stack(pooled, axis=2).astype(u.dtype) - ug
    mixed = jnp.einsum('bsgc,gcd->bsgd', pooled, w_pool)
    return mixed.reshape(b, s, POOL_WIDTH) * pool_scale


def relative_bias_band(rel_bias):
    qi = jnp.arange(BLOCK)[:, None]
    kj = jnp.arange(2 * BLOCK)[None, :]
    dist = qi + BLOCK - kj
    n = jnp.maximum(dist, 0)
    nf = jnp.maximum(n, 1).astype(jnp.float32)
    large = MAX_EXACT + (jnp.log(nf / MAX_EXACT) / np.float32(np.log(MAX_DISTANCE / MAX_EXACT))
                         * (N_BUCKETS - MAX_EXACT)).astype(jnp.int32)
    large = jnp.minimum(large, N_BUCKETS - 1)
    bucket = jnp.where(n < MAX_EXACT, n, large)
    bias = rel_bias.astype(jnp.float32)[bucket]
    bias = jnp.transpose(bias, (2, 0, 1)).reshape(N_KV_HEADS, GQA_GROUP, BLOCK, 2 * BLOCK)
    in_window = (dist >= 0) & (dist < WINDOW)
    return bias, in_window


def sliding_window_attention(q, k, v, rel_bias, sinks):
    b, s = q.shape[:2]
    nb = s // BLOCK
    qb = q.reshape(b, nb, BLOCK, N_KV_HEADS, GQA_GROUP, HEAD_DIM)

    def band(t):
        prev = jnp.pad(t, ((0, 0), (BLOCK, 0), (0, 0), (0, 0)))[:, :s]
        prev = prev.reshape(b, nb, BLOCK, N_KV_HEADS, HEAD_DIM)
        cur = t.reshape(b, nb, BLOCK, N_KV_HEADS, HEAD_DIM)
        return jnp.concatenate([prev, cur], axis=2)

    kb, vb = band(k), band(v)
    bias, in_window = relative_bias_band(rel_bias)
    key_exists = (jnp.arange(nb)[:, None] > 0) | (jnp.arange(2 * BLOCK)[None, :] >= BLOCK)
    mask = in_window[None] & key_exists[:, None, :]
    scale = 1.0 / np.sqrt(HEAD_DIM).astype(np.float32)
    logits = jnp.einsum('bnqhgd,bnkhd->bnhgqk', qb, kb).astype(jnp.float32) * scale
    logits = logits + bias[None, None]
    logits = jnp.where(mask[None, :, None, None], logits, NEG_INF)
    sink = sinks.astype(jnp.float32).reshape(N_KV_HEADS, GQA_GROUP)[None, None, :, :, None, None]
    m = jnp.maximum(jnp.max(logits, axis=-1, keepdims=True), sink)
    p = jnp.exp(logits - m)
    denom = jnp.sum(p, axis=-1, keepdims=True) + jnp.exp(sink - m)
    probs = (p / denom).astype(v.dtype)
    out = jnp.einsum('bnhgqk,bnkhd->bnqhgd', probs, vb)
    return out.reshape(b, s, ATTN_WIDTH)


def setup_inputs(seed: int = 0) -> dict:
    key = jax.random.key(seed)
    ks = jax.random.split(key, 16)
    f32 = jnp.float32

    def w(k, shape, fan_in):
        return jax.random.normal(k, shape, f32) * (fan_in ** -0.5)

    def gain(k, shape):
        return 1.0 + 0.05 * jax.random.normal(k, shape, f32)

    return {
        "x": jax.random.normal(ks[0], (BATCH, SEQ, D_MODEL), f32),
        "g_pre_mix": gain(ks[1], (DEPTH, D_MODEL)),
        "w_in": w(ks[2], (DEPTH, D_MODEL, IN_WIDTH), D_MODEL),
        "w_pool": w(ks[3], (DEPTH, N_POOL_GROUPS, POOL_GROUP_DIM, POOL_GROUP_DIM), POOL_GROUP_DIM),
        "pool_scale": 1.0 + 0.1 * jax.random.normal(ks[4], (DEPTH, POOL_WIDTH), f32),
        "rel_bias": 0.5 * jax.random.normal(ks[5], (N_BUCKETS, N_Q_HEADS), f32),
        "sinks": 0.5 * jax.random.normal(ks[6], (DEPTH, N_Q_HEADS), f32),
        "w_out": w(ks[7], (DEPTH, MIX_WIDTH, D_MODEL), MIX_WIDTH),
        "g_post_mix": gain(ks[8], (DEPTH, D_MODEL)),
        "g_pre_ffn": gain(ks[9], (DEPTH, D_MODEL)),
        "w_gate": w(ks[10], (DEPTH, D_MODEL, D_FF), D_MODEL),
        "w_up": w(ks[11], (DEPTH, D_MODEL, D_FF), D_MODEL),
        "w_down": w(ks[12], (DEPTH, D_FF, D_MODEL), D_FF),
        "g_post_ffn": gain(ks[13], (DEPTH, D_MODEL)),
    }


def reference(x, g_pre_mix, w_in, w_pool, pool_scale, rel_bias, sinks, w_out,
              g_post_mix, g_pre_ffn, w_gate, w_up, w_down, g_post_ffn):
    b, s, _ = x.shape
    q_end = POOL_WIDTH + N_Q_HEADS * HEAD_DIM
    k_end = q_end + N_KV_HEADS * HEAD_DIM
    for l in range(DEPTH):
        h = rmsnorm(x, g_pre_mix[l])
        proj = h @ w_in[l]
        u = proj[..., :POOL_WIDTH]
        q = proj[..., POOL_WIDTH:q_end].reshape(b, s, N_Q_HEADS, HEAD_DIM)
        k = proj[..., q_end:k_end].reshape(b, s, N_KV_HEADS, HEAD_DIM)
        v = proj[..., k_end:].reshape(b, s, N_KV_HEADS, HEAD_DIM)
        pool_out = multiscale_pool(u, w_pool[l], pool_scale[l])
        attn_out = sliding_window_attention(q, k, v, rel_bias, sinks[l])
        mix = jnp.concatenate([pool_out, attn_out], axis=-1) @ w_out[l]
        x = x + rmsnorm(mix, g_post_mix[l])
        h = rmsnorm(x, g_pre_ffn[l])
        f = (jax.nn.silu(h @ w_gate[l]) * (h @ w_up[l])) @ w_down[l]
        x = x + rmsnorm(f, g_post_ffn[l])
    return x
```

```python
import functools

import numpy as np
import jax
import jax.numpy as jnp
from jax import lax
from jax.experimental import pallas as pl
from jax.experimental.pallas import tpu as pltpu

D_MODEL = 1024
POOL_WIDTH = 512
POOL_WINDOWS = (2, 4, 8, 16)
N_POOL_GROUPS = len(POOL_WINDOWS)
POOL_GROUP_DIM = POOL_WIDTH // N_POOL_GROUPS
ATTN_WIDTH = 512
HEAD_DIM = 64
N_Q_HEADS = 8
N_KV_HEADS = 2
GQA_GROUP = N_Q_HEADS // N_KV_HEADS
WINDOW = 128
BLOCK = 128
N_BUCKETS = 32
MAX_EXACT = N_BUCKETS // 2
MAX_DISTANCE = 128
KV_WIDTH = N_KV_HEADS * HEAD_DIM
IN_WIDTH = POOL_WIDTH + ATTN_WIDTH + 2 * KV_WIDTH
D_FF = 2816
EPS = 1e-6
NEG_INF = -1e30

POOL_HALO = 16
MIX_TILE = 512
FFN_TILE = 512
FFN_CHUNK = 256
VMEM_LIMIT_BYTES = 56 * 1024 * 1024

_F32 = jnp.float32
_BF16 = jnp.bfloat16


def _dot(a, b):
    return jnp.dot(a, b, preferred_element_type=_F32)


def _rmsnorm(xf, g):
    ms = jnp.mean(xf * xf, axis=-1, keepdims=True)
    return xf * lax.rsqrt(ms + EPS) * g


def _bucket_table_t():
    qi = np.arange(BLOCK)[None, :]
    kj = np.arange(2 * BLOCK)[:, None]
    dist = qi + BLOCK - kj
    n = np.maximum(dist, 0)
    nf = np.maximum(n, 1).astype(np.float32)
    large = MAX_EXACT + (
        np.log(nf / np.float32(MAX_EXACT)) / np.float32(np.log(MAX_DISTANCE / MAX_EXACT)) * np.float32(N_BUCKETS - MAX_EXACT)
    ).astype(np.int32)
    large = np.minimum(large, N_BUCKETS - 1)
    bucket = np.where(n < MAX_EXACT, n, large)
    in_window = (dist >= 0) & (dist < WINDOW)
    return np.where(in_window, bucket, -1).astype(np.int32)


def _mix_kernel(x_ref, gpre_ref, win_ref, wpool_ref, pscale_ref, relb_ref, sinks_ref, bkt_ref, wout_ref, gpost_ref,
                o_ref, u_buf, q_stack, kcat, vt, mixcat, bias_t):
    tile = x_ref.shape[0]
    n_blocks = tile // BLOCK
    b = pl.program_id(0)
    i = pl.program_id(1)

    @pl.when((b == 0) & (i == 0))
    def _build_bias():
        bkt = bkt_ref[...]
        key_is_prev = lax.broadcasted_iota(jnp.int32, bkt.shape, 0) < BLOCK
        for h in range(N_KV_HEADS):
            for g in range(GQA_GROUP):
                head = h * GQA_GROUP + g

                def body(bb, tab):
                    return jnp.where(bkt == bb, relb_ref[bb, head], tab)

                tab = lax.fori_loop(0, N_BUCKETS, body, jnp.zeros(bkt.shape, _F32))
                tab = jnp.where(bkt < 0, NEG_INF, tab)
                rows = slice(h * 2 * BLOCK, (h + 1) * 2 * BLOCK)
                cols = slice(g * BLOCK, (g + 1) * BLOCK)
                bias_t[0, rows, cols] = tab
                bias_t[1, rows, cols] = jnp.where(key_is_prev, NEG_INF, tab)

    @pl.when(i == 0)
    def _reset_history():
        u_buf[0:POOL_HALO, :] = jnp.zeros((POOL_HALO, POOL_WIDTH), _F32)
        kcat[:, 0:BLOCK, :] = jnp.zeros((N_KV_HEADS, BLOCK, KV_WIDTH), _BF16)
        vt[:, 0:BLOCK] = jnp.zeros((KV_WIDTH, BLOCK), _BF16)

    h_in = _rmsnorm(x_ref[...], gpre_ref[...]).astype(_BF16)

    u = _dot(h_in, win_ref[:, 0:POOL_WIDTH])
    u_buf[POOL_HALO:POOL_HALO + tile, :] = u
    q = _dot(h_in, win_ref[:, POOL_WIDTH:POOL_WIDTH + ATTN_WIDTH]).astype(_BF16)
    for j in range(n_blocks):
        for g in range(GQA_GROUP):
            q_stack[j, g * BLOCK:(g + 1) * BLOCK, :] = q[j * BLOCK:(j + 1) * BLOCK, g * BLOCK:(g + 1) * BLOCK]
    kv = _dot(h_in, win_ref[:, POOL_WIDTH + ATTN_WIDTH:IN_WIDTH])
    k = kv[:, 0:KV_WIDTH]
    v = kv[:, KV_WIDTH:2 * KV_WIDTH]
    lane = lax.broadcasted_iota(jnp.int32, k.shape, 1)
    kcat[0, BLOCK:BLOCK + tile, :] = jnp.where(lane < HEAD_DIM, k, 0.0).astype(_BF16)
    kcat[1, BLOCK:BLOCK + tile, :] = jnp.where(lane >= HEAD_DIM, k, 0.0).astype(_BF16)
    vt[:, BLOCK:BLOCK + tile] = v.T.astype(_BF16)

    pos = i * tile + lax.broadcasted_iota(jnp.int32, (tile, 1), 0)
    for g, w in enumerate(POOL_WINDOWS):
        cols = slice(g * POOL_GROUP_DIM, (g + 1) * POOL_GROUP_DIM)
        ug = u[:, cols]
        acc = ug
        for lag in range(1, w):
            acc = acc + u_buf[POOL_HALO - lag:POOL_HALO - lag + tile, cols]
        count = jnp.minimum(pos + 1, w).astype(_F32)
        pooled = acc / count - ug
        mixed = _dot(pooled.astype(_BF16), wpool_ref[g]) * pscale_ref[:, cols]
        mixcat[:, cols] = mixed.astype(_BF16)

    col = lax.broadcasted_iota(jnp.int32, (1, GQA_GROUP * BLOCK), 1)
    sink_rows = []
    for h in range(N_KV_HEADS):
        row = jnp.full((1, GQA_GROUP * BLOCK), sinks_ref[h * GQA_GROUP], _F32)
        for g in range(1, GQA_GROUP):
            row = jnp.where(col >= g * BLOCK, sinks_ref[h * GQA_GROUP + g], row)
        sink_rows.append(row)
    first_variant = jnp.where(i == 0, 1, 0)
    for j in range(n_blocks):
        band = slice(j * BLOCK, (j + 2) * BLOCK)
        kb = jnp.concatenate([kcat[0, band, :], kcat[1, band, :]], axis=0)
        s = lax.dot_general(kb, q_stack[j], (((1,), (1,)), ((), ())), preferred_element_type=_F32)
        s = s + (bias_t[first_variant] if j == 0 else bias_t[0])
        outs = []
        for h in range(N_KV_HEADS):
            sh = s[h * 2 * BLOCK:(h + 1) * 2 * BLOCK, :]
            m = jnp.maximum(jnp.max(sh, axis=0, keepdims=True), sink_rows[h])
            p = jnp.exp(sh - m)
            denom = jnp.sum(p, axis=0, keepdims=True) + jnp.exp(sink_rows[h] - m)
            probs = (p * (1.0 / denom)).astype(_BF16)
            outs.append(_dot(vt[h * HEAD_DIM:(h + 1) * HEAD_DIM, band], probs))
        o_t = jnp.concatenate(outs, axis=0)
        for g in range(GQA_GROUP):
            blk = o_t[:, g * BLOCK:(g + 1) * BLOCK].T
            mixcat[j * BLOCK:(j + 1) * BLOCK, POOL_WIDTH + g * BLOCK:POOL_WIDTH + (g + 1) * BLOCK] = blk.astype(_BF16)

    mix = _dot(mixcat[...], wout_ref[...])
    o_ref[...] = x_ref[...] + _rmsnorm(mix, gpost_ref[...])

    u_buf[0:POOL_HALO, :] = u_buf[tile:tile + POOL_HALO, :]
    kcat[:, 0:BLOCK, :] = kcat[:, tile:tile + BLOCK, :]
    vt[:, 0:BLOCK] = vt[:, tile:tile + BLOCK]


def _ffn_kernel(x_ref, gpre_ref, wg_ref, wu_ref, wd_ref, gpost_ref, o_ref):
    h_in = _rmsnorm(x_ref[...], gpre_ref[...]).astype(_BF16)
    acc = None
    for c in range(0, D_FF, FFN_CHUNK):
        gate = _dot(h_in, wg_ref[:, c:c + FFN_CHUNK])
        up = _dot(h_in, wu_ref[:, c:c + FFN_CHUNK])
        act = (gate * (1.0 / (1.0 + jnp.exp(-gate))) * up).astype(_BF16)
        part = _dot(act, wd_ref[c:c + FFN_CHUNK, :])
        acc = part if acc is None else acc + part
    o_ref[...] = x_ref[...] + _rmsnorm(acc, gpost_ref[...])


def _resident(shape):
    return pl.BlockSpec(shape, lambda *_: (0,) * len(shape), pipeline_mode=pl.Buffered(1))


def _mix_layer(x, g_pre, w_in, w_pool, pool_scale, rel_bias, sinks, w_out, g_post):
    batch, seq, _ = x.shape
    tile = MIX_TILE
    n_blocks = tile // BLOCK
    q0, q1 = POOL_WIDTH, POOL_WIDTH + ATTN_WIDTH
    w_q = w_in[:, q0:q1].reshape(D_MODEL, N_KV_HEADS, GQA_GROUP, HEAD_DIM).transpose(0, 2, 1, 3).reshape(D_MODEL, ATTN_WIDTH)
    w_in_p = jnp.concatenate([w_in[:, :q0], w_q * (HEAD_DIM ** -0.5), w_in[:, q1:]], axis=1).astype(_BF16)
    w_out_attn = w_out[POOL_WIDTH:].reshape(N_KV_HEADS, GQA_GROUP, HEAD_DIM, D_MODEL).transpose(1, 0, 2, 3).reshape(ATTN_WIDTH, D_MODEL)
    w_out_p = jnp.concatenate([w_out[:POOL_WIDTH], w_out_attn], axis=0).astype(_BF16)
    bucket_t = jnp.asarray(_bucket_table_t())

    smem = pl.BlockSpec(memory_space=pltpu.SMEM)
    return pl.pallas_call(
        _mix_kernel,
        out_shape=jax.ShapeDtypeStruct(x.shape, x.dtype),
        grid=(batch, seq // tile),
        in_specs=[
            pl.BlockSpec((None, tile, D_MODEL), lambda b, i: (b, i, 0)),
            _resident((1, D_MODEL)),
            _resident((D_MODEL, IN_WIDTH)),
            _resident((N_POOL_GROUPS, POOL_GROUP_DIM, POOL_GROUP_DIM)),
            _resident((1, POOL_WIDTH)),
            smem,
            smem,
            _resident((2 * BLOCK, BLOCK)),
            _resident((D_MODEL, D_MODEL)),
            _resident((1, D_MODEL)),
        ],
        out_specs=pl.BlockSpec((None, tile, D_MODEL), lambda b, i: (b, i, 0)),
        scratch_shapes=[
            pltpu.VMEM((POOL_HALO + tile, POOL_WIDTH), _F32),
            pltpu.VMEM((n_blocks, GQA_GROUP * BLOCK, KV_WIDTH), _BF16),
            pltpu.VMEM((N_KV_HEADS, BLOCK + tile, KV_WIDTH), _BF16),
            pltpu.VMEM((KV_WIDTH, BLOCK + tile), _BF16),
            pltpu.VMEM((tile, D_MODEL), _BF16),
            pltpu.VMEM((2, N_KV_HEADS * 2 * BLOCK, GQA_GROUP * BLOCK), _F32),
        ],
        compiler_params=pltpu.CompilerParams(
            dimension_semantics=("arbitrary", "arbitrary"),
            vmem_limit_bytes=VMEM_LIMIT_BYTES,
        ),
        name="mix_layer",
    )(x, g_pre.reshape(1, D_MODEL), w_in_p, w_pool.astype(_BF16), pool_scale.reshape(1, POOL_WIDTH),
      rel_bias, sinks, bucket_t, w_out_p, g_post.reshape(1, D_MODEL))


def _ffn_layer(x, g_pre, w_gate, w_up, w_down, g_post):
    batch, seq, _ = x.shape
    tokens = batch * seq
    tile = FFN_TILE
    x2 = x.reshape(tokens, D_MODEL)
    out = pl.pallas_call(
        _ffn_kernel,
        out_shape=jax.ShapeDtypeStruct(x2.shape, x2.dtype),
        grid=(tokens // tile,),
        in_specs=[
            pl.BlockSpec((tile, D_MODEL), lambda i: (i, 0)),
            _resident((1, D_MODEL)),
            _resident((D_MODEL, D_FF)),
            _resident((D_MODEL, D_FF)),
            _resident((D_FF, D_MODEL)),
            _resident((1, D_MODEL)),
        ],
        out_specs=pl.BlockSpec((tile, D_MODEL), lambda i: (i, 0)),
        compiler_params=pltpu.CompilerParams(
            dimension_semantics=("arbitrary",),
            vmem_limit_bytes=VMEM_LIMIT_BYTES,
        ),
        name="ffn_layer",
    )(x2, g_pre.reshape(1, D_MODEL), w_gate.astype(_BF16), w_up.astype(_BF16), w_down.astype(_BF16),
      g_post.reshape(1, D_MODEL))
    return out.reshape(x.shape)


def kernel(x, g_pre_mix, w_in, w_pool, pool_scale, rel_bias, sinks, w_out, g_post_mix, g_pre_ffn, w_gate, w_up, w_down, g_post_ffn):
    depth = g_pre_mix.shape[0]
    for l in range(depth):
        x = _mix_layer(x, g_pre_mix[l], w_in[l], w_pool[l], pool_scale[l], rel_bias, sinks[l], w_out[l], g_post_mix[l])
        x = _ffn_layer(x, g_pre_ffn[l], w_gate[l], w_up[l], w_down[l], g_post_ffn[l])
    return x
```

```python
import functools

import numpy as np
import jax
import jax.numpy as jnp
from jax import lax
from jax.experimental import pallas as pl
from jax.experimental.pallas import tpu as pltpu

D_MODEL = 1024
POOL_WIDTH = 512
POOL_WINDOWS = (2, 4, 8, 16)
N_POOL_GROUPS = len(POOL_WINDOWS)
POOL_GROUP_DIM = POOL_WIDTH // N_POOL_GROUPS
ATTN_WIDTH = 512
HEAD_DIM = 64
N_Q_HEADS = 8
N_KV_HEADS = 2
GQA_GROUP = N_Q_HEADS // N_KV_HEADS
WINDOW = 128
BLOCK = 128
N_BUCKETS = 32
MAX_EXACT = N_BUCKETS // 2
MAX_DISTANCE = 128
KV_WIDTH = N_KV_HEADS * HEAD_DIM
IN_WIDTH = POOL_WIDTH + ATTN_WIDTH + 2 * KV_WIDTH
D_FF = 2816
EPS = 1e-6
NEG_INF = -1e30

POOL_HALO = 16
MIX_TILE = 512
FFN_TILE = 512
FFN_CHUNK = 256
VMEM_LIMIT_BYTES = 56 * 1024 * 1024

_F32 = jnp.float32
_BF16 = jnp.bfloat16


def _dot(a, b):
    return jnp.dot(a, b, preferred_element_type=_F32)


def _rmsnorm(xf, g):
    ms = jnp.mean(xf * xf, axis=-1, keepdims=True)
    return xf * lax.rsqrt(ms + EPS) * g


def _bucket_table_t():
    qi = np.arange(BLOCK)[None, :]
    kj = np.arange(2 * BLOCK)[:, None]
    dist = qi + BLOCK - kj
    n = np.maximum(dist, 0)
    nf = np.maximum(n, 1).astype(np.float32)
    large = MAX_EXACT + (
        np.log(nf / np.float32(MAX_EXACT)) / np.float32(np.log(MAX_DISTANCE / MAX_EXACT)) * np.float32(N_BUCKETS - MAX_EXACT)
    ).astype(np.int32)
    large = np.minimum(large, N_BUCKETS - 1)
    bucket = np.where(n < MAX_EXACT, n, large)
    in_window = (dist >= 0) & (dist < WINDOW)
    return np.where(in_window, bucket, -1).astype(np.int32)


def _mix_kernel(x_ref, gpre_ref, win_ref, wpool_ref, pscale_ref, relb_ref, sinks_ref, bkt_ref, wout_ref, gpost_ref,
                o_ref, u_buf, q_stack, kcat, vt, mixcat, bias_t):
    tile = x_ref.shape[0]
    n_blocks = tile // BLOCK
    b = pl.program_id(0)
    i = pl.program_id(1)

    @pl.when((b == 0) & (i == 0))
    def _build_bias():
        bkt = bkt_ref[...]
        key_is_prev = lax.broadcasted_iota(jnp.int32, bkt.shape, 0) < BLOCK
        for h in range(N_KV_HEADS):
            for g in range(GQA_GROUP):
                head = h * GQA_GROUP + g

                def body(bb, tab):
                    return jnp.where(bkt == bb, relb_ref[bb, head], tab)

                tab = lax.fori_loop(0, N_BUCKETS, body, jnp.zeros(bkt.shape, _F32))
                tab = jnp.where(bkt < 0, NEG_INF, tab)
                rows = slice(h * 2 * BLOCK, (h + 1) * 2 * BLOCK)
                cols = slice(g * BLOCK, (g + 1) * BLOCK)
                bias_t[0, rows, cols] = tab
                bias_t[1, rows, cols] = jnp.where(key_is_prev, NEG_INF, tab)

    @pl.when(i == 0)
    def _reset_history():
        u_buf[0:POOL_HALO, :] = jnp.zeros((POOL_HALO, POOL_WIDTH), _F32)
        kcat[:, 0:BLOCK, :] = jnp.zeros((N_KV_HEADS, BLOCK, KV_WIDTH), _BF16)
        vt[:, 0:BLOCK] = jnp.zeros((KV_WIDTH, BLOCK), _BF16)

    h_in = _rmsnorm(x_ref[...], gpre_ref[...]).astype(_BF16)

    u = _dot(h_in, win_ref[:, 0:POOL_WIDTH])
    u_buf[POOL_HALO:POOL_HALO + tile, :] = u
    q = _dot(h_in, win_ref[:, POOL_WIDTH:POOL_WIDTH + ATTN_WIDTH]).astype(_BF16)
    for j in range(n_blocks):
        for g in range(GQA_GROUP):
            q_stack[j, g * BLOCK:(g + 1) * BLOCK, :] = q[j * BLOCK:(j + 1) * BLOCK, g * BLOCK:(g + 1) * BLOCK]
    kv = _dot(h_in, win_ref[:, POOL_WIDTH + ATTN_WIDTH:IN_WIDTH])
    k = kv[:, 0:KV_WIDTH]
    v = kv[:, KV_WIDTH:2 * KV_WIDTH]
    lane = lax.broadcasted_iota(jnp.int32, k.shape, 1)
    kcat[0, BLOCK:BLOCK + tile, :] = jnp.where(lane < HEAD_DIM, k, 0.0).astype(_BF16)
    kcat[1, BLOCK:BLOCK + tile, :] = jnp.where(lane >= HEAD_DIM, k, 0.0).astype(_BF16)
    vt[:, BLOCK:BLOCK + tile] = v.T.astype(_BF16)

    pos = i * tile + lax.broadcasted_iota(jnp.int32, (tile, 1), 0)
    for g, w in enumerate(POOL_WINDOWS):
        cols = slice(g * POOL_GROUP_DIM, (g + 1) * POOL_GROUP_DIM)
        ug = u[:, cols]
        acc = u_buf[:, cols]
        lag = 1
        while lag < w:
            acc = acc + jnp.concatenate([acc[:lag], acc[:-lag]], axis=0)
            lag *= 2
        count = jnp.minimum(pos + 1, w).astype(_F32)
        pooled = acc[POOL_HALO:] / count - ug
        mixed = _dot(pooled.astype(_BF16), wpool_ref[g]) * pscale_ref[:, cols]
        mixcat[:, cols] = mixed.astype(_BF16)

    col = lax.broadcasted_iota(jnp.int32, (1, GQA_GROUP * BLOCK), 1)
    sink_rows = []
    for h in range(N_KV_HEADS):
        row = jnp.full((1, GQA_GROUP * BLOCK), sinks_ref[h * GQA_GROUP], _F32)
        for g in range(1, GQA_GROUP):
            row = jnp.where(col >= g * BLOCK, sinks_ref[h * GQA_GROUP + g], row)
        sink_rows.append(row)
    first_variant = jnp.where(i == 0, 1, 0)

    def logits(j):
        band = slice(j * BLOCK, (j + 2) * BLOCK)
        kb = jnp.concatenate([kcat[0, band, :], kcat[1, band, :]], axis=0)
        s = lax.dot_general(kb, q_stack[j], (((1,), (1,)), ((), ())), preferred_element_type=_F32)
        return s + (bias_t[first_variant] if j == 0 else bias_t[0])

    s_next = logits(0)
    for j in range(n_blocks):
        band = slice(j * BLOCK, (j + 2) * BLOCK)
        s = s_next
        if j + 1 < n_blocks:
            s_next = logits(j + 1)
        outs = []
        for h in range(N_KV_HEADS):
            sh = s[h * 2 * BLOCK:(h + 1) * 2 * BLOCK, :]
            m = jnp.maximum(jnp.max(sh, axis=0, keepdims=True), sink_rows[h])
            p = jnp.exp(sh - m)
            denom = jnp.sum(p, axis=0, keepdims=True) + jnp.exp(sink_rows[h] - m)
            out_h = _dot(vt[h * HEAD_DIM:(h + 1) * HEAD_DIM, band], p.astype(_BF16))
            outs.append(out_h * (1.0 / denom))
        o_t = jnp.concatenate(outs, axis=0)
        for g in range(GQA_GROUP):
            blk = o_t[:, g * BLOCK:(g + 1) * BLOCK].T
            mixcat[j * BLOCK:(j + 1) * BLOCK, POOL_WIDTH + g * BLOCK:POOL_WIDTH + (g + 1) * BLOCK] = blk.astype(_BF16)

    mix = _dot(mixcat[...], wout_ref[...])
    o_ref[...] = x_ref[...] + _rmsnorm(mix, gpost_ref[...])

    u_buf[0:POOL_HALO, :] = u_buf[tile:tile + POOL_HALO, :]
    kcat[:, 0:BLOCK, :] = kcat[:, tile:tile + BLOCK, :]
    vt[:, 0:BLOCK] = vt[:, tile:tile + BLOCK]


def _ffn_kernel(x_ref, gpre_ref, wg_ref, wu_ref, wd_ref, gpost_ref, o_ref):
    h_in = _rmsnorm(x_ref[...], gpre_ref[...]).astype(_BF16)
    acc = None
    for c in range(0, D_FF, FFN_CHUNK):
        gate = _dot(h_in, wg_ref[:, c:c + FFN_CHUNK])
        up = _dot(h_in, wu_ref[:, c:c + FFN_CHUNK])
        act = (gate * (1.0 / (1.0 + jnp.exp(-gate))) * up).astype(_BF16)
        part = _dot(act, wd_ref[c:c + FFN_CHUNK, :])
        acc = part if acc is None else acc + part
    o_ref[...] = x_ref[...] + _rmsnorm(acc, gpost_ref[...])


def _resident(shape):
    return pl.BlockSpec(shape, lambda *_: (0,) * len(shape), pipeline_mode=pl.Buffered(1))


def _mix_layer(x, g_pre, w_in, w_pool, pool_scale, rel_bias, sinks, w_out, g_post):
    batch, seq, _ = x.shape
    tile = MIX_TILE
    n_blocks = tile // BLOCK
    q0, q1 = POOL_WIDTH, POOL_WIDTH + ATTN_WIDTH
    w_q = w_in[:, q0:q1].reshape(D_MODEL, N_KV_HEADS, GQA_GROUP, HEAD_DIM).transpose(0, 2, 1, 3).reshape(D_MODEL, ATTN_WIDTH)
    w_in_p = jnp.concatenate([w_in[:, :q0], w_q * (HEAD_DIM ** -0.5), w_in[:, q1:]], axis=1).astype(_BF16)
    w_out_attn = w_out[POOL_WIDTH:].reshape(N_KV_HEADS, GQA_GROUP, HEAD_DIM, D_MODEL).transpose(1, 0, 2, 3).reshape(ATTN_WIDTH, D_MODEL)
    w_out_p = jnp.concatenate([w_out[:POOL_WIDTH], w_out_attn], axis=0).astype(_BF16)
    bucket_t = jnp.asarray(_bucket_table_t())

    smem = pl.BlockSpec(memory_space=pltpu.SMEM)
    return pl.pallas_call(
        _mix_kernel,
        out_shape=jax.ShapeDtypeStruct(x.shape, x.dtype),
        grid=(batch, seq // tile),
        in_specs=[
            pl.BlockSpec((None, tile, D_MODEL), lambda b, i: (b, i, 0)),
            _resident((1, D_MODEL)),
            _resident((D_MODEL, IN_WIDTH)),
            _resident((N_POOL_GROUPS, POOL_GROUP_DIM, POOL_GROUP_DIM)),
            _resident((1, POOL_WIDTH)),
            smem,
            smem,
            _resident((2 * BLOCK, BLOCK)),
            _resident((D_MODEL, D_MODEL)),
            _resident((1, D_MODEL)),
        ],
        out_specs=pl.BlockSpec((None, tile, D_MODEL), lambda b, i: (b, i, 0)),
        scratch_shapes=[
            pltpu.VMEM((POOL_HALO + tile, POOL_WIDTH), _F32),
            pltpu.VMEM((n_blocks, GQA_GROUP * BLOCK, KV_WIDTH), _BF16),
            pltpu.VMEM((N_KV_HEADS, BLOCK + tile, KV_WIDTH), _BF16),
            pltpu.VMEM((KV_WIDTH, BLOCK + tile), _BF16),
            pltpu.VMEM((tile, D_MODEL), _BF16),
            pltpu.VMEM((2, N_KV_HEADS * 2 * BLOCK, GQA_GROUP * BLOCK), _F32),
        ],
        compiler_params=pltpu.CompilerParams(
            dimension_semantics=("arbitrary", "arbitrary"),
            vmem_limit_bytes=VMEM_LIMIT_BYTES,
        ),
        name="mix_layer",
    )(x, g_pre.reshape(1, D_MODEL), w_in_p, w_pool.astype(_BF16), pool_scale.reshape(1, POOL_WIDTH),
      rel_bias, sinks, bucket_t, w_out_p, g_post.reshape(1, D_MODEL))


def _ffn_layer(x, g_pre, w_gate, w_up, w_down, g_post):
    batch, seq, _ = x.shape
    tokens = batch * seq
    tile = FFN_TILE
    x2 = x.reshape(tokens, D_MODEL)
    out = pl.pallas_call(
        _ffn_kernel,
        out_shape=jax.ShapeDtypeStruct(x2.shape, x2.dtype),
        grid=(tokens // tile,),
        in_specs=[
            pl.BlockSpec((tile, D_MODEL), lambda i: (i, 0)),
            _resident((1, D_MODEL)),
            _resident((D_MODEL, D_FF)),
            _resident((D_MODEL, D_FF)),
            _resident((D_FF, D_MODEL)),
            _resident((1, D_MODEL)),
        ],
        out_specs=pl.BlockSpec((tile, D_MODEL), lambda i: (i, 0)),
        compiler_params=pltpu.CompilerParams(
            dimension_semantics=("arbitrary",),
            vmem_limit_bytes=VMEM_LIMIT_BYTES,
        ),
        name="ffn_layer",
    )(x2, g_pre.reshape(1, D_MODEL), w_gate.astype(_BF16), w_up.astype(_BF16), w_down.astype(_BF16),
      g_post.reshape(1, D_MODEL))
    return out.reshape(x.shape)


def kernel(x, g_pre_mix, w_in, w_pool, pool_scale, rel_bias, sinks, w_out, g_post_mix, g_pre_ffn, w_gate, w_up, w_down, g_post_ffn):
    depth = g_pre_mix.shape[0]
    for l in range(depth):
        x = _mix_layer(x, g_pre_mix[l], w_in[l], w_pool[l], pool_scale[l], rel_bias, sinks[l], w_out[l], g_post_mix[l])
        x = _ffn_layer(x, g_pre_ffn[l], w_gate[l], w_up[l], w_down[l], g_post_ffn[l])
    return x
```

```python
import functools

import numpy as np
import jax
import jax.numpy as jnp
from jax import lax
from jax.experimental import pallas as pl
from jax.experimental.pallas import tpu as pltpu

D_MODEL = 1024
POOL_WIDTH = 512
POOL_WINDOWS = (2, 4, 8, 16)
N_POOL_GROUPS = len(POOL_WINDOWS)
POOL_GROUP_DIM = POOL_WIDTH // N_POOL_GROUPS
ATTN_WIDTH = 512
HEAD_DIM = 64
N_Q_HEADS = 8
N_KV_HEADS = 2
GQA_GROUP = N_Q_HEADS // N_KV_HEADS
WINDOW = 128
BLOCK = 128
N_BUCKETS = 32
MAX_EXACT = N_BUCKETS // 2
MAX_DISTANCE = 128
KV_WIDTH = N_KV_HEADS * HEAD_DIM
IN_WIDTH = POOL_WIDTH + ATTN_WIDTH + 2 * KV_WIDTH
D_FF = 2816
EPS = 1e-6
NEG_INF = -1e30

POOL_HALO = 16
TILE = 512
N_BLOCKS = TILE // BLOCK
FFN_CHUNK = 256
N_CHUNKS = D_FF // FFN_CHUNK
VMEM_LIMIT_BYTES = 60 * 1024 * 1024

_F32 = jnp.float32
_BF16 = jnp.bfloat16


def _dot(a, b):
    return jnp.dot(a, b, preferred_element_type=_F32)


def _rmsnorm(xf, g):
    ms = jnp.mean(xf * xf, axis=-1, keepdims=True)
    return xf * lax.rsqrt(ms + EPS) * g


def _bucket_table_t():
    qi = np.arange(BLOCK)[None, :]
    kj = np.arange(2 * BLOCK)[:, None]
    dist = qi + BLOCK - kj
    n = np.maximum(dist, 0)
    nf = np.maximum(n, 1).astype(np.float32)
    large = MAX_EXACT + (
        np.log(nf / np.float32(MAX_EXACT)) / np.float32(np.log(MAX_DISTANCE / MAX_EXACT)) * np.float32(N_BUCKETS - MAX_EXACT)
    ).astype(np.int32)
    large = np.minimum(large, N_BUCKETS - 1)
    bucket = np.where(n < MAX_EXACT, n, large)
    in_window = (dist >= 0) & (dist < WINDOW)
    return np.where(in_window, bucket, -1).astype(np.int32)


def _layer_kernel(tiles_per_seq, n_tiles,
                  xnext_ref, xprev_ref, win_ref, wpool_ref, pscale_ref, relb_ref, sinks_ref, bkt_ref, wout_ref,
                  gpost_ref, gpre2_ref, wg_ref, wu_ref, wd_ref, gpost2_ref,
                  o_ref, u_buf, q_stack, kcat, vt, mixcat, bias_t):
    step = pl.program_id(0)
    seq_tile = lax.rem(jnp.minimum(step, n_tiles - 1), tiles_per_seq)
    next_starts_seq = lax.rem(jnp.minimum(step + 1, n_tiles - 1), tiles_per_seq) == 0

    def stage_a_compute(x_ref):
        x = x_ref[...]
        inv_rms = lax.rsqrt(jnp.mean(x * x, axis=-1, keepdims=True) + EPS)
        xb = x.astype(_BF16)
        u = _dot(xb, win_ref[:, 0:POOL_WIDTH]) * inv_rms
        q = (_dot(xb, win_ref[:, POOL_WIDTH:POOL_WIDTH + ATTN_WIDTH]) * inv_rms).astype(_BF16)
        kv = _dot(xb, win_ref[:, POOL_WIDTH + ATTN_WIDTH:IN_WIDTH]) * inv_rms
        return u, q, kv

    def stage_a_store(u, q, kv, starts_seq):
        def history(tail):
            if starts_seq is None:
                return jnp.zeros(tail.shape, tail.dtype)
            return jnp.where(starts_seq, jnp.zeros(tail.shape, tail.dtype), tail[...])

        u_buf[0:POOL_HALO, :] = history(u_buf.at[TILE:TILE + POOL_HALO, :])
        u_buf[POOL_HALO:POOL_HALO + TILE, :] = u
        for j in range(N_BLOCKS):
            for g in range(GQA_GROUP):
                q_stack[j, g * BLOCK:(g + 1) * BLOCK, :] = q[j * BLOCK:(j + 1) * BLOCK, g * BLOCK:(g + 1) * BLOCK]
        k = kv[:, 0:KV_WIDTH]
        v = kv[:, KV_WIDTH:2 * KV_WIDTH]
        lane = lax.broadcasted_iota(jnp.int32, k.shape, 1)
        kcat[:, 0:BLOCK, :] = history(kcat.at[:, TILE:TILE + BLOCK, :])
        kcat[0, BLOCK:BLOCK + TILE, :] = jnp.where(lane < HEAD_DIM, k, 0.0).astype(_BF16)
        kcat[1, BLOCK:BLOCK + TILE, :] = jnp.where(lane >= HEAD_DIM, k, 0.0).astype(_BF16)
        vt[:, 0:BLOCK] = history(vt.at[:, TILE:TILE + BLOCK])
        vt[:, BLOCK:BLOCK + TILE] = v.T.astype(_BF16)

    @pl.when(step == 0)
    def _first_step():
        mixcat[...] = jnp.zeros(mixcat.shape, _BF16)
        stage_a_store(*stage_a_compute(xprev_ref), None)
        bkt = bkt_ref[...]
        key_is_prev = lax.broadcasted_iota(jnp.int32, bkt.shape, 0) < BLOCK
        for h in range(N_KV_HEADS):
            for g in range(GQA_GROUP):
                head = h * GQA_GROUP + g

                def body(bb, tab):
                    return jnp.where(bkt == bb, relb_ref[bb, head], tab)

                tab = lax.fori_loop(0, N_BUCKETS, body, jnp.zeros(bkt.shape, _F32))
                tab = jnp.where(bkt < 0, NEG_INF, tab)
                rows = slice(h * 2 * BLOCK, (h + 1) * 2 * BLOCK)
                cols = slice(g * BLOCK, (g + 1) * BLOCK)
                bias_t[0, rows, cols] = tab
                bias_t[1, rows, cols] = jnp.where(key_is_prev, NEG_INF, tab)

    def ffn_gate_up(h2, c):
        cols = slice(c * FFN_CHUNK, (c + 1) * FFN_CHUNK)
        gate = _dot(h2, wg_ref[:, cols])
        up = _dot(h2, wu_ref[:, cols])
        return (gate * (1.0 / (1.0 + jnp.exp(-gate))) * up).astype(_BF16)

    def ffn_down(c, act, rows=slice(None)):
        return _dot(act[rows], wd_ref[c * FFN_CHUNK:(c + 1) * FFN_CHUNK, :])

    pos = seq_tile * TILE + lax.broadcasted_iota(jnp.int32, (TILE, 1), 0)

    def pool_group(g):
        w = POOL_WINDOWS[g]
        cols = slice(g * POOL_GROUP_DIM, (g + 1) * POOL_GROUP_DIM)
        ext = u_buf[:, cols]
        acc = ext
        lag = 1
        while lag < w:
            acc = acc + jnp.concatenate([acc[:lag], acc[:-lag]], axis=0)
            lag *= 2
        count = jnp.minimum(pos + 1, w).astype(_F32)
        pooled = acc[POOL_HALO:] / count - ext[POOL_HALO:]
        mixed = _dot(pooled.astype(_BF16), wpool_ref[g]) * pscale_ref[:, cols]
        mixcat[:, cols] = mixed.astype(_BF16)

    col = lax.broadcasted_iota(jnp.int32, (1, GQA_GROUP * BLOCK), 1)
    sink_rows = []
    for h in range(N_KV_HEADS):
        row = jnp.full((1, GQA_GROUP * BLOCK), sinks_ref[h * GQA_GROUP], _F32)
        for g in range(1, GQA_GROUP):
            row = jnp.where(col >= g * BLOCK, sinks_ref[h * GQA_GROUP + g], row)
        sink_rows.append(row)
    first_variant = jnp.where(seq_tile == 0, 1, 0)

    def logits(j):
        band = slice(j * BLOCK, (j + 2) * BLOCK)
        kb = jnp.concatenate([kcat[0, band, :], kcat[1, band, :]], axis=0)
        s = lax.dot_general(kb, q_stack[j], (((1,), (1,)), ((), ())), preferred_element_type=_F32)
        return s + (bias_t[first_variant] if j == 0 else bias_t[0])

    def attend(j, s):
        band = slice(j * BLOCK, (j + 2) * BLOCK)
        outs = []
        for h in range(N_KV_HEADS):
            sh = s[h * 2 * BLOCK:(h + 1) * 2 * BLOCK, :]
            m = jnp.maximum(jnp.max(sh, axis=0, keepdims=True), sink_rows[h])
            p = jnp.exp(sh - m)
            denom = jnp.sum(p, axis=0, keepdims=True) + jnp.exp(sink_rows[h] - m)
            out_h = _dot(vt[h * HEAD_DIM:(h + 1) * HEAD_DIM, band], p.astype(_BF16))
            outs.append(out_h * (1.0 / denom))
        o_t = jnp.concatenate(outs, axis=0)
        blks = []
        for g in range(GQA_GROUP):
            blks.append(o_t[:, g * BLOCK:(g + 1) * BLOCK].T.astype(_BF16))
        return jnp.concatenate(blks, axis=1)

    mix = _dot(mixcat[...], wout_ref[...])
    next_proj = stage_a_compute(xnext_ref)
    x1 = xprev_ref[...] + _rmsnorm(mix, gpost_ref[...])
    h2 = _rmsnorm(x1, gpre2_ref[...]).astype(_BF16)
    acts = {0: ffn_gate_up(h2, 0)}
    s_blocks = {0: logits(0)}
    pool_group(0)
    pool_group(1)
    acc = None
    for c in range(N_CHUNKS - 1):
        acts[c + 1] = ffn_gate_up(h2, c + 1)
        part = ffn_down(c, acts.pop(c))
        acc = part if acc is None else acc + part
        if c < N_BLOCKS:
            if c + 1 < N_BLOCKS:
                s_blocks[c + 1] = logits(c + 1)
            mixcat[c * BLOCK:(c + 1) * BLOCK, POOL_WIDTH:] = attend(c, s_blocks.pop(c))
        if c == 0:
            pool_group(2)
            pool_group(3)
    act = acts.pop(N_CHUNKS - 1)
    for r in range(N_BLOCKS):
        rows = slice(r * BLOCK, (r + 1) * BLOCK)
        f = acc[rows] + ffn_down(N_CHUNKS - 1, act, rows)
        o_ref[rows, :] = x1[rows] + _rmsnorm(f, gpost2_ref[...])
    stage_a_store(*next_proj, next_starts_seq)


def _resident(shape):
    return pl.BlockSpec(shape, lambda *_: (0,) * len(shape), pipeline_mode=pl.Buffered(1))


def _layer(x, g_pre_mix, w_in, w_pool, pool_scale, rel_bias, sinks, w_out, g_post_mix, g_pre_ffn, w_gate, w_up, w_down,
           g_post_ffn):
    batch, seq, _ = x.shape
    tokens = batch * seq
    n_tiles = tokens // TILE
    tiles_per_seq = seq // TILE
    q0, q1 = POOL_WIDTH, POOL_WIDTH + ATTN_WIDTH
    w_q = w_in[:, q0:q1].reshape(D_MODEL, N_KV_HEADS, GQA_GROUP, HEAD_DIM).transpose(0, 2, 1, 3).reshape(D_MODEL, ATTN_WIDTH)
    w_in_p = jnp.concatenate([w_in[:, :q0], w_q * (HEAD_DIM ** -0.5), w_in[:, q1:]], axis=1)
    w_in_p = (g_pre_mix[:, None] * w_in_p).astype(_BF16)
    w_out_attn = w_out[POOL_WIDTH:].reshape(N_KV_HEADS, GQA_GROUP, HEAD_DIM, D_MODEL).transpose(1, 0, 2, 3).reshape(ATTN_WIDTH, D_MODEL)
    w_out_p = jnp.concatenate([w_out[:POOL_WIDTH], w_out_attn], axis=0).astype(_BF16)
    bucket_t = jnp.asarray(_bucket_table_t())
    x2 = x.reshape(tokens, D_MODEL)
    row = lambda a: a.reshape(1, -1)

    smem = pl.BlockSpec(memory_space=pltpu.SMEM)
    out = pl.pallas_call(
        functools.partial(_layer_kernel, tiles_per_seq, n_tiles),
        out_shape=jax.ShapeDtypeStruct(x2.shape, x2.dtype),
        grid=(n_tiles + 1,),
        in_specs=[
            pl.BlockSpec((TILE, D_MODEL), lambda s: (jnp.minimum(s + 1, n_tiles - 1), 0)),
            pl.BlockSpec((TILE, D_MODEL), lambda s: (jnp.maximum(s - 1, 0), 0)),
            _resident((D_MODEL, IN_WIDTH)),
            _resident((N_POOL_GROUPS, POOL_GROUP_DIM, POOL_GROUP_DIM)),
            _resident((1, POOL_WIDTH)),
            smem,
            smem,
            _resident((2 * BLOCK, BLOCK)),
            _resident((D_MODEL, D_MODEL)),
            _resident((1, D_MODEL)),
            _resident((1, D_MODEL)),
            _resident((D_MODEL, D_FF)),
            _resident((D_MODEL, D_FF)),
            _resident((D_FF, D_MODEL)),
            _resident((1, D_MODEL)),
        ],
        out_specs=pl.BlockSpec((TILE, D_MODEL), lambda s: (jnp.maximum(s - 1, 0), 0)),
        scratch_shapes=[
            pltpu.VMEM((POOL_HALO + TILE, POOL_WIDTH), _F32),
            pltpu.VMEM((N_BLOCKS, GQA_GROUP * BLOCK, KV_WIDTH), _BF16),
            pltpu.VMEM((N_KV_HEADS, BLOCK + TILE, KV_WIDTH), _BF16),
            pltpu.VMEM((KV_WIDTH, BLOCK + TILE), _BF16),
            pltpu.VMEM((TILE, D_MODEL), _BF16),
            pltpu.VMEM((2, N_KV_HEADS * 2 * BLOCK, GQA_GROUP * BLOCK), _F32),
        ],
        compiler_params=pltpu.CompilerParams(
            dimension_semantics=("arbitrary",),
            vmem_limit_bytes=VMEM_LIMIT_BYTES,
        ),
        name="hybrid_layer",
    )(x2, x2, w_in_p, w_pool.astype(_BF16), row(pool_scale), rel_bias, sinks, bucket_t, w_out_p,
      row(g_post_mix), row(g_pre_ffn), w_gate.astype(_BF16), w_up.astype(_BF16), w_down.astype(_BF16), row(g_post_ffn))
    return out.reshape(x.shape)


def kernel(x, g_pre_mix, w_in, w_pool, pool_scale, rel_bias, sinks, w_out, g_post_mix, g_pre_ffn, w_gate, w_up, w_down, g_post_ffn):
    depth = g_pre_mix.shape[0]
    for l in range(depth):
        x = _layer(x, g_pre_mix[l], w_in[l], w_pool[l], pool_scale[l], rel_bias, sinks[l], w_out[l], g_post_mix[l],
                   g_pre_ffn[l], w_gate[l], w_up[l], w_down[l], g_post_ffn[l])
    return x
```

```python
import functools

import numpy as np
import jax
import jax.numpy as jnp
from jax import lax
from jax.experimental import pallas as pl
from jax.experimental.pallas import tpu as pltpu

D_MODEL = 1024
POOL_WIDTH = 512
POOL_WINDOWS = (2, 4, 8, 16)
N_POOL_GROUPS = len(POOL_WINDOWS)
POOL_GROUP_DIM = POOL_WIDTH // N_POOL_GROUPS
ATTN_WIDTH = 512
HEAD_DIM = 64
N_Q_HEADS = 8
N_KV_HEADS = 2
GQA_GROUP = N_Q_HEADS // N_KV_HEADS
WINDOW = 128
BLOCK = 128
N_BUCKETS = 32
MAX_EXACT = N_BUCKETS // 2
MAX_DISTANCE = 128
KV_WIDTH = N_KV_HEADS * HEAD_DIM
IN_WIDTH = POOL_WIDTH + ATTN_WIDTH + 2 * KV_WIDTH
D_FF = 2816
EPS = 1e-6
NEG_INF = -1e30

POOL_HALO = 16
TILE = 512
N_BLOCKS = TILE // BLOCK
FFN_CHUNK = 256
N_CHUNKS = D_FF // FFN_CHUNK
STAGE_A_AFTER_CHUNK = 8
VMEM_LIMIT_BYTES = 60 * 1024 * 1024

_F32 = jnp.float32
_BF16 = jnp.bfloat16


def _dot(a, b):
    return jnp.dot(a, b, preferred_element_type=_F32)


def _rmsnorm(xf, g):
    ms = jnp.mean(xf * xf, axis=-1, keepdims=True)
    return xf * lax.rsqrt(ms + EPS) * g


def _bucket_table_t():
    qi = np.arange(BLOCK)[None, :]
    kj = np.arange(2 * BLOCK)[:, None]
    dist = qi + BLOCK - kj
    n = np.maximum(dist, 0)
    nf = np.maximum(n, 1).astype(np.float32)
    large = MAX_EXACT + (
        np.log(nf / np.float32(MAX_EXACT)) / np.float32(np.log(MAX_DISTANCE / MAX_EXACT)) * np.float32(N_BUCKETS - MAX_EXACT)
    ).astype(np.int32)
    large = np.minimum(large, N_BUCKETS - 1)
    bucket = np.where(n < MAX_EXACT, n, large)
    in_window = (dist >= 0) & (dist < WINDOW)
    return np.where(in_window, bucket, -1).astype(np.int32)


def _layer_kernel(tiles_per_seq, n_tiles,
                  xnext_ref, xprev_ref, win_ref, wpool_ref, pscale_ref, relb_ref, sinks_ref, bkt_ref, wout_ref,
                  gpost_ref, wg_ref, wu_ref, wd_ref, gpost2_ref,
                  o_ref, u_buf, q_stack, kcat, vt, mixcat, bias_t):
    step = pl.program_id(0)
    seq_tile = lax.rem(jnp.minimum(step, n_tiles - 1), tiles_per_seq)
    next_starts_seq = lax.rem(jnp.minimum(step + 1, n_tiles - 1), tiles_per_seq) == 0

    def stage_a_compute(x_ref):
        x = x_ref[...]
        inv_rms = lax.rsqrt(jnp.mean(x * x, axis=-1, keepdims=True) + EPS)
        xb = x.astype(_BF16)
        u = _dot(xb, win_ref[:, 0:POOL_WIDTH]) * inv_rms
        q = (_dot(xb, win_ref[:, POOL_WIDTH:POOL_WIDTH + ATTN_WIDTH]) * inv_rms).astype(_BF16)
        kv = _dot(xb, win_ref[:, POOL_WIDTH + ATTN_WIDTH:IN_WIDTH]) * inv_rms
        return u, q, kv

    def stage_a_store(u, q, kv, starts_seq):
        def history(tail):
            if starts_seq is None:
                return jnp.zeros(tail.shape, tail.dtype)
            return jnp.where(starts_seq, jnp.zeros(tail.shape, tail.dtype), tail[...])

        u_buf[0:POOL_HALO, :] = history(u_buf.at[TILE:TILE + POOL_HALO, :])
        u_buf[POOL_HALO:POOL_HALO + TILE, :] = u
        for j in range(N_BLOCKS):
            for g in range(GQA_GROUP):
                q_stack[j, g * BLOCK:(g + 1) * BLOCK, :] = q[j * BLOCK:(j + 1) * BLOCK, g * BLOCK:(g + 1) * BLOCK]
        k = kv[:, 0:KV_WIDTH]
        v = kv[:, KV_WIDTH:2 * KV_WIDTH]
        lane = lax.broadcasted_iota(jnp.int32, k.shape, 1)
        kcat[:, 0:BLOCK, :] = history(kcat.at[:, TILE:TILE + BLOCK, :])
        kcat[0, BLOCK:BLOCK + TILE, :] = jnp.where(lane < HEAD_DIM, k, 0.0).astype(_BF16)
        kcat[1, BLOCK:BLOCK + TILE, :] = jnp.where(lane >= HEAD_DIM, k, 0.0).astype(_BF16)
        vt[:, 0:BLOCK] = history(vt.at[:, TILE:TILE + BLOCK])
        vt[:, BLOCK:BLOCK + TILE] = v.T.astype(_BF16)

    @pl.when(step == 0)
    def _first_step():
        mixcat[...] = jnp.zeros(mixcat.shape, _BF16)
        stage_a_store(*stage_a_compute(xprev_ref), None)
        bkt = bkt_ref[...]
        key_is_prev = lax.broadcasted_iota(jnp.int32, bkt.shape, 0) < BLOCK
        for h in range(N_KV_HEADS):
            for g in range(GQA_GROUP):
                head = h * GQA_GROUP + g

                def body(bb, tab):
                    return jnp.where(bkt == bb, relb_ref[bb, head], tab)

                tab = lax.fori_loop(0, N_BUCKETS, body, jnp.zeros(bkt.shape, _F32))
                tab = jnp.where(bkt < 0, NEG_INF, tab)
                rows = slice(h * 2 * BLOCK, (h + 1) * 2 * BLOCK)
                cols = slice(g * BLOCK, (g + 1) * BLOCK)
                bias_t[0, rows, cols] = tab
                bias_t[1, rows, cols] = jnp.where(key_is_prev, NEG_INF, tab)

    def ffn_gate_up(h2, inv_rms2, c):
        cols = slice(c * FFN_CHUNK, (c + 1) * FFN_CHUNK)
        gate = _dot(h2, wg_ref[:, cols]) * inv_rms2
        up = _dot(h2, wu_ref[:, cols]) * inv_rms2
        return (gate * (1.0 / (1.0 + jnp.exp(-gate))) * up).astype(_BF16)

    def ffn_down(c, act, rows=slice(None)):
        return _dot(act[rows], wd_ref[c * FFN_CHUNK:(c + 1) * FFN_CHUNK, :])

    pos = seq_tile * TILE + lax.broadcasted_iota(jnp.int32, (TILE, 1), 0)

    def pool_group(g):
        w = POOL_WINDOWS[g]
        cols = slice(g * POOL_GROUP_DIM, (g + 1) * POOL_GROUP_DIM)
        ext = u_buf[:, cols]
        acc = ext
        lag = 1
        while lag < w:
            acc = acc + jnp.concatenate([acc[:lag], acc[:-lag]], axis=0)
            lag *= 2
        count = jnp.minimum(pos + 1, w).astype(_F32)
        pooled = acc[POOL_HALO:] / count - ext[POOL_HALO:]
        mixed = _dot(pooled.astype(_BF16), wpool_ref[g]) * pscale_ref[:, cols]
        mixcat[:, cols] = mixed.astype(_BF16)

    col = lax.broadcasted_iota(jnp.int32, (1, GQA_GROUP * BLOCK), 1)
    sink_rows = []
    for h in range(N_KV_HEADS):
        row = jnp.full((1, GQA_GROUP * BLOCK), sinks_ref[h * GQA_GROUP], _F32)
        for g in range(1, GQA_GROUP):
            row = jnp.where(col >= g * BLOCK, sinks_ref[h * GQA_GROUP + g], row)
        sink_rows.append(row)
    first_variant = jnp.where(seq_tile == 0, 1, 0)

    def logits(j):
        band = slice(j * BLOCK, (j + 2) * BLOCK)
        kb = jnp.concatenate([kcat[0, band, :], kcat[1, band, :]], axis=0)
        s = lax.dot_general(kb, q_stack[j], (((1,), (1,)), ((), ())), preferred_element_type=_F32)
        return s + (bias_t[first_variant] if j == 0 else bias_t[0])

    def attend(j, s):
        band = slice(j * BLOCK, (j + 2) * BLOCK)
        outs = []
        for h in range(N_KV_HEADS):
            sh = s[h * 2 * BLOCK:(h + 1) * 2 * BLOCK, :]
            m = jnp.maximum(jnp.max(sh, axis=0, keepdims=True), sink_rows[h])
            p = jnp.exp(sh - m)
            denom = jnp.sum(p, axis=0, keepdims=True) + jnp.exp(sink_rows[h] - m)
            out_h = _dot(vt[h * HEAD_DIM:(h + 1) * HEAD_DIM, band], p.astype(_BF16))
            outs.append(out_h * (1.0 / denom))
        o_t = jnp.concatenate(outs, axis=0)
        blks = []
        for g in range(GQA_GROUP):
            blks.append(o_t[:, g * BLOCK:(g + 1) * BLOCK].T.astype(_BF16))
        return jnp.concatenate(blks, axis=1)

    half = TILE // 2
    x1_halves = []
    for r in range(2):
        rows = slice(r * half, (r + 1) * half)
        mix = _dot(mixcat[rows, :], wout_ref[...])
        x1_halves.append(xprev_ref[rows, :] + _rmsnorm(mix, gpost_ref[...]))
    s_blocks = [logits(j) for j in range(N_BLOCKS)]
    x1 = jnp.concatenate(x1_halves, axis=0)
    inv_rms2 = lax.rsqrt(jnp.mean(x1 * x1, axis=-1, keepdims=True) + EPS)
    h2 = x1.astype(_BF16)
    acts = {0: ffn_gate_up(h2, inv_rms2, 0)}
    acc = None
    next_proj = None
    for c in range(N_CHUNKS - 1):
        acts[c + 1] = ffn_gate_up(h2, inv_rms2, c + 1)
        part = ffn_down(c, acts.pop(c))
        acc = part if acc is None else acc + part
        if c < N_BLOCKS:
            mixcat[c * BLOCK:(c + 1) * BLOCK, POOL_WIDTH:] = attend(c, s_blocks[c])
        elif c < N_BLOCKS + N_POOL_GROUPS:
            pool_group(c - N_BLOCKS)
        elif c == STAGE_A_AFTER_CHUNK:
            next_proj = stage_a_compute(xnext_ref)
    act = acts.pop(N_CHUNKS - 1)
    for r in range(N_BLOCKS):
        rows = slice(r * BLOCK, (r + 1) * BLOCK)
        f = acc[rows] + ffn_down(N_CHUNKS - 1, act, rows)
        o_ref[rows, :] = x1[rows] + _rmsnorm(f, gpost2_ref[...])
    if next_proj is None:
        next_proj = stage_a_compute(xnext_ref)
    stage_a_store(*next_proj, next_starts_seq)


def _resident(shape):
    return pl.BlockSpec(shape, lambda *_: (0,) * len(shape), pipeline_mode=pl.Buffered(1))


def _layer(x, g_pre_mix, w_in, w_pool, pool_scale, rel_bias, sinks, w_out, g_post_mix, g_pre_ffn, w_gate, w_up, w_down,
           g_post_ffn):
    batch, seq, _ = x.shape
    tokens = batch * seq
    n_tiles = tokens // TILE
    tiles_per_seq = seq // TILE
    q0, q1 = POOL_WIDTH, POOL_WIDTH + ATTN_WIDTH
    w_q = w_in[:, q0:q1].reshape(D_MODEL, N_KV_HEADS, GQA_GROUP, HEAD_DIM).transpose(0, 2, 1, 3).reshape(D_MODEL, ATTN_WIDTH)
    w_in_p = jnp.concatenate([w_in[:, :q0], w_q * (HEAD_DIM ** -0.5), w_in[:, q1:]], axis=1)
    w_in_p = (g_pre_mix[:, None] * w_in_p).astype(_BF16)
    w_out_attn = w_out[POOL_WIDTH:].reshape(N_KV_HEADS, GQA_GROUP, HEAD_DIM, D_MODEL).transpose(1, 0, 2, 3).reshape(ATTN_WIDTH, D_MODEL)
    w_out_p = jnp.concatenate([w_out[:POOL_WIDTH], w_out_attn], axis=0).astype(_BF16)
    bucket_t = jnp.asarray(_bucket_table_t())
    x2 = x.reshape(tokens, D_MODEL)
    row = lambda a: a.reshape(1, -1)

    smem = pl.BlockSpec(memory_space=pltpu.SMEM)
    out = pl.pallas_call(
        functools.partial(_layer_kernel, tiles_per_seq, n_tiles),
        out_shape=jax.ShapeDtypeStruct(x2.shape, x2.dtype),
        grid=(n_tiles + 1,),
        in_specs=[
            pl.BlockSpec((TILE, D_MODEL), lambda s: (jnp.minimum(s + 1, n_tiles - 1), 0)),
            pl.BlockSpec((TILE, D_MODEL), lambda s: (jnp.maximum(s - 1, 0), 0)),
            _resident((D_MODEL, IN_WIDTH)),
            _resident((N_POOL_GROUPS, POOL_GROUP_DIM, POOL_GROUP_DIM)),
            _resident((1, POOL_WIDTH)),
            smem,
            smem,
            _resident((2 * BLOCK, BLOCK)),
            _resident((D_MODEL, D_MODEL)),
            _resident((1, D_MODEL)),
            _resident((D_MODEL, D_FF)),
            _resident((D_MODEL, D_FF)),
            _resident((D_FF, D_MODEL)),
            _resident((1, D_MODEL)),
        ],
        out_specs=pl.BlockSpec((TILE, D_MODEL), lambda s: (jnp.maximum(s - 1, 0), 0)),
        scratch_shapes=[
            pltpu.VMEM((POOL_HALO + TILE, POOL_WIDTH), _F32),
            pltpu.VMEM((N_BLOCKS, GQA_GROUP * BLOCK, KV_WIDTH), _BF16),
            pltpu.VMEM((N_KV_HEADS, BLOCK + TILE, KV_WIDTH), _BF16),
            pltpu.VMEM((KV_WIDTH, BLOCK + TILE), _BF16),
            pltpu.VMEM((TILE, D_MODEL), _BF16),
            pltpu.VMEM((2, N_KV_HEADS * 2 * BLOCK, GQA_GROUP * BLOCK), _F32),
        ],
        compiler_params=pltpu.CompilerParams(
            dimension_semantics=("arbitrary",),
            vmem_limit_bytes=VMEM_LIMIT_BYTES,
        ),
        name="hybrid_layer",
    )(x2, x2, w_in_p, w_pool.astype(_BF16), row(pool_scale), rel_bias, sinks, bucket_t, w_out_p,
      row(g_post_mix), (g_pre_ffn[:, None] * w_gate).astype(_BF16), (g_pre_ffn[:, None] * w_up).astype(_BF16),
      w_down.astype(_BF16), row(g_post_ffn))
    return out.reshape(x.shape)


def kernel(x, g_pre_mix, w_in, w_pool, pool_scale, rel_bias, sinks, w_out, g_post_mix, g_pre_ffn, w_gate, w_up, w_down, g_post_ffn):
    depth = g_pre_mix.shape[0]
    for l in range(depth):
        x = _layer(x, g_pre_mix[l], w_in[l], w_pool[l], pool_scale[l], rel_bias, sinks[l], w_out[l], g_post_mix[l],
                   g_pre_ffn[l], w_gate[l], w_up[l], w_down[l], g_post_ffn[l])
    return x
```

```python
import functools

import numpy as np
import jax
import jax.numpy as jnp
from jax import lax
from jax.experimental import pallas as pl
from jax.experimental.pallas import tpu as pltpu

D_MODEL = 1024
POOL_WIDTH = 512
POOL_WINDOWS = (2, 4, 8, 16)
N_POOL_GROUPS = len(POOL_WINDOWS)
POOL_GROUP_DIM = POOL_WIDTH // N_POOL_GROUPS
ATTN_WIDTH = 512
HEAD_DIM = 64
N_Q_HEADS = 8
N_KV_HEADS = 2
GQA_GROUP = N_Q_HEADS // N_KV_HEADS
WINDOW = 128
BLOCK = 128
N_BUCKETS = 32
MAX_EXACT = N_BUCKETS // 2
MAX_DISTANCE = 128
KV_WIDTH = N_KV_HEADS * HEAD_DIM
IN_WIDTH = POOL_WIDTH + ATTN_WIDTH + 2 * KV_WIDTH
D_FF = 2816
EPS = 1e-6
NEG_INF = -1e30

POOL_HALO = 16
TILE = 512
N_BLOCKS = TILE // BLOCK
FFN_CHUNK = 256
N_CHUNKS = D_FF // FFN_CHUNK
STAGE_A_AFTER_CHUNK = 8
WEIGHT_STAGE_ROWS = 128
WOUT_BLOCK_ROWS = HEAD_DIM
VMEM_LIMIT_BYTES = 60 * 1024 * 1024

_F32 = jnp.float32
_BF16 = jnp.bfloat16


def _dot(a, b):
    return jnp.dot(a, b, preferred_element_type=_F32)


def _rmsnorm(xf, g):
    ms = jnp.mean(xf * xf, axis=-1, keepdims=True)
    return xf * lax.rsqrt(ms + EPS) * g


def _bucket_table_t():
    qi = np.arange(BLOCK)[None, :]
    kj = np.arange(2 * BLOCK)[:, None]
    dist = qi + BLOCK - kj
    n = np.maximum(dist, 0)
    nf = np.maximum(n, 1).astype(np.float32)
    large = MAX_EXACT + (
        np.log(nf / np.float32(MAX_EXACT)) / np.float32(np.log(MAX_DISTANCE / MAX_EXACT)) * np.float32(N_BUCKETS - MAX_EXACT)
    ).astype(np.int32)
    large = np.minimum(large, N_BUCKETS - 1)
    bucket = np.where(n < MAX_EXACT, n, large)
    in_window = (dist >= 0) & (dist < WINDOW)
    return np.where(in_window, bucket, -1).astype(np.int32)


def _layer_kernel(tiles_per_seq, n_tiles,
                  xnext_ref, xprev_ref, win_hbm, wpool_f32_ref, pscale_ref, relb_ref, sinks_ref, bkt_ref, wout_hbm,
                  gpost_ref, wg_hbm, wu_hbm, wd_hbm, gpost2_ref, gin_col_ref, gffn_col_ref,
                  o_ref, u_buf, q_stack, kcat, vt, mixcat, bias_t,
                  win_ref, wpool_ref, wout_ref, wg_ref, wu_ref, wd_ref, stage, stage_sem):
    step = pl.program_id(0)
    seq_tile = lax.rem(jnp.minimum(step, n_tiles - 1), tiles_per_seq)
    next_starts_seq = lax.rem(jnp.minimum(step + 1, n_tiles - 1), tiles_per_seq) == 0

    def stage_a_compute(x_ref):
        x = x_ref[...]
        inv_rms = lax.rsqrt(jnp.mean(x * x, axis=-1, keepdims=True) + EPS)
        xb = x.astype(_BF16)
        u = _dot(xb, win_ref[:, 0:POOL_WIDTH]) * inv_rms
        q = (_dot(xb, win_ref[:, POOL_WIDTH:POOL_WIDTH + ATTN_WIDTH]) * inv_rms).astype(_BF16)
        kv = _dot(xb, win_ref[:, POOL_WIDTH + ATTN_WIDTH:IN_WIDTH]) * inv_rms
        return u, q, kv

    def stage_a_store(u, q, kv, starts_seq):
        def history(tail):
            if starts_seq is None:
                return jnp.zeros(tail.shape, tail.dtype)
            return jnp.where(starts_seq, jnp.zeros(tail.shape, tail.dtype), tail[...])

        u_buf[0:POOL_HALO, :] = history(u_buf.at[TILE:TILE + POOL_HALO, :])
        u_buf[POOL_HALO:POOL_HALO + TILE, :] = u
        for j in range(N_BLOCKS):
            for g in range(GQA_GROUP):
                q_stack[j, g * BLOCK:(g + 1) * BLOCK, :] = q[j * BLOCK:(j + 1) * BLOCK, g * BLOCK:(g + 1) * BLOCK]
        k = kv[:, 0:KV_WIDTH]
        v = kv[:, KV_WIDTH:2 * KV_WIDTH]
        lane = lax.broadcasted_iota(jnp.int32, k.shape, 1)
        kcat[:, 0:BLOCK, :] = history(kcat.at[:, TILE:TILE + BLOCK, :])
        kcat[0, BLOCK:BLOCK + TILE, :] = jnp.where(lane < HEAD_DIM, k, 0.0).astype(_BF16)
        kcat[1, BLOCK:BLOCK + TILE, :] = jnp.where(lane >= HEAD_DIM, k, 0.0).astype(_BF16)
        vt[:, 0:BLOCK] = history(vt.at[:, TILE:TILE + BLOCK])
        vt[:, BLOCK:BLOCK + TILE] = v.T.astype(_BF16)

    def prepare_weights():
        def stream(src, rows_per_chunk, n_chunks, width, src_row, consume):
            def dma(i, slot):
                return pltpu.make_async_copy(src.at[pl.ds(src_row(i), rows_per_chunk), :],
                                             stage.at[slot, 0:rows_per_chunk, 0:width], stage_sem.at[slot])

            dma(0, 0).start()

            def body(i, carry):
                slot = lax.rem(i, 2)

                @pl.when(i + 1 < n_chunks)
                def _prefetch():
                    dma(i + 1, 1 - slot).start()

                dma(i, slot).wait()
                consume(i, stage[slot, 0:rows_per_chunk, 0:width])
                return carry

            lax.fori_loop(0, n_chunks, body, 0)

        rows = WEIGHT_STAGE_ROWS

        def chunk_rows(i):
            return pl.ds(pl.multiple_of(i * rows, rows), rows)

        def consume_in(i, blk):
            qb = blk[:, POOL_WIDTH:POOL_WIDTH + ATTN_WIDTH]
            heads = [qb[:, (h * GQA_GROUP + g) * HEAD_DIM:(h * GQA_GROUP + g + 1) * HEAD_DIM]
                     for g in range(GQA_GROUP) for h in range(N_KV_HEADS)]
            q_regrouped = jnp.concatenate(heads, axis=1) * (HEAD_DIM ** -0.5)
            full = jnp.concatenate([blk[:, :POOL_WIDTH], q_regrouped, blk[:, POOL_WIDTH + ATTN_WIDTH:]], axis=1)
            win_ref[chunk_rows(i), :] = (gin_col_ref[chunk_rows(i), :] * full).astype(_BF16)

        def consume_gated(dst):
            def consume(i, blk):
                dst[chunk_rows(i), :] = (gffn_col_ref[chunk_rows(i), :] * blk).astype(_BF16)
            return consume

        def consume_down(i, blk):
            wd_ref[chunk_rows(i), :] = blk.astype(_BF16)

        def wout_src_row(i):
            j = i - POOL_WIDTH // WOUT_BLOCK_ROWS
            head = (j % N_KV_HEADS) * GQA_GROUP + j // N_KV_HEADS
            return jnp.where(j < 0, i, POOL_WIDTH // WOUT_BLOCK_ROWS + head) * WOUT_BLOCK_ROWS

        def consume_out(i, blk):
            dst = pl.ds(pl.multiple_of(i * WOUT_BLOCK_ROWS, WOUT_BLOCK_ROWS), WOUT_BLOCK_ROWS)
            wout_ref[dst, :] = blk.astype(_BF16)

        stream(win_hbm, rows, D_MODEL // rows, IN_WIDTH, lambda i: i * rows, consume_in)
        stream(wout_hbm, WOUT_BLOCK_ROWS, D_MODEL // WOUT_BLOCK_ROWS, D_MODEL, wout_src_row, consume_out)
        stream(wg_hbm, rows, D_MODEL // rows, D_FF, lambda i: i * rows, consume_gated(wg_ref))
        stream(wu_hbm, rows, D_MODEL // rows, D_FF, lambda i: i * rows, consume_gated(wu_ref))
        stream(wd_hbm, rows, D_FF // rows, D_MODEL, lambda i: i * rows, consume_down)
        wpool_ref[...] = wpool_f32_ref[...].astype(_BF16)

    @pl.when(step == 0)
    def _first_step():
        prepare_weights()
        mixcat[...] = jnp.zeros(mixcat.shape, _BF16)
        stage_a_store(*stage_a_compute(xprev_ref), None)
        bkt = bkt_ref[...]
        key_is_prev = lax.broadcasted_iota(jnp.int32, bkt.shape, 0) < BLOCK
        for h in range(N_KV_HEADS):
            for g in range(GQA_GROUP):
                head = h * GQA_GROUP + g

                def body(bb, tab):
                    return jnp.where(bkt == bb, relb_ref[bb, head], tab)

                tab = lax.fori_loop(0, N_BUCKETS, body, jnp.zeros(bkt.shape, _F32))
                tab = jnp.where(bkt < 0, NEG_INF, tab)
                rows = slice(h * 2 * BLOCK, (h + 1) * 2 * BLOCK)
                cols = slice(g * BLOCK, (g + 1) * BLOCK)
                bias_t[0, rows, cols] = tab
                bias_t[1, rows, cols] = jnp.where(key_is_prev, NEG_INF, tab)

    def ffn_gate_up(h2, inv_rms2, c):
        cols = slice(c * FFN_CHUNK, (c + 1) * FFN_CHUNK)
        gate = _dot(h2, wg_ref[:, cols]) * inv_rms2
        up = _dot(h2, wu_ref[:, cols]) * inv_rms2
        return (gate * (1.0 / (1.0 + jnp.exp(-gate))) * up).astype(_BF16)

    def ffn_down(c, act, rows=slice(None)):
        return _dot(act[rows], wd_ref[c * FFN_CHUNK:(c + 1) * FFN_CHUNK, :])

    pos = seq_tile * TILE + lax.broadcasted_iota(jnp.int32, (TILE, 1), 0)

    def pool_group(g):
        w = POOL_WINDOWS[g]
        cols = slice(g * POOL_GROUP_DIM, (g + 1) * POOL_GROUP_DIM)
        ext = u_buf[:, cols]
        acc = ext
        lag = 1
        while lag < w:
            acc = acc + jnp.concatenate([acc[:lag], acc[:-lag]], axis=0)
            lag *= 2
        count = jnp.minimum(pos + 1, w).astype(_F32)
        pooled = acc[POOL_HALO:] / count - ext[POOL_HALO:]
        mixed = _dot(pooled.astype(_BF16), wpool_ref[g]) * pscale_ref[:, cols]
        mixcat[:, cols] = mixed.astype(_BF16)

    col = lax.broadcasted_iota(jnp.int32, (1, GQA_GROUP * BLOCK), 1)
    sink_rows = []
    for h in range(N_KV_HEADS):
        row = jnp.full((1, GQA_GROUP * BLOCK), sinks_ref[h * GQA_GROUP], _F32)
        for g in range(1, GQA_GROUP):
            row = jnp.where(col >= g * BLOCK, sinks_ref[h * GQA_GROUP + g], row)
        sink_rows.append(row)
    first_variant = jnp.where(seq_tile == 0, 1, 0)

    def logits(j):
        band = slice(j * BLOCK, (j + 2) * BLOCK)
        kb = jnp.concatenate([kcat[0, band, :], kcat[1, band, :]], axis=0)
        s = lax.dot_general(kb, q_stack[j], (((1,), (1,)), ((), ())), preferred_element_type=_F32)
        return s + (bias_t[first_variant] if j == 0 else bias_t[0])

    def attend(j, s):
        band = slice(j * BLOCK, (j + 2) * BLOCK)
        outs = []
        for h in range(N_KV_HEADS):
            sh = s[h * 2 * BLOCK:(h + 1) * 2 * BLOCK, :]
            m = jnp.maximum(jnp.max(sh, axis=0, keepdims=True), sink_rows[h])
            p = jnp.exp(sh - m)
            denom = jnp.sum(p, axis=0, keepdims=True) + jnp.exp(sink_rows[h] - m)
            out_h = _dot(vt[h * HEAD_DIM:(h + 1) * HEAD_DIM, band], p.astype(_BF16))
            outs.append(out_h * (1.0 / denom))
        o_t = jnp.concatenate(outs, axis=0)
        blks = []
        for g in range(GQA_GROUP):
            blks.append(o_t[:, g * BLOCK:(g + 1) * BLOCK].T.astype(_BF16))
        return jnp.concatenate(blks, axis=1)

    half = TILE // 2
    x1_halves = []
    for r in range(2):
        rows = slice(r * half, (r + 1) * half)
        mix = _dot(mixcat[rows, :], wout_ref[...])
        x1_halves.append(xprev_ref[rows, :] + _rmsnorm(mix, gpost_ref[...]))
    s_blocks = [logits(j) for j in range(N_BLOCKS)]
    x1 = jnp.concatenate(x1_halves, axis=0)
    inv_rms2 = lax.rsqrt(jnp.mean(x1 * x1, axis=-1, keepdims=True) + EPS)
    h2 = x1.astype(_BF16)
    acts = {0: ffn_gate_up(h2, inv_rms2, 0)}
    acc = None
    next_proj = None
    for c in range(N_CHUNKS - 1):
        acts[c + 1] = ffn_gate_up(h2, inv_rms2, c + 1)
        part = ffn_down(c, acts.pop(c))
        acc = part if acc is None else acc + part
        if c < N_BLOCKS:
            mixcat[c * BLOCK:(c + 1) * BLOCK, POOL_WIDTH:] = attend(c, s_blocks[c])
        elif c < N_BLOCKS + N_POOL_GROUPS:
            pool_group(c - N_BLOCKS)
        elif c == STAGE_A_AFTER_CHUNK:
            next_proj = stage_a_compute(xnext_ref)
    act = acts.pop(N_CHUNKS - 1)
    for r in range(N_BLOCKS):
        rows = slice(r * BLOCK, (r + 1) * BLOCK)
        f = acc[rows] + ffn_down(N_CHUNKS - 1, act, rows)
        o_ref[rows, :] = x1[rows] + _rmsnorm(f, gpost2_ref[...])
    if next_proj is None:
        next_proj = stage_a_compute(xnext_ref)
    stage_a_store(*next_proj, next_starts_seq)


def _resident(shape):
    return pl.BlockSpec(shape, lambda *_: (0,) * len(shape), pipeline_mode=pl.Buffered(1))


def _layer(x, g_pre_mix, w_in, w_pool, pool_scale, rel_bias, sinks, w_out, g_post_mix, g_pre_ffn, w_gate, w_up, w_down,
           g_post_ffn):
    batch, seq, _ = x.shape
    tokens = batch * seq
    n_tiles = tokens // TILE
    tiles_per_seq = seq // TILE
    bucket_t = jnp.asarray(_bucket_table_t())
    x2 = x.reshape(tokens, D_MODEL)
    row = lambda a: a.reshape(1, -1)
    column = lambda a: a.reshape(-1, 1)

    smem = pl.BlockSpec(memory_space=pltpu.SMEM)
    hbm = pl.BlockSpec(memory_space=pl.ANY)
    out = pl.pallas_call(
        functools.partial(_layer_kernel, tiles_per_seq, n_tiles),
        out_shape=jax.ShapeDtypeStruct(x2.shape, x2.dtype),
        grid=(n_tiles + 1,),
        in_specs=[
            pl.BlockSpec((TILE, D_MODEL), lambda s: (jnp.minimum(s + 1, n_tiles - 1), 0)),
            pl.BlockSpec((TILE, D_MODEL), lambda s: (jnp.maximum(s - 1, 0), 0)),
            hbm,
            _resident((N_POOL_GROUPS, POOL_GROUP_DIM, POOL_GROUP_DIM)),
            _resident((1, POOL_WIDTH)),
            smem,
            smem,
            _resident((2 * BLOCK, BLOCK)),
            hbm,
            _resident((1, D_MODEL)),
            hbm,
            hbm,
            hbm,
            _resident((1, D_MODEL)),
            _resident((D_MODEL, 1)),
            _resident((D_MODEL, 1)),
        ],
        out_specs=pl.BlockSpec((TILE, D_MODEL), lambda s: (jnp.maximum(s - 1, 0), 0)),
        scratch_shapes=[
            pltpu.VMEM((POOL_HALO + TILE, POOL_WIDTH), _F32),
            pltpu.VMEM((N_BLOCKS, GQA_GROUP * BLOCK, KV_WIDTH), _BF16),
            pltpu.VMEM((N_KV_HEADS, BLOCK + TILE, KV_WIDTH), _BF16),
            pltpu.VMEM((KV_WIDTH, BLOCK + TILE), _BF16),
            pltpu.VMEM((TILE, D_MODEL), _BF16),
            pltpu.VMEM((2, N_KV_HEADS * 2 * BLOCK, GQA_GROUP * BLOCK), _F32),
            pltpu.VMEM((D_MODEL, IN_WIDTH), _BF16),
            pltpu.VMEM((N_POOL_GROUPS, POOL_GROUP_DIM, POOL_GROUP_DIM), _BF16),
            pltpu.VMEM((D_MODEL, D_MODEL), _BF16),
            pltpu.VMEM((D_MODEL, D_FF), _BF16),
            pltpu.VMEM((D_MODEL, D_FF), _BF16),
            pltpu.VMEM((D_FF, D_MODEL), _BF16),
            pltpu.VMEM((2, WEIGHT_STAGE_ROWS, D_FF), _F32),
            pltpu.SemaphoreType.DMA((2,)),
        ],
        compiler_params=pltpu.CompilerParams(
            dimension_semantics=("arbitrary",),
            vmem_limit_bytes=VMEM_LIMIT_BYTES,
        ),
        name="hybrid_layer",
    )(x2, x2, w_in, w_pool, row(pool_scale), rel_bias, sinks, bucket_t, w_out,
      row(g_post_mix), w_gate, w_up, w_down, row(g_post_ffn), column(g_pre_mix), column(g_pre_ffn))
    return out.reshape(x.shape)


def kernel(x, g_pre_mix, w_in, w_pool, pool_scale, rel_bias, sinks, w_out, g_post_mix, g_pre_ffn, w_gate, w_up, w_down, g_post_ffn):
    depth = g_pre_mix.shape[0]
    for l in range(depth):
        x = _layer(x, g_pre_mix[l], w_in[l], w_pool[l], pool_scale[l], rel_bias, sinks[l], w_out[l], g_post_mix[l],
                   g_pre_ffn[l], w_gate[l], w_up[l], w_down[l], g_post_ffn[l])
    return x
```

```python
import functools

import numpy as np
import jax
import jax.numpy as jnp
from jax import lax
from jax.experimental import pallas as pl
from jax.experimental.pallas import tpu as pltpu

D_MODEL = 1024
POOL_WIDTH = 512
POOL_WINDOWS = (2, 4, 8, 16)
N_POOL_GROUPS = len(POOL_WINDOWS)
POOL_GROUP_DIM = POOL_WIDTH // N_POOL_GROUPS
ATTN_WIDTH = 512
HEAD_DIM = 64
N_Q_HEADS = 8
N_KV_HEADS = 2
GQA_GROUP = N_Q_HEADS // N_KV_HEADS
WINDOW = 128
BLOCK = 128
N_BUCKETS = 32
MAX_EXACT = N_BUCKETS // 2
MAX_DISTANCE = 128
KV_WIDTH = N_KV_HEADS * HEAD_DIM
IN_WIDTH = POOL_WIDTH + ATTN_WIDTH + 2 * KV_WIDTH
D_FF = 2816
EPS = 1e-6
NEG_INF = -1e30

POOL_HALO = 16
TILE = 512
N_BLOCKS = TILE // BLOCK
FFN_CHUNK = 256
N_CHUNKS = D_FF // FFN_CHUNK
STAGE_A_AFTER_CHUNK = 8
WEIGHT_STAGE_ROWS = 128
WEIGHT_STAGE_SLOTS = 4
WOUT_BLOCK_ROWS = HEAD_DIM
VMEM_LIMIT_BYTES = 60 * 1024 * 1024

_F32 = jnp.float32
_BF16 = jnp.bfloat16


def _dot(a, b):
    return jnp.dot(a, b, preferred_element_type=_F32)


def _rmsnorm(xf, g):
    ms = jnp.mean(xf * xf, axis=-1, keepdims=True)
    return xf * lax.rsqrt(ms + EPS) * g


def _bucket_table_t():
    qi = np.arange(BLOCK)[None, :]
    kj = np.arange(2 * BLOCK)[:, None]
    dist = qi + BLOCK - kj
    n = np.maximum(dist, 0)
    nf = np.maximum(n, 1).astype(np.float32)
    large = MAX_EXACT + (
        np.log(nf / np.float32(MAX_EXACT)) / np.float32(np.log(MAX_DISTANCE / MAX_EXACT)) * np.float32(N_BUCKETS - MAX_EXACT)
    ).astype(np.int32)
    large = np.minimum(large, N_BUCKETS - 1)
    bucket = np.where(n < MAX_EXACT, n, large)
    in_window = (dist >= 0) & (dist < WINDOW)
    return np.where(in_window, bucket, -1).astype(np.int32)


def _layer_kernel(tiles_per_seq, n_tiles,
                  xnext_ref, xprev_ref, win_hbm, wpool_f32_ref, pscale_ref, relb_ref, sinks_ref, bkt_ref, wout_hbm,
                  gpost_ref, wg_hbm, wu_hbm, wd_hbm, gpost2_ref, gin_col_ref, gffn_col_ref,
                  o_ref, u_buf, q_stack, kcat, vt, mixcat, bias_t,
                  win_ref, wpool_ref, wout_ref, wg_ref, wu_ref, wd_ref, stage, stage_sem):
    step = pl.program_id(0)
    seq_tile = lax.rem(jnp.minimum(step, n_tiles - 1), tiles_per_seq)
    next_starts_seq = lax.rem(jnp.minimum(step + 1, n_tiles - 1), tiles_per_seq) == 0

    def stage_a_compute(x_ref):
        x = x_ref[...]
        inv_rms = lax.rsqrt(jnp.mean(x * x, axis=-1, keepdims=True) + EPS)
        xb = x.astype(_BF16)
        u = _dot(xb, win_ref[:, 0:POOL_WIDTH]) * inv_rms
        q = (_dot(xb, win_ref[:, POOL_WIDTH:POOL_WIDTH + ATTN_WIDTH]) * inv_rms).astype(_BF16)
        kv = _dot(xb, win_ref[:, POOL_WIDTH + ATTN_WIDTH:IN_WIDTH]) * inv_rms
        return u, q, kv

    def stage_a_store(u, q, kv, starts_seq):
        def history(tail):
            if starts_seq is None:
                return jnp.zeros(tail.shape, tail.dtype)
            return jnp.where(starts_seq, jnp.zeros(tail.shape, tail.dtype), tail[...])

        u_buf[0:POOL_HALO, :] = history(u_buf.at[TILE:TILE + POOL_HALO, :])
        u_buf[POOL_HALO:POOL_HALO + TILE, :] = u
        for j in range(N_BLOCKS):
            for g in range(GQA_GROUP):
                q_stack[j, g * BLOCK:(g + 1) * BLOCK, :] = q[j * BLOCK:(j + 1) * BLOCK, g * BLOCK:(g + 1) * BLOCK]
        k = kv[:, 0:KV_WIDTH]
        v = kv[:, KV_WIDTH:2 * KV_WIDTH]
        lane = lax.broadcasted_iota(jnp.int32, k.shape, 1)
        kcat[:, 0:BLOCK, :] = history(kcat.at[:, TILE:TILE + BLOCK, :])
        kcat[0, BLOCK:BLOCK + TILE, :] = jnp.where(lane < HEAD_DIM, k, 0.0).astype(_BF16)
        kcat[1, BLOCK:BLOCK + TILE, :] = jnp.where(lane >= HEAD_DIM, k, 0.0).astype(_BF16)
        vt[:, 0:BLOCK] = history(vt.at[:, TILE:TILE + BLOCK])
        vt[:, BLOCK:BLOCK + TILE] = v.T.astype(_BF16)

    def prepare_weights():
        def stream(src, rows_per_chunk, n_chunks, width, src_row, consume, placements):
            per_iter = len(placements)
            n_iter = n_chunks // per_iter

            def copies(i, slot):
                return [pltpu.make_async_copy(src.at[pl.ds(src_row(i * per_iter + k), rows_per_chunk), :],
                                              stage.at[slot, r0:r0 + rows_per_chunk, l0:l0 + width],
                                              stage_sem.at[slot])
                        for k, (r0, l0) in enumerate(placements)]

            def start(i, slot):
                for c in copies(i, slot):
                    c.start()

            for i in range(min(WEIGHT_STAGE_SLOTS - 1, n_iter)):
                start(i, i)

            def body(i, carry):
                slot = lax.rem(i, WEIGHT_STAGE_SLOTS)
                ahead = i + WEIGHT_STAGE_SLOTS - 1

                @pl.when(ahead < n_iter)
                def _prefetch():
                    start(ahead, lax.rem(ahead, WEIGHT_STAGE_SLOTS))

                for c in copies(i, slot):
                    c.wait()
                for k, (r0, l0) in enumerate(placements):
                    consume(i * per_iter + k, stage[slot, r0:r0 + rows_per_chunk, l0:l0 + width])
                return carry

            lax.fori_loop(0, n_iter, body, 0)

        rows = WEIGHT_STAGE_ROWS

        def chunk_rows(i):
            return pl.ds(pl.multiple_of(i * rows, rows), rows)

        def consume_in(i, blk):
            qb = blk[:, POOL_WIDTH:POOL_WIDTH + ATTN_WIDTH]
            heads = [qb[:, (h * GQA_GROUP + g) * HEAD_DIM:(h * GQA_GROUP + g + 1) * HEAD_DIM]
                     for g in range(GQA_GROUP) for h in range(N_KV_HEADS)]
            q_regrouped = jnp.concatenate(heads, axis=1) * (HEAD_DIM ** -0.5)
            full = jnp.concatenate([blk[:, :POOL_WIDTH], q_regrouped, blk[:, POOL_WIDTH + ATTN_WIDTH:]], axis=1)
            win_ref[chunk_rows(i), :] = (gin_col_ref[chunk_rows(i), :] * full).astype(_BF16)

        def consume_gated(dst):
            def consume(i, blk):
                dst[chunk_rows(i), :] = (gffn_col_ref[chunk_rows(i), :] * blk).astype(_BF16)
            return consume

        def consume_down(i, blk):
            wd_ref[chunk_rows(i), :] = blk.astype(_BF16)

        def wout_src_row(i):
            j = i - POOL_WIDTH // WOUT_BLOCK_ROWS
            head = (j % N_KV_HEADS) * GQA_GROUP + j // N_KV_HEADS
            return jnp.where(j < 0, i, POOL_WIDTH // WOUT_BLOCK_ROWS + head) * WOUT_BLOCK_ROWS

        def consume_out(i, blk):
            dst = pl.ds(pl.multiple_of(i * WOUT_BLOCK_ROWS, WOUT_BLOCK_ROWS), WOUT_BLOCK_ROWS)
            wout_ref[dst, :] = blk.astype(_BF16)

        stream(win_hbm, rows, D_MODEL // rows, IN_WIDTH, lambda c: c * rows, consume_in, [(0, 0), (0, IN_WIDTH)])
        stream(wout_hbm, WOUT_BLOCK_ROWS, D_MODEL // WOUT_BLOCK_ROWS, D_MODEL, wout_src_row, consume_out,
               [(0, 0), (WOUT_BLOCK_ROWS, 0), (0, D_MODEL), (WOUT_BLOCK_ROWS, D_MODEL)])
        stream(wg_hbm, rows, D_MODEL // rows, D_FF, lambda c: c * rows, consume_gated(wg_ref), [(0, 0)])
        stream(wu_hbm, rows, D_MODEL // rows, D_FF, lambda c: c * rows, consume_gated(wu_ref), [(0, 0)])
        stream(wd_hbm, rows, D_FF // rows, D_MODEL, lambda c: c * rows, consume_down, [(0, 0), (0, D_MODEL)])
        wpool_ref[...] = wpool_f32_ref[...].astype(_BF16)

    @pl.when(step == 0)
    def _first_step():
        prepare_weights()
        mixcat[...] = jnp.zeros(mixcat.shape, _BF16)
        stage_a_store(*stage_a_compute(xprev_ref), None)
        bkt = bkt_ref[...]
        key_is_prev = lax.broadcasted_iota(jnp.int32, bkt.shape, 0) < BLOCK
        for h in range(N_KV_HEADS):
            for g in range(GQA_GROUP):
                head = h * GQA_GROUP + g

                def body(bb, tab):
                    return jnp.where(bkt == bb, relb_ref[bb, head], tab)

                tab = lax.fori_loop(0, N_BUCKETS, body, jnp.zeros(bkt.shape, _F32))
                tab = jnp.where(bkt < 0, NEG_INF, tab)
                rows = slice(h * 2 * BLOCK, (h + 1) * 2 * BLOCK)
                cols = slice(g * BLOCK, (g + 1) * BLOCK)
                bias_t[0, rows, cols] = tab
                bias_t[1, rows, cols] = jnp.where(key_is_prev, NEG_INF, tab)

    def ffn_gate_up(h2, inv_rms2, c):
        cols = slice(c * FFN_CHUNK, (c + 1) * FFN_CHUNK)
        gate = _dot(h2, wg_ref[:, cols]) * inv_rms2
        up = _dot(h2, wu_ref[:, cols]) * inv_rms2
        return (gate * (1.0 / (1.0 + jnp.exp(-gate))) * up).astype(_BF16)

    def ffn_down(c, act, rows=slice(None)):
        return _dot(act[rows], wd_ref[c * FFN_CHUNK:(c + 1) * FFN_CHUNK, :])

    pos = seq_tile * TILE + lax.broadcasted_iota(jnp.int32, (TILE, 1), 0)

    def pool_group(g):
        w = POOL_WINDOWS[g]
        cols = slice(g * POOL_GROUP_DIM, (g + 1) * POOL_GROUP_DIM)
        ext = u_buf[:, cols]
        acc = ext
        lag = 1
        while lag < w:
            acc = acc + jnp.concatenate([acc[:lag], acc[:-lag]], axis=0)
            lag *= 2
        count = jnp.minimum(pos + 1, w).astype(_F32)
        pooled = acc[POOL_HALO:] / count - ext[POOL_HALO:]
        mixed = _dot(pooled.astype(_BF16), wpool_ref[g]) * pscale_ref[:, cols]
        mixcat[:, cols] = mixed.astype(_BF16)

    col = lax.broadcasted_iota(jnp.int32, (1, GQA_GROUP * BLOCK), 1)
    sink_rows = []
    for h in range(N_KV_HEADS):
        row = jnp.full((1, GQA_GROUP * BLOCK), sinks_ref[h * GQA_GROUP], _F32)
        for g in range(1, GQA_GROUP):
            row = jnp.where(col >= g * BLOCK, sinks_ref[h * GQA_GROUP + g], row)
        sink_rows.append(row)
    first_variant = jnp.where(seq_tile == 0, 1, 0)

    def logits(j):
        band = slice(j * BLOCK, (j + 2) * BLOCK)
        kb = jnp.concatenate([kcat[0, band, :], kcat[1, band, :]], axis=0)
        s = lax.dot_general(kb, q_stack[j], (((1,), (1,)), ((), ())), preferred_element_type=_F32)
        return s + (bias_t[first_variant] if j == 0 else bias_t[0])

    def attend(j, s):
        band = slice(j * BLOCK, (j + 2) * BLOCK)
        outs = []
        for h in range(N_KV_HEADS):
            sh = s[h * 2 * BLOCK:(h + 1) * 2 * BLOCK, :]
            m = jnp.maximum(jnp.max(sh, axis=0, keepdims=True), sink_rows[h])
            p = jnp.exp(sh - m)
            denom = jnp.sum(p, axis=0, keepdims=True) + jnp.exp(sink_rows[h] - m)
            out_h = _dot(vt[h * HEAD_DIM:(h + 1) * HEAD_DIM, band], p.astype(_BF16))
            outs.append(out_h * (1.0 / denom))
        o_t = jnp.concatenate(outs, axis=0)
        blks = []
        for g in range(GQA_GROUP):
            blks.append(o_t[:, g * BLOCK:(g + 1) * BLOCK].T.astype(_BF16))
        return jnp.concatenate(blks, axis=1)

    half = TILE // 2
    x1_halves = []
    for r in range(2):
        rows = slice(r * half, (r + 1) * half)
        mix = _dot(mixcat[rows, :], wout_ref[...])
        x1_halves.append(xprev_ref[rows, :] + _rmsnorm(mix, gpost_ref[...]))
    s_blocks = [logits(j) for j in range(N_BLOCKS)]
    x1 = jnp.concatenate(x1_halves, axis=0)
    inv_rms2 = lax.rsqrt(jnp.mean(x1 * x1, axis=-1, keepdims=True) + EPS)
    h2 = x1.astype(_BF16)
    acts = {0: ffn_gate_up(h2, inv_rms2, 0)}
    acc = None
    next_proj = None
    for c in range(N_CHUNKS - 1):
        acts[c + 1] = ffn_gate_up(h2, inv_rms2, c + 1)
        part = ffn_down(c, acts.pop(c))
        acc = part if acc is None else acc + part
        if c < N_BLOCKS:
            mixcat[c * BLOCK:(c + 1) * BLOCK, POOL_WIDTH:] = attend(c, s_blocks[c])
        elif c < N_BLOCKS + N_POOL_GROUPS:
            pool_group(c - N_BLOCKS)
        elif c == STAGE_A_AFTER_CHUNK:
            next_proj = stage_a_compute(xnext_ref)
    act = acts.pop(N_CHUNKS - 1)
    for r in range(N_BLOCKS):
        rows = slice(r * BLOCK, (r + 1) * BLOCK)
        f = acc[rows] + ffn_down(N_CHUNKS - 1, act, rows)
        o_ref[rows, :] = x1[rows] + _rmsnorm(f, gpost2_ref[...])
    if next_proj is None:
        next_proj = stage_a_compute(xnext_ref)
    stage_a_store(*next_proj, next_starts_seq)


def _resident(shape):
    return pl.BlockSpec(shape, lambda *_: (0,) * len(shape), pipeline_mode=pl.Buffered(1))


def _layer(x, g_pre_mix, w_in, w_pool, pool_scale, rel_bias, sinks, w_out, g_post_mix, g_pre_ffn, w_gate, w_up, w_down,
           g_post_ffn):
    batch, seq, _ = x.shape
    tokens = batch * seq
    n_tiles = tokens // TILE
    tiles_per_seq = seq // TILE
    bucket_t = jnp.asarray(_bucket_table_t())
    x2 = x.reshape(tokens, D_MODEL)
    row = lambda a: a.reshape(1, -1)
    column = lambda a: a.reshape(-1, 1)

    smem = pl.BlockSpec(memory_space=pltpu.SMEM)
    hbm = pl.BlockSpec(memory_space=pl.ANY)
    out = pl.pallas_call(
        functools.partial(_layer_kernel, tiles_per_seq, n_tiles),
        out_shape=jax.ShapeDtypeStruct(x2.shape, x2.dtype),
        grid=(n_tiles + 1,),
        in_specs=[
            pl.BlockSpec((TILE, D_MODEL), lambda s: (jnp.minimum(s + 1, n_tiles - 1), 0)),
            pl.BlockSpec((TILE, D_MODEL), lambda s: (jnp.maximum(s - 1, 0), 0)),
            hbm,
            _resident((N_POOL_GROUPS, POOL_GROUP_DIM, POOL_GROUP_DIM)),
            _resident((1, POOL_WIDTH)),
            smem,
            smem,
            _resident((2 * BLOCK, BLOCK)),
            hbm,
            _resident((1, D_MODEL)),
            hbm,
            hbm,
            hbm,
            _resident((1, D_MODEL)),
            _resident((D_MODEL, 1)),
            _resident((D_MODEL, 1)),
        ],
        out_specs=pl.BlockSpec((TILE, D_MODEL), lambda s: (jnp.maximum(s - 1, 0), 0)),
        scratch_shapes=[
            pltpu.VMEM((POOL_HALO + TILE, POOL_WIDTH), _F32),
            pltpu.VMEM((N_BLOCKS, GQA_GROUP * BLOCK, KV_WIDTH), _BF16),
            pltpu.VMEM((N_KV_HEADS, BLOCK + TILE, KV_WIDTH), _BF16),
            pltpu.VMEM((KV_WIDTH, BLOCK + TILE), _BF16),
            pltpu.VMEM((TILE, D_MODEL), _BF16),
            pltpu.VMEM((2, N_KV_HEADS * 2 * BLOCK, GQA_GROUP * BLOCK), _F32),
            pltpu.VMEM((D_MODEL, IN_WIDTH), _BF16),
            pltpu.VMEM((N_POOL_GROUPS, POOL_GROUP_DIM, POOL_GROUP_DIM), _BF16),
            pltpu.VMEM((D_MODEL, D_MODEL), _BF16),
            pltpu.VMEM((D_MODEL, D_FF), _BF16),
            pltpu.VMEM((D_MODEL, D_FF), _BF16),
            pltpu.VMEM((D_FF, D_MODEL), _BF16),
            pltpu.VMEM((WEIGHT_STAGE_SLOTS, WEIGHT_STAGE_ROWS, D_FF), _F32),
            pltpu.SemaphoreType.DMA((WEIGHT_STAGE_SLOTS,)),
        ],
        compiler_params=pltpu.CompilerParams(
            dimension_semantics=("arbitrary",),
            vmem_limit_bytes=VMEM_LIMIT_BYTES,
        ),
        name="hybrid_layer",
    )(x2, x2, w_in, w_pool, row(pool_scale), rel_bias, sinks, bucket_t, w_out,
      row(g_post_mix), w_gate, w_up, w_down, row(g_post_ffn), column(g_pre_mix), column(g_pre_ffn))
    return out.reshape(x.shape)


def kernel(x, g_pre_mix, w_in, w_pool, pool_scale, rel_bias, sinks, w_out, g_post_mix, g_pre_ffn, w_gate, w_up, w_down, g_post_ffn):
    depth = g_pre_mix.shape[0]
    for l in range(depth):
        x = _layer(x, g_pre_mix[l], w_in[l], w_pool[l], pool_scale[l], rel_bias, sinks[l], w_out[l], g_post_mix[l],
                   g_pre_ffn[l], w_gate[l], w_up[l], w_down[l], g_post_ffn[l])
    return x
```

```python
import functools

import numpy as np
import jax
import jax.numpy as jnp
from jax import lax
from jax.experimental import pallas as pl
from jax.experimental.pallas import tpu as pltpu

D_MODEL = 1024
POOL_WIDTH = 512
POOL_WINDOWS = (2, 4, 8, 16)
N_POOL_GROUPS = len(POOL_WINDOWS)
POOL_GROUP_DIM = POOL_WIDTH // N_POOL_GROUPS
ATTN_WIDTH = 512
HEAD_DIM = 64
N_Q_HEADS = 8
N_KV_HEADS = 2
GQA_GROUP = N_Q_HEADS // N_KV_HEADS
WINDOW = 128
BLOCK = 128
N_BUCKETS = 32
MAX_EXACT = N_BUCKETS // 2
MAX_DISTANCE = 128
KV_WIDTH = N_KV_HEADS * HEAD_DIM
IN_WIDTH = POOL_WIDTH + ATTN_WIDTH + 2 * KV_WIDTH
D_FF = 2816
EPS = 1e-6
NEG_INF = -1e30

POOL_HALO = 16
TILE = 512
N_BLOCKS = TILE // BLOCK
FFN_CHUNK = 256
N_CHUNKS = D_FF // FFN_CHUNK
N_HALVES = 2
ATTEND_SLOTS = (0, 2, 4, 6)
POOL_SLOTS = (8, 10, 12, 14)
STAGE_A_SLOT = 16
WEIGHT_STAGE_ROWS = 128
WEIGHT_STAGE_SLOTS = 4
WOUT_BLOCK_ROWS = HEAD_DIM
VMEM_LIMIT_BYTES = 60 * 1024 * 1024

_F32 = jnp.float32
_BF16 = jnp.bfloat16


def _dot(a, b):
    return jnp.dot(a, b, preferred_element_type=_F32)


def _rmsnorm(xf, g):
    ms = jnp.mean(xf * xf, axis=-1, keepdims=True)
    return xf * lax.rsqrt(ms + EPS) * g


def _bucket_table_t():
    qi = np.arange(BLOCK)[None, :]
    kj = np.arange(2 * BLOCK)[:, None]
    dist = qi + BLOCK - kj
    n = np.maximum(dist, 0)
    nf = np.maximum(n, 1).astype(np.float32)
    large = MAX_EXACT + (
        np.log(nf / np.float32(MAX_EXACT)) / np.float32(np.log(MAX_DISTANCE / MAX_EXACT)) * np.float32(N_BUCKETS - MAX_EXACT)
    ).astype(np.int32)
    large = np.minimum(large, N_BUCKETS - 1)
    bucket = np.where(n < MAX_EXACT, n, large)
    in_window = (dist >= 0) & (dist < WINDOW)
    return np.where(in_window, bucket, -1).astype(np.int32)


def _layer_kernel(tiles_per_seq, n_tiles,
                  xnext_ref, xprev_ref, win_hbm, wpool_f32_ref, pscale_ref, relb_ref, sinks_ref, bkt_ref, wout_hbm,
                  gpost_ref, wg_hbm, wu_hbm, wd_hbm, gpost2_ref, gin_col_ref, gffn_col_ref,
                  o_ref, u_buf, q_stack, kcat, vt, mixcat, bias_t,
                  win_ref, wpool_ref, wout_ref, wg_ref, wu_ref, wd_ref, stage, stage_sem):
    step = pl.program_id(0)
    seq_tile = lax.rem(jnp.minimum(step, n_tiles - 1), tiles_per_seq)
    next_starts_seq = lax.rem(jnp.minimum(step + 1, n_tiles - 1), tiles_per_seq) == 0

    def stage_a_compute(x_ref):
        x = x_ref[...]
        inv_rms = lax.rsqrt(jnp.mean(x * x, axis=-1, keepdims=True) + EPS)
        xb = x.astype(_BF16)
        u = _dot(xb, win_ref[:, 0:POOL_WIDTH]) * inv_rms
        q = (_dot(xb, win_ref[:, POOL_WIDTH:POOL_WIDTH + ATTN_WIDTH]) * inv_rms).astype(_BF16)
        kv = _dot(xb, win_ref[:, POOL_WIDTH + ATTN_WIDTH:IN_WIDTH]) * inv_rms
        return u, q, kv

    def stage_a_store(u, q, kv, starts_seq):
        def history(tail):
            if starts_seq is None:
                return jnp.zeros(tail.shape, tail.dtype)
            return jnp.where(starts_seq, jnp.zeros(tail.shape, tail.dtype), tail[...])

        u_buf[0:POOL_HALO, :] = history(u_buf.at[TILE:TILE + POOL_HALO, :])
        u_buf[POOL_HALO:POOL_HALO + TILE, :] = u
        for j in range(N_BLOCKS):
            for g in range(GQA_GROUP):
                q_stack[j, g * BLOCK:(g + 1) * BLOCK, :] = q[j * BLOCK:(j + 1) * BLOCK, g * BLOCK:(g + 1) * BLOCK]
        k = kv[:, 0:KV_WIDTH]
        v = kv[:, KV_WIDTH:2 * KV_WIDTH]
        lane = lax.broadcasted_iota(jnp.int32, k.shape, 1)
        kcat[:, 0:BLOCK, :] = history(kcat.at[:, TILE:TILE + BLOCK, :])
        kcat[0, BLOCK:BLOCK + TILE, :] = jnp.where(lane < HEAD_DIM, k, 0.0).astype(_BF16)
        kcat[1, BLOCK:BLOCK + TILE, :] = jnp.where(lane >= HEAD_DIM, k, 0.0).astype(_BF16)
        vt[:, 0:BLOCK] = history(vt.at[:, TILE:TILE + BLOCK])
        vt[:, BLOCK:BLOCK + TILE] = v.T.astype(_BF16)

    def prepare_weights():
        def stream(src, rows_per_chunk, n_chunks, width, src_row, consume, placements):
            per_iter = len(placements)
            n_iter = n_chunks // per_iter

            def copies(i, slot):
                return [pltpu.make_async_copy(src.at[pl.ds(src_row(i * per_iter + k), rows_per_chunk), :],
                                              stage.at[slot, r0:r0 + rows_per_chunk, l0:l0 + width],
                                              stage_sem.at[slot])
                        for k, (r0, l0) in enumerate(placements)]

            def start(i, slot):
                for c in copies(i, slot):
                    c.start()

            for i in range(min(WEIGHT_STAGE_SLOTS - 1, n_iter)):
                start(i, i)

            def body(i, carry):
                slot = lax.rem(i, WEIGHT_STAGE_SLOTS)
                ahead = i + WEIGHT_STAGE_SLOTS - 1

                @pl.when(ahead < n_iter)
                def _prefetch():
                    start(ahead, lax.rem(ahead, WEIGHT_STAGE_SLOTS))

                for c in copies(i, slot):
                    c.wait()
                for k, (r0, l0) in enumerate(placements):
                    consume(i * per_iter + k, stage[slot, r0:r0 + rows_per_chunk, l0:l0 + width])
                return carry

            lax.fori_loop(0, n_iter, body, 0)

        rows = WEIGHT_STAGE_ROWS

        def chunk_rows(i):
            return pl.ds(pl.multiple_of(i * rows, rows), rows)

        def consume_in(i, blk):
            qb = blk[:, POOL_WIDTH:POOL_WIDTH + ATTN_WIDTH]
            heads = [qb[:, (h * GQA_GROUP + g) * HEAD_DIM:(h * GQA_GROUP + g + 1) * HEAD_DIM]
                     for g in range(GQA_GROUP) for h in range(N_KV_HEADS)]
            q_regrouped = jnp.concatenate(heads, axis=1) * (HEAD_DIM ** -0.5)
            full = jnp.concatenate([blk[:, :POOL_WIDTH], q_regrouped, blk[:, POOL_WIDTH + ATTN_WIDTH:]], axis=1)
            win_ref[chunk_rows(i), :] = (gin_col_ref[chunk_rows(i), :] * full).astype(_BF16)

        def consume_gated(dst):
            def consume(i, blk):
                dst[chunk_rows(i), :] = (gffn_col_ref[chunk_rows(i), :] * blk).astype(_BF16)
            return consume

        def consume_down(i, blk):
            wd_ref[chunk_rows(i), :] = blk.astype(_BF16)

        def wout_src_row(i):
            j = i - POOL_WIDTH // WOUT_BLOCK_ROWS
            head = (j % N_KV_HEADS) * GQA_GROUP + j // N_KV_HEADS
            return jnp.where(j < 0, i, POOL_WIDTH // WOUT_BLOCK_ROWS + head) * WOUT_BLOCK_ROWS

        def consume_out(i, blk):
            dst = pl.ds(pl.multiple_of(i * WOUT_BLOCK_ROWS, WOUT_BLOCK_ROWS), WOUT_BLOCK_ROWS)
            wout_ref[dst, :] = blk.astype(_BF16)

        stream(win_hbm, rows, D_MODEL // rows, IN_WIDTH, lambda c: c * rows, consume_in, [(0, 0), (0, IN_WIDTH)])
        stream(wout_hbm, WOUT_BLOCK_ROWS, D_MODEL // WOUT_BLOCK_ROWS, D_MODEL, wout_src_row, consume_out,
               [(0, 0), (WOUT_BLOCK_ROWS, 0), (0, D_MODEL), (WOUT_BLOCK_ROWS, D_MODEL)])
        stream(wg_hbm, rows, D_MODEL // rows, D_FF, lambda c: c * rows, consume_gated(wg_ref), [(0, 0)])
        stream(wu_hbm, rows, D_MODEL // rows, D_FF, lambda c: c * rows, consume_gated(wu_ref), [(0, 0)])
        stream(wd_hbm, rows, D_FF // rows, D_MODEL, lambda c: c * rows, consume_down, [(0, 0), (0, D_MODEL)])
        wpool_ref[...] = wpool_f32_ref[...].astype(_BF16)

    @pl.when(step == 0)
    def _first_step():
        prepare_weights()
        mixcat[...] = jnp.zeros(mixcat.shape, _BF16)
        stage_a_store(*stage_a_compute(xprev_ref), None)
        bkt = bkt_ref[...]
        key_is_prev = lax.broadcasted_iota(jnp.int32, bkt.shape, 0) < BLOCK
        for h in range(N_KV_HEADS):
            for g in range(GQA_GROUP):
                head = h * GQA_GROUP + g

                def body(bb, tab):
                    return jnp.where(bkt == bb, relb_ref[bb, head], tab)

                tab = lax.fori_loop(0, N_BUCKETS, body, jnp.zeros(bkt.shape, _F32))
                tab = jnp.where(bkt < 0, NEG_INF, tab)
                rows = slice(h * 2 * BLOCK, (h + 1) * 2 * BLOCK)
                cols = slice(g * BLOCK, (g + 1) * BLOCK)
                bias_t[0, rows, cols] = tab
                bias_t[1, rows, cols] = jnp.where(key_is_prev, NEG_INF, tab)

    def ffn_gate_up(h2, inv_rms2, c):
        cols = slice(c * FFN_CHUNK, (c + 1) * FFN_CHUNK)
        gate = _dot(h2, wg_ref[:, cols]) * inv_rms2
        up = _dot(h2, wu_ref[:, cols]) * inv_rms2
        return (gate * (1.0 / (1.0 + jnp.exp(-gate))) * up).astype(_BF16)

    def ffn_down(c, act, rows=slice(None)):
        return _dot(act[rows], wd_ref[c * FFN_CHUNK:(c + 1) * FFN_CHUNK, :])

    pos = seq_tile * TILE + lax.broadcasted_iota(jnp.int32, (TILE, 1), 0)

    def pool_group(g):
        w = POOL_WINDOWS[g]
        cols = slice(g * POOL_GROUP_DIM, (g + 1) * POOL_GROUP_DIM)
        ext = u_buf[:, cols]
        acc = ext
        lag = 1
        while lag < w:
            acc = acc + jnp.concatenate([acc[:lag], acc[:-lag]], axis=0)
            lag *= 2
        count = jnp.minimum(pos + 1, w).astype(_F32)
        pooled = acc[POOL_HALO:] / count - ext[POOL_HALO:]
        mixed = _dot(pooled.astype(_BF16), wpool_ref[g]) * pscale_ref[:, cols]
        mixcat[:, cols] = mixed.astype(_BF16)

    col = lax.broadcasted_iota(jnp.int32, (1, GQA_GROUP * BLOCK), 1)
    sink_rows = []
    for h in range(N_KV_HEADS):
        row = jnp.full((1, GQA_GROUP * BLOCK), sinks_ref[h * GQA_GROUP], _F32)
        for g in range(1, GQA_GROUP):
            row = jnp.where(col >= g * BLOCK, sinks_ref[h * GQA_GROUP + g], row)
        sink_rows.append(row)
    first_variant = jnp.where(seq_tile == 0, 1, 0)

    def logits(j):
        band = slice(j * BLOCK, (j + 2) * BLOCK)
        kb = jnp.concatenate([kcat[0, band, :], kcat[1, band, :]], axis=0)
        s = lax.dot_general(kb, q_stack[j], (((1,), (1,)), ((), ())), preferred_element_type=_F32)
        return s + (bias_t[first_variant] if j == 0 else bias_t[0])

    def attend(j, s):
        band = slice(j * BLOCK, (j + 2) * BLOCK)
        outs = []
        for h in range(N_KV_HEADS):
            sh = s[h * 2 * BLOCK:(h + 1) * 2 * BLOCK, :]
            m = jnp.maximum(jnp.max(sh, axis=0, keepdims=True), sink_rows[h])
            p = jnp.exp(sh - m)
            denom = jnp.sum(p, axis=0, keepdims=True) + jnp.exp(sink_rows[h] - m)
            out_h = _dot(vt[h * HEAD_DIM:(h + 1) * HEAD_DIM, band], p.astype(_BF16))
            outs.append(out_h * (1.0 / denom))
        o_t = jnp.concatenate(outs, axis=0)
        blks = []
        for g in range(GQA_GROUP):
            blks.append(o_t[:, g * BLOCK:(g + 1) * BLOCK].T.astype(_BF16))
        return jnp.concatenate(blks, axis=1)

    half = TILE // N_HALVES
    x1_halves, h2_halves, inv_halves = [], [], []
    for r in range(N_HALVES):
        rows = slice(r * half, (r + 1) * half)
        mix = _dot(mixcat[rows, :], wout_ref[...])
        x1_r = xprev_ref[rows, :] + _rmsnorm(mix, gpost_ref[...])
        x1_halves.append(x1_r)
        inv_halves.append(lax.rsqrt(jnp.mean(x1_r * x1_r, axis=-1, keepdims=True) + EPS))
        h2_halves.append(x1_r.astype(_BF16))
    s_blocks = [logits(j) for j in range(N_BLOCKS)]
    work = [(r, c) for r in range(N_HALVES) for c in range(N_CHUNKS)]

    def attend_and_store(j):
        mixcat[j * BLOCK:(j + 1) * BLOCK, POOL_WIDTH:] = attend(j, s_blocks[j])

    next_proj = []

    def project_next_tile():
        next_proj.extend(stage_a_compute(xnext_ref))

    extras = {STAGE_A_SLOT: project_next_tile}
    for j in range(N_BLOCKS):
        extras[ATTEND_SLOTS[j]] = functools.partial(attend_and_store, j)
    for g in range(N_POOL_GROUPS):
        extras[POOL_SLOTS[g]] = functools.partial(pool_group, g)

    def gate_up(k):
        r, c = work[k]
        return ffn_gate_up(h2_halves[r], inv_halves[r], c)

    act_next = gate_up(0)
    acc = None
    for k, (r, c) in enumerate(work):
        act = act_next
        if k + 1 < len(work):
            act_next = gate_up(k + 1)
        part = ffn_down(c, act)
        acc = part if acc is None else acc + part
        if c == N_CHUNKS - 1:
            o_ref[r * half:(r + 1) * half, :] = x1_halves[r] + _rmsnorm(acc, gpost2_ref[...])
            acc = None
        if k in extras:
            extras[k]()
    stage_a_store(*next_proj, next_starts_seq)


def _resident(shape):
    return pl.BlockSpec(shape, lambda *_: (0,) * len(shape), pipeline_mode=pl.Buffered(1))


def _layer(x, g_pre_mix, w_in, w_pool, pool_scale, rel_bias, sinks, w_out, g_post_mix, g_pre_ffn, w_gate, w_up, w_down,
           g_post_ffn):
    batch, seq, _ = x.shape
    tokens = batch * seq
    n_tiles = tokens // TILE
    tiles_per_seq = seq // TILE
    bucket_t = jnp.asarray(_bucket_table_t())
    x2 = x.reshape(tokens, D_MODEL)
    row = lambda a: a.reshape(1, -1)
    column = lambda a: a.reshape(-1, 1)

    smem = pl.BlockSpec(memory_space=pltpu.SMEM)
    hbm = pl.BlockSpec(memory_space=pl.ANY)
    out = pl.pallas_call(
        functools.partial(_layer_kernel, tiles_per_seq, n_tiles),
        out_shape=jax.ShapeDtypeStruct(x2.shape, x2.dtype),
        grid=(n_tiles + 1,),
        in_specs=[
            pl.BlockSpec((TILE, D_MODEL), lambda s: (jnp.minimum(s + 1, n_tiles - 1), 0)),
            pl.BlockSpec((TILE, D_MODEL), lambda s: (jnp.maximum(s - 1, 0), 0)),
            hbm,
            _resident((N_POOL_GROUPS, POOL_GROUP_DIM, POOL_GROUP_DIM)),
            _resident((1, POOL_WIDTH)),
            smem,
            smem,
            _resident((2 * BLOCK, BLOCK)),
            hbm,
            _resident((1, D_MODEL)),
            hbm,
            hbm,
            hbm,
            _resident((1, D_MODEL)),
            _resident((D_MODEL, 1)),
            _resident((D_MODEL, 1)),
        ],
        out_specs=pl.BlockSpec((TILE, D_MODEL), lambda s: (jnp.maximum(s - 1, 0), 0)),
        scratch_shapes=[
            pltpu.VMEM((POOL_HALO + TILE, POOL_WIDTH), _F32),
            pltpu.VMEM((N_BLOCKS, GQA_GROUP * BLOCK, KV_WIDTH), _BF16),
            pltpu.VMEM((N_KV_HEADS, BLOCK + TILE, KV_WIDTH), _BF16),
            pltpu.VMEM((KV_WIDTH, BLOCK + TILE), _BF16),
            pltpu.VMEM((TILE, D_MODEL), _BF16),
            pltpu.VMEM((2, N_KV_HEADS * 2 * BLOCK, GQA_GROUP * BLOCK), _F32),
            pltpu.VMEM((D_MODEL, IN_WIDTH), _BF16),
            pltpu.VMEM((N_POOL_GROUPS, POOL_GROUP_DIM, POOL_GROUP_DIM), _BF16),
            pltpu.VMEM((D_MODEL, D_MODEL), _BF16),
            pltpu.VMEM((D_MODEL, D_FF), _BF16),
            pltpu.VMEM((D_MODEL, D_FF), _BF16),
            pltpu.VMEM((D_FF, D_MODEL), _BF16),
            pltpu.VMEM((WEIGHT_STAGE_SLOTS, WEIGHT_STAGE_ROWS, D_FF), _F32),
            pltpu.SemaphoreType.DMA((WEIGHT_STAGE_SLOTS,)),
        ],
        compiler_params=pltpu.CompilerParams(
            dimension_semantics=("arbitrary",),
            vmem_limit_bytes=VMEM_LIMIT_BYTES,
        ),
        name="hybrid_layer",
    )(x2, x2, w_in, w_pool, row(pool_scale), rel_bias, sinks, bucket_t, w_out,
      row(g_post_mix), w_gate, w_up, w_down, row(g_post_ffn), column(g_pre_mix), column(g_pre_ffn))
    return out.reshape(x.shape)


def kernel(x, g_pre_mix, w_in, w_pool, pool_scale, rel_bias, sinks, w_out, g_post_mix, g_pre_ffn, w_gate, w_up, w_down, g_post_ffn):
    depth = g_pre_mix.shape[0]
    for l in range(depth):
        x = _layer(x, g_pre_mix[l], w_in[l], w_pool[l], pool_scale[l], rel_bias, sinks[l], w_out[l], g_post_mix[l],
                   g_pre_ffn[l], w_gate[l], w_up[l], w_down[l], g_post_ffn[l])
    return x
```

```python
import functools

import numpy as np
import jax
import jax.numpy as jnp
from jax import lax
from jax.experimental import pallas as pl
from jax.experimental.pallas import tpu as pltpu

D_MODEL = 1024
POOL_WIDTH = 512
POOL_WINDOWS = (2, 4, 8, 16)
N_POOL_GROUPS = len(POOL_WINDOWS)
POOL_GROUP_DIM = POOL_WIDTH // N_POOL_GROUPS
ATTN_WIDTH = 512
HEAD_DIM = 64
N_Q_HEADS = 8
N_KV_HEADS = 2
GQA_GROUP = N_Q_HEADS // N_KV_HEADS
WINDOW = 128
BLOCK = 128
N_BUCKETS = 32
MAX_EXACT = N_BUCKETS // 2
MAX_DISTANCE = 128
KV_WIDTH = N_KV_HEADS * HEAD_DIM
IN_WIDTH = POOL_WIDTH + ATTN_WIDTH + 2 * KV_WIDTH
D_FF = 2816
EPS = 1e-6
NEG_INF = -1e30

POOL_HALO = 16
TILE = 512
N_BLOCKS = TILE // BLOCK
FFN_CHUNK = 256
N_CHUNKS = D_FF // FFN_CHUNK
N_HALVES = 2
ATTEND_SLOTS = (0, 2, 4, 6)
POOL_SLOTS = (8, 10, 12, 14)
STAGE_A_SLOT = 16
WEIGHT_STAGE_ROWS = 128
WEIGHT_STAGE_SLOTS = 4
WOUT_BLOCK_ROWS = HEAD_DIM
VMEM_LIMIT_BYTES = 60 * 1024 * 1024

_F32 = jnp.float32
_BF16 = jnp.bfloat16


def _dot(a, b):
    return jnp.dot(a, b, preferred_element_type=_F32)


def _rmsnorm(xf, g):
    ms = jnp.mean(xf * xf, axis=-1, keepdims=True)
    return xf * lax.rsqrt(ms + EPS) * g


def _bucket_table_t():
    qi = np.arange(BLOCK)[None, :]
    kj = np.arange(2 * BLOCK)[:, None]
    dist = qi + BLOCK - kj
    n = np.maximum(dist, 0)
    nf = np.maximum(n, 1).astype(np.float32)
    large = MAX_EXACT + (
        np.log(nf / np.float32(MAX_EXACT)) / np.float32(np.log(MAX_DISTANCE / MAX_EXACT)) * np.float32(N_BUCKETS - MAX_EXACT)
    ).astype(np.int32)
    large = np.minimum(large, N_BUCKETS - 1)
    bucket = np.where(n < MAX_EXACT, n, large)
    in_window = (dist >= 0) & (dist < WINDOW)
    return np.where(in_window, bucket, -1).astype(np.int32)


def _layer_kernel(tiles_per_seq, n_tiles,
                  xnext_ref, xprev_ref, win_hbm, wpool_f32_ref, pscale_ref, relb_ref, sinks_ref, bkt_ref, wout_hbm,
                  gpost_ref, wg_hbm, wu_hbm, wd_hbm, gpost2_ref, gin_col_ref, gffn_col_ref,
                  o_ref, u_buf, q_stack, kcat, vt, mixcat, mix_buf, bias_t,
                  win_ref, wpool_ref, wout_ref, wg_ref, wu_ref, wd_ref, stage, stage_sem):
    step = pl.program_id(0)
    seq_tile = lax.rem(jnp.minimum(step, n_tiles - 1), tiles_per_seq)
    next_starts_seq = lax.rem(jnp.minimum(step + 1, n_tiles - 1), tiles_per_seq) == 0

    def stage_a_compute(x_ref):
        x = x_ref[...]
        inv_rms = lax.rsqrt(jnp.mean(x * x, axis=-1, keepdims=True) + EPS)
        xb = x.astype(_BF16)
        u = _dot(xb, win_ref[:, 0:POOL_WIDTH]) * inv_rms
        q = (_dot(xb, win_ref[:, POOL_WIDTH:POOL_WIDTH + ATTN_WIDTH]) * inv_rms).astype(_BF16)
        kv = _dot(xb, win_ref[:, POOL_WIDTH + ATTN_WIDTH:IN_WIDTH]) * inv_rms
        return u, q, kv

    def stage_a_store(u, q, kv, starts_seq):
        def history(tail):
            if starts_seq is None:
                return jnp.zeros(tail.shape, tail.dtype)
            return jnp.where(starts_seq, jnp.zeros(tail.shape, tail.dtype), tail[...])

        u_buf[0:POOL_HALO, :] = history(u_buf.at[TILE:TILE + POOL_HALO, :])
        u_buf[POOL_HALO:POOL_HALO + TILE, :] = u
        for j in range(N_BLOCKS):
            for g in range(GQA_GROUP):
                q_stack[j, g * BLOCK:(g + 1) * BLOCK, :] = q[j * BLOCK:(j + 1) * BLOCK, g * BLOCK:(g + 1) * BLOCK]
        k = kv[:, 0:KV_WIDTH]
        v = kv[:, KV_WIDTH:2 * KV_WIDTH]
        lane = lax.broadcasted_iota(jnp.int32, k.shape, 1)
        kcat[:, 0:BLOCK, :] = history(kcat.at[:, TILE:TILE + BLOCK, :])
        kcat[0, BLOCK:BLOCK + TILE, :] = jnp.where(lane < HEAD_DIM, k, 0.0).astype(_BF16)
        kcat[1, BLOCK:BLOCK + TILE, :] = jnp.where(lane >= HEAD_DIM, k, 0.0).astype(_BF16)
        vt[:, 0:BLOCK] = history(vt.at[:, TILE:TILE + BLOCK])
        vt[:, BLOCK:BLOCK + TILE] = v.T.astype(_BF16)

    def prepare_weights():
        def stream(src, rows_per_chunk, n_chunks, width, src_row, consume, placements):
            per_iter = len(placements)
            n_iter = n_chunks // per_iter

            def copies(i, slot):
                return [pltpu.make_async_copy(src.at[pl.ds(src_row(i * per_iter + k), rows_per_chunk), :],
                                              stage.at[slot, r0:r0 + rows_per_chunk, l0:l0 + width],
                                              stage_sem.at[slot])
                        for k, (r0, l0) in enumerate(placements)]

            def start(i, slot):
                for c in copies(i, slot):
                    c.start()

            for i in range(min(WEIGHT_STAGE_SLOTS - 1, n_iter)):
                start(i, i)

            def body(i, carry):
                slot = lax.rem(i, WEIGHT_STAGE_SLOTS)
                ahead = i + WEIGHT_STAGE_SLOTS - 1

                @pl.when(ahead < n_iter)
                def _prefetch():
                    start(ahead, lax.rem(ahead, WEIGHT_STAGE_SLOTS))

                for c in copies(i, slot):
                    c.wait()
                for k, (r0, l0) in enumerate(placements):
                    consume(i * per_iter + k, stage[slot, r0:r0 + rows_per_chunk, l0:l0 + width])
                return carry

            lax.fori_loop(0, n_iter, body, 0)

        rows = WEIGHT_STAGE_ROWS

        def chunk_rows(i):
            return pl.ds(pl.multiple_of(i * rows, rows), rows)

        def consume_in(i, blk):
            qb = blk[:, POOL_WIDTH:POOL_WIDTH + ATTN_WIDTH]
            heads = [qb[:, (h * GQA_GROUP + g) * HEAD_DIM:(h * GQA_GROUP + g + 1) * HEAD_DIM]
                     for g in range(GQA_GROUP) for h in range(N_KV_HEADS)]
            q_regrouped = jnp.concatenate(heads, axis=1) * (HEAD_DIM ** -0.5)
            full = jnp.concatenate([blk[:, :POOL_WIDTH], q_regrouped, blk[:, POOL_WIDTH + ATTN_WIDTH:]], axis=1)
            win_ref[chunk_rows(i), :] = (gin_col_ref[chunk_rows(i), :] * full).astype(_BF16)

        def consume_gated(dst):
            def consume(i, blk):
                dst[chunk_rows(i), :] = (gffn_col_ref[chunk_rows(i), :] * blk).astype(_BF16)
            return consume

        def consume_down(i, blk):
            wd_ref[chunk_rows(i), :] = blk.astype(_BF16)

        def wout_src_row(i):
            j = i - POOL_WIDTH // WOUT_BLOCK_ROWS
            head = (j % N_KV_HEADS) * GQA_GROUP + j // N_KV_HEADS
            return jnp.where(j < 0, i, POOL_WIDTH // WOUT_BLOCK_ROWS + head) * WOUT_BLOCK_ROWS

        def consume_out(i, blk):
            dst = pl.ds(pl.multiple_of(i * WOUT_BLOCK_ROWS, WOUT_BLOCK_ROWS), WOUT_BLOCK_ROWS)
            wout_ref[dst, :] = blk.astype(_BF16)

        stream(win_hbm, rows, D_MODEL // rows, IN_WIDTH, lambda c: c * rows, consume_in, [(0, 0), (0, IN_WIDTH)])
        stream(wout_hbm, WOUT_BLOCK_ROWS, D_MODEL // WOUT_BLOCK_ROWS, D_MODEL, wout_src_row, consume_out,
               [(0, 0), (WOUT_BLOCK_ROWS, 0), (0, D_MODEL), (WOUT_BLOCK_ROWS, D_MODEL)])
        stream(wg_hbm, rows, D_MODEL // rows, D_FF, lambda c: c * rows, consume_gated(wg_ref), [(0, 0)])
        stream(wu_hbm, rows, D_MODEL // rows, D_FF, lambda c: c * rows, consume_gated(wu_ref), [(0, 0)])
        stream(wd_hbm, rows, D_FF // rows, D_MODEL, lambda c: c * rows, consume_down, [(0, 0), (0, D_MODEL)])
        wpool_ref[...] = wpool_f32_ref[...].astype(_BF16)

    @pl.when(step == 0)
    def _first_step():
        prepare_weights()
        mix_buf[...] = jnp.zeros(mix_buf.shape, _F32)
        stage_a_store(*stage_a_compute(xprev_ref), None)
        bkt = bkt_ref[...]
        key_is_prev = lax.broadcasted_iota(jnp.int32, bkt.shape, 0) < BLOCK
        for h in range(N_KV_HEADS):
            for g in range(GQA_GROUP):
                head = h * GQA_GROUP + g

                def body(bb, tab):
                    return jnp.where(bkt == bb, relb_ref[bb, head], tab)

                tab = lax.fori_loop(0, N_BUCKETS, body, jnp.zeros(bkt.shape, _F32))
                tab = jnp.where(bkt < 0, NEG_INF, tab)
                rows = slice(h * 2 * BLOCK, (h + 1) * 2 * BLOCK)
                cols = slice(g * BLOCK, (g + 1) * BLOCK)
                bias_t[0, rows, cols] = tab
                bias_t[1, rows, cols] = jnp.where(key_is_prev, NEG_INF, tab)

    def ffn_gate_up(h2, inv_rms2, c):
        cols = slice(c * FFN_CHUNK, (c + 1) * FFN_CHUNK)
        gate = _dot(h2, wg_ref[:, cols]) * inv_rms2
        up = _dot(h2, wu_ref[:, cols]) * inv_rms2
        return (gate * (1.0 / (1.0 + jnp.exp(-gate))) * up).astype(_BF16)

    def ffn_down(c, act, rows=slice(None)):
        return _dot(act[rows], wd_ref[c * FFN_CHUNK:(c + 1) * FFN_CHUNK, :])

    pos = seq_tile * TILE + lax.broadcasted_iota(jnp.int32, (TILE, 1), 0)

    def pool_group(g):
        w = POOL_WINDOWS[g]
        cols = slice(g * POOL_GROUP_DIM, (g + 1) * POOL_GROUP_DIM)
        ext = u_buf[:, cols]
        acc = ext
        lag = 1
        while lag < w:
            acc = acc + jnp.concatenate([acc[:lag], acc[:-lag]], axis=0)
            lag *= 2
        count = jnp.minimum(pos + 1, w).astype(_F32)
        pooled = acc[POOL_HALO:] / count - ext[POOL_HALO:]
        mixed = _dot(pooled.astype(_BF16), wpool_ref[g]) * pscale_ref[:, cols]
        mixcat[:, cols] = mixed.astype(_BF16)

    col = lax.broadcasted_iota(jnp.int32, (1, GQA_GROUP * BLOCK), 1)
    sink_rows = []
    for h in range(N_KV_HEADS):
        row = jnp.full((1, GQA_GROUP * BLOCK), sinks_ref[h * GQA_GROUP], _F32)
        for g in range(1, GQA_GROUP):
            row = jnp.where(col >= g * BLOCK, sinks_ref[h * GQA_GROUP + g], row)
        sink_rows.append(row)
    first_variant = jnp.where(seq_tile == 0, 1, 0)

    def logits(j):
        band = slice(j * BLOCK, (j + 2) * BLOCK)
        kb = jnp.concatenate([kcat[0, band, :], kcat[1, band, :]], axis=0)
        s = lax.dot_general(kb, q_stack[j], (((1,), (1,)), ((), ())), preferred_element_type=_F32)
        return s + (bias_t[first_variant] if j == 0 else bias_t[0])

    def attend(j, s):
        band = slice(j * BLOCK, (j + 2) * BLOCK)
        outs = []
        for h in range(N_KV_HEADS):
            sh = s[h * 2 * BLOCK:(h + 1) * 2 * BLOCK, :]
            m = jnp.maximum(jnp.max(sh, axis=0, keepdims=True), sink_rows[h])
            p = jnp.exp(sh - m)
            denom = jnp.sum(p, axis=0, keepdims=True) + jnp.exp(sink_rows[h] - m)
            out_h = _dot(vt[h * HEAD_DIM:(h + 1) * HEAD_DIM, band], p.astype(_BF16))
            outs.append(out_h * (1.0 / denom))
        o_t = jnp.concatenate(outs, axis=0)
        blks = []
        for g in range(GQA_GROUP):
            blks.append(o_t[:, g * BLOCK:(g + 1) * BLOCK].T.astype(_BF16))
        return jnp.concatenate(blks, axis=1)

    half = TILE // N_HALVES
    x1_halves, h2_halves, inv_halves = [], [], []
    for r in range(N_HALVES):
        rows = slice(r * half, (r + 1) * half)
        x1_r = xprev_ref[rows, :] + _rmsnorm(mix_buf[rows, :], gpost_ref[...])
        x1_halves.append(x1_r)
        inv_halves.append(lax.rsqrt(jnp.mean(x1_r * x1_r, axis=-1, keepdims=True) + EPS))
        h2_halves.append(x1_r.astype(_BF16))
    next_proj = list(stage_a_compute(xnext_ref))
    s_blocks = [logits(j) for j in range(N_BLOCKS)]
    work = [(r, c) for r in range(N_HALVES) for c in range(N_CHUNKS)]

    def attend_and_store(j):
        mixcat[j * BLOCK:(j + 1) * BLOCK, POOL_WIDTH:] = attend(j, s_blocks[j])

    extras = {}
    for j in range(N_BLOCKS):
        extras[ATTEND_SLOTS[j]] = functools.partial(attend_and_store, j)
    for g in range(N_POOL_GROUPS):
        extras[POOL_SLOTS[g]] = functools.partial(pool_group, g)

    def gate_up(k):
        r, c = work[k]
        return ffn_gate_up(h2_halves[r], inv_halves[r], c)

    act_next = gate_up(0)
    acc = None
    for k, (r, c) in enumerate(work):
        act = act_next
        if k + 1 < len(work):
            act_next = gate_up(k + 1)
        part = ffn_down(c, act)
        acc = part if acc is None else acc + part
        if c == N_CHUNKS - 1:
            o_ref[r * half:(r + 1) * half, :] = x1_halves[r] + _rmsnorm(acc, gpost2_ref[...])
            acc = None
        if k in extras:
            extras[k]()
    mix_buf[...] = _dot(mixcat[...], wout_ref[...])
    stage_a_store(*next_proj, next_starts_seq)


def _resident(shape):
    return pl.BlockSpec(shape, lambda *_: (0,) * len(shape), pipeline_mode=pl.Buffered(1))


def _layer(x, g_pre_mix, w_in, w_pool, pool_scale, rel_bias, sinks, w_out, g_post_mix, g_pre_ffn, w_gate, w_up, w_down,
           g_post_ffn):
    batch, seq, _ = x.shape
    tokens = batch * seq
    n_tiles = tokens // TILE
    tiles_per_seq = seq // TILE
    bucket_t = jnp.asarray(_bucket_table_t())
    x2 = x.reshape(tokens, D_MODEL)
    row = lambda a: a.reshape(1, -1)
    column = lambda a: a.reshape(-1, 1)

    smem = pl.BlockSpec(memory_space=pltpu.SMEM)
    hbm = pl.BlockSpec(memory_space=pl.ANY)
    out = pl.pallas_call(
        functools.partial(_layer_kernel, tiles_per_seq, n_tiles),
        out_shape=jax.ShapeDtypeStruct(x2.shape, x2.dtype),
        grid=(n_tiles + 1,),
        in_specs=[
            pl.BlockSpec((TILE, D_MODEL), lambda s: (jnp.minimum(s + 1, n_tiles - 1), 0)),
            pl.BlockSpec((TILE, D_MODEL), lambda s: (jnp.maximum(s - 1, 0), 0)),
            hbm,
            _resident((N_POOL_GROUPS, POOL_GROUP_DIM, POOL_GROUP_DIM)),
            _resident((1, POOL_WIDTH)),
            smem,
            smem,
            _resident((2 * BLOCK, BLOCK)),
            hbm,
            _resident((1, D_MODEL)),
            hbm,
            hbm,
            hbm,
            _resident((1, D_MODEL)),
            _resident((D_MODEL, 1)),
            _resident((D_MODEL, 1)),
        ],
        out_specs=pl.BlockSpec((TILE, D_MODEL), lambda s: (jnp.maximum(s - 1, 0), 0)),
        scratch_shapes=[
            pltpu.VMEM((POOL_HALO + TILE, POOL_WIDTH), _F32),
            pltpu.VMEM((N_BLOCKS, GQA_GROUP * BLOCK, KV_WIDTH), _BF16),
            pltpu.VMEM((N_KV_HEADS, BLOCK + TILE, KV_WIDTH), _BF16),
            pltpu.VMEM((KV_WIDTH, BLOCK + TILE), _BF16),
            pltpu.VMEM((TILE, D_MODEL), _BF16),
            pltpu.VMEM((TILE, D_MODEL), _F32),
            pltpu.VMEM((2, N_KV_HEADS * 2 * BLOCK, GQA_GROUP * BLOCK), _F32),
            pltpu.VMEM((D_MODEL, IN_WIDTH), _BF16),
            pltpu.VMEM((N_POOL_GROUPS, POOL_GROUP_DIM, POOL_GROUP_DIM), _BF16),
            pltpu.VMEM((D_MODEL, D_MODEL), _BF16),
            pltpu.VMEM((D_MODEL, D_FF), _BF16),
            pltpu.VMEM((D_MODEL, D_FF), _BF16),
            pltpu.VMEM((D_FF, D_MODEL), _BF16),
            pltpu.VMEM((WEIGHT_STAGE_SLOTS, WEIGHT_STAGE_ROWS, D_FF), _F32),
            pltpu.SemaphoreType.DMA((WEIGHT_STAGE_SLOTS,)),
        ],
        compiler_params=pltpu.CompilerParams(
            dimension_semantics=("arbitrary",),
            vmem_limit_bytes=VMEM_LIMIT_BYTES,
        ),
        name="hybrid_layer",
    )(x2, x2, w_in, w_pool, row(pool_scale), rel_bias, sinks, bucket_t, w_out,
      row(g_post_mix), w_gate, w_up, w_down, row(g_post_ffn), column(g_pre_mix), column(g_pre_ffn))
    return out.reshape(x.shape)


def kernel(x, g_pre_mix, w_in, w_pool, pool_scale, rel_bias, sinks, w_out, g_post_mix, g_pre_ffn, w_gate, w_up, w_down, g_post_ffn):
    depth = g_pre_mix.shape[0]
    for l in range(depth):
        x = _layer(x, g_pre_mix[l], w_in[l], w_pool[l], pool_scale[l], rel_bias, sinks[l], w_out[l], g_post_mix[l],
                   g_pre_ffn[l], w_gate[l], w_up[l], w_down[l], g_post_ffn[l])
    return x
```

```python
import functools

import numpy as np
import jax
import jax.numpy as jnp
from jax import lax
from jax.experimental import pallas as pl
from jax.experimental.pallas import tpu as pltpu

D_MODEL = 1024
POOL_WIDTH = 512
POOL_WINDOWS = (2, 4, 8, 16)
N_POOL_GROUPS = len(POOL_WINDOWS)
POOL_GROUP_DIM = POOL_WIDTH // N_POOL_GROUPS
ATTN_WIDTH = 512
HEAD_DIM = 64
N_Q_HEADS = 8
N_KV_HEADS = 2
GQA_GROUP = N_Q_HEADS // N_KV_HEADS
WINDOW = 128
BLOCK = 128
N_BUCKETS = 32
MAX_EXACT = N_BUCKETS // 2
MAX_DISTANCE = 128
KV_WIDTH = N_KV_HEADS * HEAD_DIM
IN_WIDTH = POOL_WIDTH + ATTN_WIDTH + 2 * KV_WIDTH
D_FF = 2816
EPS = 1e-6
NEG_INF = -1e30

POOL_HALO = 16
TILE = 512
N_BLOCKS = TILE // BLOCK
FFN_CHUNK = 256
N_CHUNKS = D_FF // FFN_CHUNK
N_HALVES = 2
ATTEND_SLOTS = (0, 2, 4, 6)
POOL_SLOTS = (8, 10, 12, 14)
STAGE_A_SLOT = 16
WEIGHT_STAGE_ROWS = 128
WEIGHT_STAGE_SLOTS = 4
WOUT_BLOCK_ROWS = HEAD_DIM
VMEM_LIMIT_BYTES = 60 * 1024 * 1024

_F32 = jnp.float32
_BF16 = jnp.bfloat16


def _dot(a, b):
    return jnp.dot(a, b, preferred_element_type=_F32)


def _rmsnorm(xf, g):
    ms = jnp.mean(xf * xf, axis=-1, keepdims=True)
    return xf * lax.rsqrt(ms + EPS) * g


def _bucket_table_t():
    qi = np.arange(BLOCK)[None, :]
    kj = np.arange(2 * BLOCK)[:, None]
    dist = qi + BLOCK - kj
    n = np.maximum(dist, 0)
    nf = np.maximum(n, 1).astype(np.float32)
    large = MAX_EXACT + (
        np.log(nf / np.float32(MAX_EXACT)) / np.float32(np.log(MAX_DISTANCE / MAX_EXACT)) * np.float32(N_BUCKETS - MAX_EXACT)
    ).astype(np.int32)
    large = np.minimum(large, N_BUCKETS - 1)
    bucket = np.where(n < MAX_EXACT, n, large)
    in_window = (dist >= 0) & (dist < WINDOW)
    return np.where(in_window, bucket, -1).astype(np.int32)


def _layer_kernel(tiles_per_seq, n_tiles,
                  xnext_ref, xprev_ref, win_hbm, wpool_f32_ref, pscale_ref, relb_ref, sinks_ref, bkt_ref, wout_hbm,
                  gpost_ref, wg_hbm, wu_hbm, wd_hbm, gpost2_ref, gin_col_ref, gffn_col_ref,
                  o_ref, u_buf, q_stack, kcat, vt, mixcat, mix_buf, bias_t,
                  win_ref, wpool_ref, wout_ref, wg_ref, wu_ref, wd_ref, stage, stage_sem):
    step = pl.program_id(0)
    seq_tile = lax.rem(jnp.minimum(step, n_tiles - 1), tiles_per_seq)
    next_starts_seq = lax.rem(jnp.minimum(step + 1, n_tiles - 1), tiles_per_seq) == 0

    def stage_a_compute(x_ref):
        x = x_ref[...]
        inv_rms = lax.rsqrt(jnp.mean(x * x, axis=-1, keepdims=True) + EPS)
        xb = x.astype(_BF16)
        u = _dot(xb, win_ref[:, 0:POOL_WIDTH]) * inv_rms
        q = (_dot(xb, win_ref[:, POOL_WIDTH:POOL_WIDTH + ATTN_WIDTH]) * inv_rms).astype(_BF16)
        kv = _dot(xb, win_ref[:, POOL_WIDTH + ATTN_WIDTH:IN_WIDTH]) * inv_rms
        return u, q, kv

    def stage_a_store(u, q, kv, starts_seq):
        def history(tail):
            if starts_seq is None:
                return jnp.zeros(tail.shape, tail.dtype)
            return jnp.where(starts_seq, jnp.zeros(tail.shape, tail.dtype), tail[...])

        u_buf[0:POOL_HALO, :] = history(u_buf.at[TILE:TILE + POOL_HALO, :])
        u_buf[POOL_HALO:POOL_HALO + TILE, :] = u
        for j in range(N_BLOCKS):
            for g in range(GQA_GROUP):
                q_stack[j, g * BLOCK:(g + 1) * BLOCK, :] = q[j * BLOCK:(j + 1) * BLOCK, g * BLOCK:(g + 1) * BLOCK]
        k = kv[:, 0:KV_WIDTH]
        v = kv[:, KV_WIDTH:2 * KV_WIDTH]
        lane = lax.broadcasted_iota(jnp.int32, k.shape, 1)
        kcat[:, 0:BLOCK, :] = history(kcat.at[:, TILE:TILE + BLOCK, :])
        kcat[0, BLOCK:BLOCK + TILE, :] = jnp.where(lane < HEAD_DIM, k, 0.0).astype(_BF16)
        kcat[1, BLOCK:BLOCK + TILE, :] = jnp.where(lane >= HEAD_DIM, k, 0.0).astype(_BF16)
        v_t = v.T
        row = lax.broadcasted_iota(jnp.int32, v_t.shape, 0)
        vt[:, :, 0:BLOCK] = history(vt.at[:, :, TILE:TILE + BLOCK])
        vt[0, :, BLOCK:BLOCK + TILE] = jnp.where(row < HEAD_DIM, v_t, 0.0).astype(_BF16)
        vt[1, :, BLOCK:BLOCK + TILE] = jnp.where(row >= HEAD_DIM, v_t, 0.0).astype(_BF16)

    def prepare_weights():
        def stream(src, rows_per_chunk, n_chunks, width, src_row, consume, placements):
            per_iter = len(placements)
            n_iter = n_chunks // per_iter

            def copies(i, slot):
                return [pltpu.make_async_copy(src.at[pl.ds(src_row(i * per_iter + k), rows_per_chunk), :],
                                              stage.at[slot, r0:r0 + rows_per_chunk, l0:l0 + width],
                                              stage_sem.at[slot])
                        for k, (r0, l0) in enumerate(placements)]

            def start(i, slot):
                for c in copies(i, slot):
                    c.start()

            for i in range(min(WEIGHT_STAGE_SLOTS - 1, n_iter)):
                start(i, i)

            def body(i, carry):
                slot = lax.rem(i, WEIGHT_STAGE_SLOTS)
                ahead = i + WEIGHT_STAGE_SLOTS - 1

                @pl.when(ahead < n_iter)
                def _prefetch():
                    start(ahead, lax.rem(ahead, WEIGHT_STAGE_SLOTS))

                for c in copies(i, slot):
                    c.wait()
                for k, (r0, l0) in enumerate(placements):
                    consume(i * per_iter + k, stage[slot, r0:r0 + rows_per_chunk, l0:l0 + width])
                return carry

            lax.fori_loop(0, n_iter, body, 0)

        rows = WEIGHT_STAGE_ROWS

        def chunk_rows(i):
            return pl.ds(pl.multiple_of(i * rows, rows), rows)

        def consume_in(i, blk):
            qb = blk[:, POOL_WIDTH:POOL_WIDTH + ATTN_WIDTH]
            heads = [qb[:, (h * GQA_GROUP + g) * HEAD_DIM:(h * GQA_GROUP + g + 1) * HEAD_DIM]
                     for g in range(GQA_GROUP) for h in range(N_KV_HEADS)]
            q_regrouped = jnp.concatenate(heads, axis=1) * (HEAD_DIM ** -0.5)
            full = jnp.concatenate([blk[:, :POOL_WIDTH], q_regrouped, blk[:, POOL_WIDTH + ATTN_WIDTH:]], axis=1)
            win_ref[chunk_rows(i), :] = (gin_col_ref[chunk_rows(i), :] * full).astype(_BF16)

        def consume_gated(dst):
            def consume(i, blk):
                dst[chunk_rows(i), :] = (gffn_col_ref[chunk_rows(i), :] * blk).astype(_BF16)
            return consume

        def consume_down(i, blk):
            wd_ref[chunk_rows(i), :] = blk.astype(_BF16)

        def wout_src_row(i):
            j = i - POOL_WIDTH // WOUT_BLOCK_ROWS
            head = (j % N_KV_HEADS) * GQA_GROUP + j // N_KV_HEADS
            return jnp.where(j < 0, i, POOL_WIDTH // WOUT_BLOCK_ROWS + head) * WOUT_BLOCK_ROWS

        def consume_out(i, blk):
            dst = pl.ds(pl.multiple_of(i * WOUT_BLOCK_ROWS, WOUT_BLOCK_ROWS), WOUT_BLOCK_ROWS)
            wout_ref[dst, :] = blk.astype(_BF16)

        stream(win_hbm, rows, D_MODEL // rows, IN_WIDTH, lambda c: c * rows, consume_in, [(0, 0), (0, IN_WIDTH)])
        stream(wout_hbm, WOUT_BLOCK_ROWS, D_MODEL // WOUT_BLOCK_ROWS, D_MODEL, wout_src_row, consume_out,
               [(0, 0), (WOUT_BLOCK_ROWS, 0), (0, D_MODEL), (WOUT_BLOCK_ROWS, D_MODEL)])
        stream(wg_hbm, rows, D_MODEL // rows, D_FF, lambda c: c * rows, consume_gated(wg_ref), [(0, 0)])
        stream(wu_hbm, rows, D_MODEL // rows, D_FF, lambda c: c * rows, consume_gated(wu_ref), [(0, 0)])
        stream(wd_hbm, rows, D_FF // rows, D_MODEL, lambda c: c * rows, consume_down, [(0, 0), (0, D_MODEL)])
        wpool_ref[...] = wpool_f32_ref[...].astype(_BF16)

    @pl.when(step == 0)
    def _first_step():
        prepare_weights()
        mix_buf[...] = jnp.zeros(mix_buf.shape, _F32)
        stage_a_store(*stage_a_compute(xprev_ref), None)
        bkt = bkt_ref[...]
        key_is_prev = lax.broadcasted_iota(jnp.int32, bkt.shape, 0) < BLOCK
        for h in range(N_KV_HEADS):
            for g in range(GQA_GROUP):
                head = h * GQA_GROUP + g

                def body(bb, tab):
                    return jnp.where(bkt == bb, relb_ref[bb, head], tab)

                tab = lax.fori_loop(0, N_BUCKETS, body, jnp.zeros(bkt.shape, _F32))
                tab = jnp.where(bkt < 0, NEG_INF, tab)
                rows = slice(h * 2 * BLOCK, (h + 1) * 2 * BLOCK)
                cols = slice(g * BLOCK, (g + 1) * BLOCK)
                bias_t[0, rows, cols] = tab
                bias_t[1, rows, cols] = jnp.where(key_is_prev, NEG_INF, tab)

    def ffn_gate_up(h2, inv_rms2, c):
        cols = slice(c * FFN_CHUNK, (c + 1) * FFN_CHUNK)
        gate = _dot(h2, wg_ref[:, cols]) * inv_rms2
        up = _dot(h2, wu_ref[:, cols]) * inv_rms2
        return (gate * (1.0 / (1.0 + jnp.exp(-gate))) * up).astype(_BF16)

    def ffn_down(c, act, rows=slice(None)):
        return _dot(act[rows], wd_ref[c * FFN_CHUNK:(c + 1) * FFN_CHUNK, :])

    pos = seq_tile * TILE + lax.broadcasted_iota(jnp.int32, (TILE, 1), 0)

    def pool_group(g):
        w = POOL_WINDOWS[g]
        cols = slice(g * POOL_GROUP_DIM, (g + 1) * POOL_GROUP_DIM)
        ext = u_buf[:, cols]
        acc = ext
        lag = 1
        while lag < w:
            acc = acc + jnp.concatenate([acc[:lag], acc[:-lag]], axis=0)
            lag *= 2
        count = jnp.minimum(pos + 1, w).astype(_F32)
        pooled = acc[POOL_HALO:] / count - ext[POOL_HALO:]
        mixed = _dot(pooled.astype(_BF16), wpool_ref[g]) * pscale_ref[:, cols]
        mixcat[:, cols] = mixed.astype(_BF16)

    col = lax.broadcasted_iota(jnp.int32, (1, GQA_GROUP * BLOCK), 1)
    sink_rows = []
    for h in range(N_KV_HEADS):
        row = jnp.full((1, GQA_GROUP * BLOCK), sinks_ref[h * GQA_GROUP], _F32)
        for g in range(1, GQA_GROUP):
            row = jnp.where(col >= g * BLOCK, sinks_ref[h * GQA_GROUP + g], row)
        sink_rows.append(row)
    first_variant = jnp.where(seq_tile == 0, 1, 0)

    def logits(j):
        band = slice(j * BLOCK, (j + 2) * BLOCK)
        kb = jnp.concatenate([kcat[0, band, :], kcat[1, band, :]], axis=0)
        s = lax.dot_general(kb, q_stack[j], (((1,), (1,)), ((), ())), preferred_element_type=_F32)
        return s + (bias_t[first_variant] if j == 0 else bias_t[0])

    def attend(j, s):
        band = slice(j * BLOCK, (j + 2) * BLOCK)
        probs, inv_denoms = [], []
        for h in range(N_KV_HEADS):
            sh = s[h * 2 * BLOCK:(h + 1) * 2 * BLOCK, :]
            m = jnp.maximum(jnp.max(sh, axis=0, keepdims=True), sink_rows[h])
            p = jnp.exp(sh - m)
            denom = jnp.sum(p, axis=0, keepdims=True) + jnp.exp(sink_rows[h] - m)
            probs.append(p.astype(_BF16))
            inv_denoms.append(1.0 / denom)
        v_both = jnp.concatenate([vt[h, :, band] for h in range(N_KV_HEADS)], axis=1)
        o_t = _dot(v_both, jnp.concatenate(probs, axis=0))
        o_t = jnp.concatenate([o_t[h * HEAD_DIM:(h + 1) * HEAD_DIM] * inv_denoms[h] for h in range(N_KV_HEADS)], axis=0)
        blks = []
        for g in range(GQA_GROUP):
            blks.append(o_t[:, g * BLOCK:(g + 1) * BLOCK].T.astype(_BF16))
        return jnp.concatenate(blks, axis=1)

    half = TILE // N_HALVES
    x1_halves, h2_halves, inv_halves = [], [], []
    for r in range(N_HALVES):
        rows = slice(r * half, (r + 1) * half)
        x1_r = xprev_ref[rows, :] + _rmsnorm(mix_buf[rows, :], gpost_ref[...])
        x1_halves.append(x1_r)
        inv_halves.append(lax.rsqrt(jnp.mean(x1_r * x1_r, axis=-1, keepdims=True) + EPS))
        h2_halves.append(x1_r.astype(_BF16))
    next_proj = list(stage_a_compute(xnext_ref))
    s_blocks = [logits(j) for j in range(N_BLOCKS)]
    work = [(r, c) for r in range(N_HALVES) for c in range(N_CHUNKS)]

    def attend_and_store(j):
        mixcat[j * BLOCK:(j + 1) * BLOCK, POOL_WIDTH:] = attend(j, s_blocks[j])

    extras = {}
    for j in range(N_BLOCKS):
        extras[ATTEND_SLOTS[j]] = functools.partial(attend_and_store, j)
    for g in range(N_POOL_GROUPS):
        extras[POOL_SLOTS[g]] = functools.partial(pool_group, g)

    def gate_up(k):
        r, c = work[k]
        return ffn_gate_up(h2_halves[r], inv_halves[r], c)

    act_next = gate_up(0)
    acc = None
    for k, (r, c) in enumerate(work):
        act = act_next
        if k + 1 < len(work):
            act_next = gate_up(k + 1)
        part = ffn_down(c, act)
        acc = part if acc is None else acc + part
        if c == N_CHUNKS - 1:
            o_ref[r * half:(r + 1) * half, :] = x1_halves[r] + _rmsnorm(acc, gpost2_ref[...])
            acc = None
        if k in extras:
            extras[k]()
    mix_buf[...] = _dot(mixcat[...], wout_ref[...])
    stage_a_store(*next_proj, next_starts_seq)


def _resident(shape):
    return pl.BlockSpec(shape, lambda *_: (0,) * len(shape), pipeline_mode=pl.Buffered(1))


def _layer(x, g_pre_mix, w_in, w_pool, pool_scale, rel_bias, sinks, w_out, g_post_mix, g_pre_ffn, w_gate, w_up, w_down,
           g_post_ffn):
    batch, seq, _ = x.shape
    tokens = batch * seq
    n_tiles = tokens // TILE
    tiles_per_seq = seq // TILE
    bucket_t = jnp.asarray(_bucket_table_t())
    x2 = x.reshape(tokens, D_MODEL)
    row = lambda a: a.reshape(1, -1)
    column = lambda a: a.reshape(-1, 1)

    smem = pl.BlockSpec(memory_space=pltpu.SMEM)
    hbm = pl.BlockSpec(memory_space=pl.ANY)
    out = pl.pallas_call(
        functools.partial(_layer_kernel, tiles_per_seq, n_tiles),
        out_shape=jax.ShapeDtypeStruct(x2.shape, x2.dtype),
        grid=(n_tiles + 1,),
        in_specs=[
            pl.BlockSpec((TILE, D_MODEL), lambda s: (jnp.minimum(s + 1, n_tiles - 1), 0)),
            pl.BlockSpec((TILE, D_MODEL), lambda s: (jnp.maximum(s - 1, 0), 0)),
            hbm,
            _resident((N_POOL_GROUPS, POOL_GROUP_DIM, POOL_GROUP_DIM)),
            _resident((1, POOL_WIDTH)),
            smem,
            smem,
            _resident((2 * BLOCK, BLOCK)),
            hbm,
            _resident((1, D_MODEL)),
            hbm,
            hbm,
            hbm,
            _resident((1, D_MODEL)),
            _resident((D_MODEL, 1)),
            _resident((D_MODEL, 1)),
        ],
        out_specs=pl.BlockSpec((TILE, D_MODEL), lambda s: (jnp.maximum(s - 1, 0), 0)),
        scratch_shapes=[
            pltpu.VMEM((POOL_HALO + TILE, POOL_WIDTH), _F32),
            pltpu.VMEM((N_BLOCKS, GQA_GROUP * BLOCK, KV_WIDTH), _BF16),
            pltpu.VMEM((N_KV_HEADS, BLOCK + TILE, KV_WIDTH), _BF16),
            pltpu.VMEM((N_KV_HEADS, KV_WIDTH, BLOCK + TILE), _BF16),
            pltpu.VMEM((TILE, D_MODEL), _BF16),
            pltpu.VMEM((TILE, D_MODEL), _F32),
            pltpu.VMEM((2, N_KV_HEADS * 2 * BLOCK, GQA_GROUP * BLOCK), _F32),
            pltpu.VMEM((D_MODEL, IN_WIDTH), _BF16),
            pltpu.VMEM((N_POOL_GROUPS, POOL_GROUP_DIM, POOL_GROUP_DIM), _BF16),
            pltpu.VMEM((D_MODEL, D_MODEL), _BF16),
            pltpu.VMEM((D_MODEL, D_FF), _BF16),
            pltpu.VMEM((D_MODEL, D_FF), _BF16),
            pltpu.VMEM((D_FF, D_MODEL), _BF16),
            pltpu.VMEM((WEIGHT_STAGE_SLOTS, WEIGHT_STAGE_ROWS, D_FF), _F32),
            pltpu.SemaphoreType.DMA((WEIGHT_STAGE_SLOTS,)),
        ],
        compiler_params=pltpu.CompilerParams(
            dimension_semantics=("arbitrary",),
            vmem_limit_bytes=VMEM_LIMIT_BYTES,
        ),
        name="hybrid_layer",
    )(x2, x2, w_in, w_pool, row(pool_scale), rel_bias, sinks, bucket_t, w_out,
      row(g_post_mix), w_gate, w_up, w_down, row(g_post_ffn), column(g_pre_mix), column(g_pre_ffn))
    return out.reshape(x.shape)


def kernel(x, g_pre_mix, w_in, w_pool, pool_scale, rel_bias, sinks, w_out, g_post_mix, g_pre_ffn, w_gate, w_up, w_down, g_post_ffn):
    depth = g_pre_mix.shape[0]
    for l in range(depth):
        x = _layer(x, g_pre_mix[l], w_in[l], w_pool[l], pool_scale[l], rel_bias, sinks[l], w_out[l], g_post_mix[l],
                   g_pre_ffn[l], w_gate[l], w_up[l], w_down[l], g_post_ffn[l])
    return x
```

```python
import functools

import numpy as np
import jax
import jax.numpy as jnp
from jax import lax
from jax.experimental import pallas as pl
from jax.experimental.pallas import tpu as pltpu

D_MODEL = 1024
POOL_WIDTH = 512
POOL_WINDOWS = (2, 4, 8, 16)
N_POOL_GROUPS = len(POOL_WINDOWS)
POOL_GROUP_DIM = POOL_WIDTH // N_POOL_GROUPS
ATTN_WIDTH = 512
HEAD_DIM = 64
N_Q_HEADS = 8
N_KV_HEADS = 2
GQA_GROUP = N_Q_HEADS // N_KV_HEADS
WINDOW = 128
BLOCK = 128
N_BUCKETS = 32
MAX_EXACT = N_BUCKETS // 2
MAX_DISTANCE = 128
KV_WIDTH = N_KV_HEADS * HEAD_DIM
IN_WIDTH = POOL_WIDTH + ATTN_WIDTH + 2 * KV_WIDTH
D_FF = 2816
EPS = 1e-6
NEG_INF = -1e30

POOL_HALO = 16
TILE = 512
N_BLOCKS = TILE // BLOCK
FFN_CHUNK = 256
N_CHUNKS = D_FF // FFN_CHUNK
N_HALVES = 2
X_RING_SLOTS = 4
ATTEND_SLOTS = (0, 2, 4, 6)
POOL_SLOTS = (8, 10, 12, 14)
WEIGHT_STAGE_ROWS = 128
WEIGHT_STAGE_SLOTS = 4
WOUT_BLOCK_ROWS = HEAD_DIM
VMEM_LIMIT_BYTES = 60 * 1024 * 1024

_F32 = jnp.float32
_BF16 = jnp.bfloat16


def _dot(a, b):
    return jnp.dot(a, b, preferred_element_type=_F32)


def _rmsnorm(xf, g):
    ms = jnp.mean(xf * xf, axis=-1, keepdims=True)
    return xf * lax.rsqrt(ms + EPS) * g


def _bucket_table_t():
    qi = np.arange(BLOCK)[None, :]
    kj = np.arange(2 * BLOCK)[:, None]
    dist = qi + BLOCK - kj
    n = np.maximum(dist, 0)
    nf = np.maximum(n, 1).astype(np.float32)
    large = MAX_EXACT + (
        np.log(nf / np.float32(MAX_EXACT)) / np.float32(np.log(MAX_DISTANCE / MAX_EXACT)) * np.float32(N_BUCKETS - MAX_EXACT)
    ).astype(np.int32)
    large = np.minimum(large, N_BUCKETS - 1)
    bucket = np.where(n < MAX_EXACT, n, large)
    in_window = (dist >= 0) & (dist < WINDOW)
    return np.where(in_window, bucket, -1).astype(np.int32)


def _layer_kernel(tiles_per_seq, n_tiles,
                  x_hbm, win_hbm, wpool_f32_ref, pscale_ref, relb_ref, sinks_ref, bkt_ref, wout_hbm,
                  gpost_ref, wg_hbm, wu_hbm, wd_hbm, gpost2_ref, gin_col_ref, gffn_col_ref,
                  y_hbm, u_buf, q_stack, kcat, vt, mixcat, mix_buf, bias_t,
                  win_ref, wpool_ref, wout_ref, wg_ref, wu_ref, wd_ref, stage, stage_sem,
                  x_ring, x_sem, out_buf, out_sem):
    def tile_rows(tile):
        start = tile * TILE
        return pl.ds(start if isinstance(start, int) else pl.multiple_of(start, TILE), TILE)

    def x_copy(tile, slot):
        return pltpu.make_async_copy(x_hbm.at[tile_rows(tile), :], x_ring.at[slot], x_sem.at[slot])

    def y_copy(tile, slot):
        return pltpu.make_async_copy(out_buf.at[slot], y_hbm.at[tile_rows(tile), :], out_sem.at[slot])

    def stage_a_compute(x_ref):
        x = x_ref[...]
        inv_rms = lax.rsqrt(jnp.mean(x * x, axis=-1, keepdims=True) + EPS)
        xb = x.astype(_BF16)
        u = _dot(xb, win_ref[:, 0:POOL_WIDTH]) * inv_rms
        q = (_dot(xb, win_ref[:, POOL_WIDTH:POOL_WIDTH + ATTN_WIDTH]) * inv_rms).astype(_BF16)
        kv = _dot(xb, win_ref[:, POOL_WIDTH + ATTN_WIDTH:IN_WIDTH]) * inv_rms
        return u, q, kv

    def stage_a_store(u, q, kv, starts_seq):
        def history(tail):
            if starts_seq is None:
                return jnp.zeros(tail.shape, tail.dtype)
            return jnp.where(starts_seq, jnp.zeros(tail.shape, tail.dtype), tail[...])

        u_buf[0:POOL_HALO, :] = history(u_buf.at[TILE:TILE + POOL_HALO, :])
        u_buf[POOL_HALO:POOL_HALO + TILE, :] = u
        for j in range(N_BLOCKS):
            for g in range(GQA_GROUP):
                q_stack[j, g * BLOCK:(g + 1) * BLOCK, :] = q[j * BLOCK:(j + 1) * BLOCK, g * BLOCK:(g + 1) * BLOCK]
        k = kv[:, 0:KV_WIDTH]
        v = kv[:, KV_WIDTH:2 * KV_WIDTH]
        lane = lax.broadcasted_iota(jnp.int32, k.shape, 1)
        kcat[:, 0:BLOCK, :] = history(kcat.at[:, TILE:TILE + BLOCK, :])
        kcat[0, BLOCK:BLOCK + TILE, :] = jnp.where(lane < HEAD_DIM, k, 0.0).astype(_BF16)
        kcat[1, BLOCK:BLOCK + TILE, :] = jnp.where(lane >= HEAD_DIM, k, 0.0).astype(_BF16)
        v_t = v.T
        row = lax.broadcasted_iota(jnp.int32, v_t.shape, 0)
        vt[:, :, 0:BLOCK] = history(vt.at[:, :, TILE:TILE + BLOCK])
        vt[0, :, BLOCK:BLOCK + TILE] = jnp.where(row < HEAD_DIM, v_t, 0.0).astype(_BF16)
        vt[1, :, BLOCK:BLOCK + TILE] = jnp.where(row >= HEAD_DIM, v_t, 0.0).astype(_BF16)

    def prepare_weights():
        def stream(src, rows_per_chunk, n_chunks, width, src_row, consume, placements):
            per_iter = len(placements)
            n_iter = n_chunks // per_iter

            def copies(i, slot):
                return [pltpu.make_async_copy(src.at[pl.ds(src_row(i * per_iter + k), rows_per_chunk), :],
                                              stage.at[slot, r0:r0 + rows_per_chunk, l0:l0 + width],
                                              stage_sem.at[slot])
                        for k, (r0, l0) in enumerate(placements)]

            def start(i, slot):
                for c in copies(i, slot):
                    c.start()

            for i in range(min(WEIGHT_STAGE_SLOTS - 1, n_iter)):
                start(i, i)

            def body(i, carry):
                slot = lax.rem(i, WEIGHT_STAGE_SLOTS)
                ahead = i + WEIGHT_STAGE_SLOTS - 1

                @pl.when(ahead < n_iter)
                def _prefetch():
                    start(ahead, lax.rem(ahead, WEIGHT_STAGE_SLOTS))

                for c in copies(i, slot):
                    c.wait()
                for k, (r0, l0) in enumerate(placements):
                    consume(i * per_iter + k, stage[slot, r0:r0 + rows_per_chunk, l0:l0 + width])
                return carry

            lax.fori_loop(0, n_iter, body, 0)

        rows = WEIGHT_STAGE_ROWS

        def chunk_rows(i):
            return pl.ds(pl.multiple_of(i * rows, rows), rows)

        def consume_in(i, blk):
            qb = blk[:, POOL_WIDTH:POOL_WIDTH + ATTN_WIDTH]
            heads = [qb[:, (h * GQA_GROUP + g) * HEAD_DIM:(h * GQA_GROUP + g + 1) * HEAD_DIM]
                     for g in range(GQA_GROUP) for h in range(N_KV_HEADS)]
            q_regrouped = jnp.concatenate(heads, axis=1) * (HEAD_DIM ** -0.5)
            full = jnp.concatenate([blk[:, :POOL_WIDTH], q_regrouped, blk[:, POOL_WIDTH + ATTN_WIDTH:]], axis=1)
            win_ref[chunk_rows(i), :] = (gin_col_ref[chunk_rows(i), :] * full).astype(_BF16)

        def consume_gated(dst):
            def consume(i, blk):
                dst[chunk_rows(i), :] = (gffn_col_ref[chunk_rows(i), :] * blk).astype(_BF16)
            return consume

        def consume_down(i, blk):
            wd_ref[chunk_rows(i), :] = blk.astype(_BF16)

        def wout_src_row(i):
            j = i - POOL_WIDTH // WOUT_BLOCK_ROWS
            head = (j % N_KV_HEADS) * GQA_GROUP + j // N_KV_HEADS
            return jnp.where(j < 0, i, POOL_WIDTH // WOUT_BLOCK_ROWS + head) * WOUT_BLOCK_ROWS

        def consume_out(i, blk):
            dst = pl.ds(pl.multiple_of(i * WOUT_BLOCK_ROWS, WOUT_BLOCK_ROWS), WOUT_BLOCK_ROWS)
            wout_ref[dst, :] = blk.astype(_BF16)

        stream(win_hbm, rows, D_MODEL // rows, IN_WIDTH, lambda c: c * rows, consume_in, [(0, 0), (0, IN_WIDTH)])
        stream(wout_hbm, WOUT_BLOCK_ROWS, D_MODEL // WOUT_BLOCK_ROWS, D_MODEL, wout_src_row, consume_out,
               [(0, 0), (WOUT_BLOCK_ROWS, 0), (0, D_MODEL), (WOUT_BLOCK_ROWS, D_MODEL)])
        stream(wg_hbm, rows, D_MODEL // rows, D_FF, lambda c: c * rows, consume_gated(wg_ref), [(0, 0)])
        stream(wu_hbm, rows, D_MODEL // rows, D_FF, lambda c: c * rows, consume_gated(wu_ref), [(0, 0)])
        stream(wd_hbm, rows, D_FF // rows, D_MODEL, lambda c: c * rows, consume_down, [(0, 0), (0, D_MODEL)])
        wpool_ref[...] = wpool_f32_ref[...].astype(_BF16)

    def first_step():
        x_copy(0, 0).start()
        x_copy(1, 1).start()
        prepare_weights()
        mix_buf[...] = jnp.zeros(mix_buf.shape, _F32)
        x_copy(0, 0).wait()
        stage_a_store(*stage_a_compute(x_ring.at[0]), None)
        bkt = bkt_ref[...]
        key_is_prev = lax.broadcasted_iota(jnp.int32, bkt.shape, 0) < BLOCK
        for h in range(N_KV_HEADS):
            for g in range(GQA_GROUP):
                head = h * GQA_GROUP + g

                def body(bb, tab):
                    return jnp.where(bkt == bb, relb_ref[bb, head], tab)

                tab = lax.fori_loop(0, N_BUCKETS, body, jnp.zeros(bkt.shape, _F32))
                tab = jnp.where(bkt < 0, NEG_INF, tab)
                rows = slice(h * 2 * BLOCK, (h + 1) * 2 * BLOCK)
                cols = slice(g * BLOCK, (g + 1) * BLOCK)
                bias_t[0, rows, cols] = tab
                bias_t[1, rows, cols] = jnp.where(key_is_prev, NEG_INF, tab)

    def step_body(step, carry):
        seq_tile = lax.rem(jnp.minimum(step, n_tiles - 1), tiles_per_seq)
        next_starts_seq = lax.rem(jnp.minimum(step + 1, n_tiles - 1), tiles_per_seq) == 0
        out_slot = lax.rem(step, 2)

        @pl.when(step + 2 < n_tiles)
        def _prefetch_x():
            x_copy(step + 2, lax.rem(step + 2, X_RING_SLOTS)).start()

        @pl.when(step + 1 < n_tiles)
        def _wait_x():
            x_copy(step + 1, lax.rem(step + 1, X_RING_SLOTS)).wait()

        @pl.when(step >= 3)
        def _wait_y():
            y_copy(step - 3, out_slot).wait()

        x_next = x_ring.at[lax.rem(step + 1, X_RING_SLOTS)]
        x_prev = x_ring.at[lax.rem(jnp.maximum(step - 1, 0), X_RING_SLOTS)]
        y_tile = out_buf.at[out_slot]

        def ffn_gate_up(h2, inv_rms2, c):
            cols = slice(c * FFN_CHUNK, (c + 1) * FFN_CHUNK)
            gate = _dot(h2, wg_ref[:, cols]) * inv_rms2
            up = _dot(h2, wu_ref[:, cols]) * inv_rms2
            return (gate * (1.0 / (1.0 + jnp.exp(-gate))) * up).astype(_BF16)

        def ffn_down(c, act):
            return _dot(act, wd_ref[c * FFN_CHUNK:(c + 1) * FFN_CHUNK, :])

        pos = seq_tile * TILE + lax.broadcasted_iota(jnp.int32, (TILE, 1), 0)

        def pool_group(g):
            w = POOL_WINDOWS[g]
            cols = slice(g * POOL_GROUP_DIM, (g + 1) * POOL_GROUP_DIM)
            ext = u_buf[:, cols]
            acc = ext
            lag = 1
            while lag < w:
                acc = acc + jnp.concatenate([acc[:lag], acc[:-lag]], axis=0)
                lag *= 2
            count = jnp.minimum(pos + 1, w).astype(_F32)
            pooled = acc[POOL_HALO:] / count - ext[POOL_HALO:]
            mixed = _dot(pooled.astype(_BF16), wpool_ref[g]) * pscale_ref[:, cols]
            mixcat[:, cols] = mixed.astype(_BF16)

        col = lax.broadcasted_iota(jnp.int32, (1, GQA_GROUP * BLOCK), 1)
        sink_rows = []
        for h in range(N_KV_HEADS):
            row = jnp.full((1, GQA_GROUP * BLOCK), sinks_ref[h * GQA_GROUP], _F32)
            for g in range(1, GQA_GROUP):
                row = jnp.where(col >= g * BLOCK, sinks_ref[h * GQA_GROUP + g], row)
            sink_rows.append(row)
        first_variant = jnp.where(seq_tile == 0, 1, 0)

        def logits(j):
            band = slice(j * BLOCK, (j + 2) * BLOCK)
            kb = jnp.concatenate([kcat[0, band, :], kcat[1, band, :]], axis=0)
            s = lax.dot_general(kb, q_stack[j], (((1,), (1,)), ((), ())), preferred_element_type=_F32)
            return s + (bias_t[first_variant] if j == 0 else bias_t[0])

        def attend(j, s):
            band = slice(j * BLOCK, (j + 2) * BLOCK)
            probs, inv_denoms = [], []
            for h in range(N_KV_HEADS):
                sh = s[h * 2 * BLOCK:(h + 1) * 2 * BLOCK, :]
                m = jnp.maximum(jnp.max(sh, axis=0, keepdims=True), sink_rows[h])
                p = jnp.exp(sh - m)
                denom = jnp.sum(p, axis=0, keepdims=True) + jnp.exp(sink_rows[h] - m)
                probs.append(p.astype(_BF16))
                inv_denoms.append(1.0 / denom)
            v_both = jnp.concatenate([vt[h, :, band] for h in range(N_KV_HEADS)], axis=1)
            o_t = _dot(v_both, jnp.concatenate(probs, axis=0))
            o_t = jnp.concatenate([o_t[h * HEAD_DIM:(h + 1) * HEAD_DIM] * inv_denoms[h] for h in range(N_KV_HEADS)], axis=0)
            blks = []
            for g in range(GQA_GROUP):
                blks.append(o_t[:, g * BLOCK:(g + 1) * BLOCK].T.astype(_BF16))
            return jnp.concatenate(blks, axis=1)

        half = TILE // N_HALVES
        x1_halves, h2_halves, inv_halves = [], [], []
        for r in range(N_HALVES):
            rows = slice(r * half, (r + 1) * half)
            x1_r = x_prev[rows, :] + _rmsnorm(mix_buf[rows, :], gpost_ref[...])
            x1_halves.append(x1_r)
            inv_halves.append(lax.rsqrt(jnp.mean(x1_r * x1_r, axis=-1, keepdims=True) + EPS))
            h2_halves.append(x1_r.astype(_BF16))
        next_proj = stage_a_compute(x_next)
        s_blocks = [logits(j) for j in range(N_BLOCKS)]
        work = [(r, c) for r in range(N_HALVES) for c in range(N_CHUNKS)]

        def attend_and_store(j):
            mixcat[j * BLOCK:(j + 1) * BLOCK, POOL_WIDTH:] = attend(j, s_blocks[j])

        extras = {}
        for j in range(N_BLOCKS):
            extras[ATTEND_SLOTS[j]] = functools.partial(attend_and_store, j)
        for g in range(N_POOL_GROUPS):
            extras[POOL_SLOTS[g]] = functools.partial(pool_group, g)

        def gate_up(k):
            r, c = work[k]
            return ffn_gate_up(h2_halves[r], inv_halves[r], c)

        act_next = gate_up(0)
        acc = None
        for k, (r, c) in enumerate(work):
            act = act_next
            if k + 1 < len(work):
                act_next = gate_up(k + 1)
            part = ffn_down(c, act)
            acc = part if acc is None else acc + part
            if c == N_CHUNKS - 1:
                y_tile[r * half:(r + 1) * half, :] = x1_halves[r] + _rmsnorm(acc, gpost2_ref[...])
                acc = None
            if k in extras:
                extras[k]()
        mix_buf[...] = _dot(mixcat[...], wout_ref[...])
        stage_a_store(*next_proj, next_starts_seq)

        @pl.when(step >= 1)
        def _store_y():
            y_copy(step - 1, out_slot).start()

        return carry

    first_step()
    lax.fori_loop(0, n_tiles + 1, step_body, 0)
    y_copy(n_tiles - 2, (n_tiles - 1) % 2).wait()
    y_copy(n_tiles - 1, n_tiles % 2).wait()


def _layer(x, g_pre_mix, w_in, w_pool, pool_scale, rel_bias, sinks, w_out, g_post_mix, g_pre_ffn, w_gate, w_up, w_down,
           g_post_ffn):
    batch, seq, _ = x.shape
    tokens = batch * seq
    n_tiles = tokens // TILE
    tiles_per_seq = seq // TILE
    bucket_t = jnp.asarray(_bucket_table_t())
    x2 = x.reshape(tokens, D_MODEL)
    row = lambda a: a.reshape(1, -1)
    column = lambda a: a.reshape(-1, 1)

    vmem = pl.BlockSpec(memory_space=pltpu.VMEM)
    smem = pl.BlockSpec(memory_space=pltpu.SMEM)
    hbm = pl.BlockSpec(memory_space=pl.ANY)
    out = pl.pallas_call(
        functools.partial(_layer_kernel, tiles_per_seq, n_tiles),
        out_shape=jax.ShapeDtypeStruct(x2.shape, x2.dtype),
        in_specs=[hbm, hbm, vmem, vmem, smem, smem, vmem, hbm, vmem, hbm, hbm, hbm, vmem, vmem, vmem],
        out_specs=hbm,
        scratch_shapes=[
            pltpu.VMEM((POOL_HALO + TILE, POOL_WIDTH), _F32),
            pltpu.VMEM((N_BLOCKS, GQA_GROUP * BLOCK, KV_WIDTH), _BF16),
            pltpu.VMEM((N_KV_HEADS, BLOCK + TILE, KV_WIDTH), _BF16),
            pltpu.VMEM((N_KV_HEADS, KV_WIDTH, BLOCK + TILE), _BF16),
            pltpu.VMEM((TILE, D_MODEL), _BF16),
            pltpu.VMEM((TILE, D_MODEL), _F32),
            pltpu.VMEM((2, N_KV_HEADS * 2 * BLOCK, GQA_GROUP * BLOCK), _F32),
            pltpu.VMEM((D_MODEL, IN_WIDTH), _BF16),
            pltpu.VMEM((N_POOL_GROUPS, POOL_GROUP_DIM, POOL_GROUP_DIM), _BF16),
            pltpu.VMEM((D_MODEL, D_MODEL), _BF16),
            pltpu.VMEM((D_MODEL, D_FF), _BF16),
            pltpu.VMEM((D_MODEL, D_FF), _BF16),
            pltpu.VMEM((D_FF, D_MODEL), _BF16),
            pltpu.VMEM((WEIGHT_STAGE_SLOTS, WEIGHT_STAGE_ROWS, D_FF), _F32),
            pltpu.SemaphoreType.DMA((WEIGHT_STAGE_SLOTS,)),
            pltpu.VMEM((X_RING_SLOTS, TILE, D_MODEL), _F32),
            pltpu.SemaphoreType.DMA((X_RING_SLOTS,)),
            pltpu.VMEM((2, TILE, D_MODEL), _F32),
            pltpu.SemaphoreType.DMA((2,)),
        ],
        compiler_params=pltpu.CompilerParams(vmem_limit_bytes=VMEM_LIMIT_BYTES),
        name="hybrid_layer",
    )(x2, w_in, w_pool, row(pool_scale), rel_bias, sinks, bucket_t, w_out,
      row(g_post_mix), w_gate, w_up, w_down, row(g_post_ffn), column(g_pre_mix), column(g_pre_ffn))
    return out.reshape(x.shape)


def kernel(x, g_pre_mix, w_in, w_pool, pool_scale, rel_bias, sinks, w_out, g_post_mix, g_pre_ffn, w_gate, w_up, w_down, g_post_ffn):
    depth = g_pre_mix.shape[0]
    for l in range(depth):
        x = _layer(x, g_pre_mix[l], w_in[l], w_pool[l], pool_scale[l], rel_bias, sinks[l], w_out[l], g_post_mix[l],
                   g_pre_ffn[l], w_gate[l], w_up[l], w_down[l], g_post_ffn[l])
    return x
```

```python
import functools

import numpy as np
import jax
import jax.numpy as jnp
from jax import lax
from jax.experimental import pallas as pl
from jax.experimental.pallas import tpu as pltpu

D_MODEL = 1024
POOL_WIDTH = 512
POOL_WINDOWS = (2, 4, 8, 16)
N_POOL_GROUPS = len(POOL_WINDOWS)
POOL_GROUP_DIM = POOL_WIDTH // N_POOL_GROUPS
ATTN_WIDTH = 512
HEAD_DIM = 64
N_Q_HEADS = 8
N_KV_HEADS = 2
GQA_GROUP = N_Q_HEADS // N_KV_HEADS
WINDOW = 128
BLOCK = 128
N_BUCKETS = 32
MAX_EXACT = N_BUCKETS // 2
MAX_DISTANCE = 128
KV_WIDTH = N_KV_HEADS * HEAD_DIM
IN_WIDTH = POOL_WIDTH + ATTN_WIDTH + 2 * KV_WIDTH
D_FF = 2816
EPS = 1e-6
NEG_INF = -1e30

POOL_HALO = 16
TILE = 512
N_BLOCKS = TILE // BLOCK
FFN_CHUNK = 256
N_CHUNKS = D_FF // FFN_CHUNK
N_HALVES = 2
X_RING_SLOTS = 4
ATTEND_SLOTS = (0, 2, 4, 6)
POOL_SLOTS = (8, 10, 12, 14)
POOL_PROJECTION_SLOT = 16
WEIGHT_STAGE_ROWS = 128
WEIGHT_STAGE_SLOTS = 4
WOUT_BLOCK_ROWS = HEAD_DIM
VMEM_LIMIT_BYTES = 60 * 1024 * 1024

_F32 = jnp.float32
_BF16 = jnp.bfloat16


def _dot(a, b):
    return jnp.dot(a, b, preferred_element_type=_F32)


def _rmsnorm(xf, g):
    ms = jnp.mean(xf * xf, axis=-1, keepdims=True)
    return xf * lax.rsqrt(ms + EPS) * g


def _bucket_table_t():
    qi = np.arange(BLOCK)[None, :]
    kj = np.arange(2 * BLOCK)[:, None]
    dist = qi + BLOCK - kj
    n = np.maximum(dist, 0)
    nf = np.maximum(n, 1).astype(np.float32)
    large = MAX_EXACT + (
        np.log(nf / np.float32(MAX_EXACT)) / np.float32(np.log(MAX_DISTANCE / MAX_EXACT)) * np.float32(N_BUCKETS - MAX_EXACT)
    ).astype(np.int32)
    large = np.minimum(large, N_BUCKETS - 1)
    bucket = np.where(n < MAX_EXACT, n, large)
    in_window = (dist >= 0) & (dist < WINDOW)
    return np.where(in_window, bucket, -1).astype(np.int32)


def _layer_kernel(tiles_per_seq, n_tiles,
                  x_hbm, win_hbm, wpool_f32_ref, pscale_ref, relb_ref, sinks_ref, bkt_ref, wout_hbm,
                  gpost_ref, wg_hbm, wu_hbm, wd_hbm, gpost2_ref, gin_col_ref, gffn_col_ref,
                  y_hbm, u_buf, q_stack, kcat, vt, mixcat, mix_buf, bias_t,
                  win_ref, wout_ref, wg_ref, wu_ref, wd_ref, stage, stage_sem,
                  x_ring, x_sem, out_buf, out_sem):
    def tile_rows(tile):
        start = tile * TILE
        return pl.ds(start if isinstance(start, int) else pl.multiple_of(start, TILE), TILE)

    def x_copy(tile, slot):
        return pltpu.make_async_copy(x_hbm.at[tile_rows(tile), :], x_ring.at[slot], x_sem.at[slot])

    def y_copy(tile, slot):
        return pltpu.make_async_copy(out_buf.at[slot], y_hbm.at[tile_rows(tile), :], out_sem.at[slot])

    def stage_a_compute(x_ref):
        x = x_ref[...]
        inv_rms = lax.rsqrt(jnp.mean(x * x, axis=-1, keepdims=True) + EPS)
        xb = x.astype(_BF16)
        u = _dot(xb, win_ref[:, 0:POOL_WIDTH]) * inv_rms
        q = (_dot(xb, win_ref[:, POOL_WIDTH:POOL_WIDTH + ATTN_WIDTH]) * inv_rms).astype(_BF16)
        kv = _dot(xb, win_ref[:, POOL_WIDTH + ATTN_WIDTH:IN_WIDTH]) * inv_rms
        return u, q, kv

    def stage_a_store(u, q, kv, starts_seq):
        def history(tail):
            if starts_seq is None:
                return jnp.zeros(tail.shape, tail.dtype)
            return jnp.where(starts_seq, jnp.zeros(tail.shape, tail.dtype), tail[...])

        u_buf[0:POOL_HALO, :] = history(u_buf.at[TILE:TILE + POOL_HALO, :])
        u_buf[POOL_HALO:POOL_HALO + TILE, :] = u
        for j in range(N_BLOCKS):
            for g in range(GQA_GROUP):
                q_stack[j, g * BLOCK:(g + 1) * BLOCK, :] = q[j * BLOCK:(j + 1) * BLOCK, g * BLOCK:(g + 1) * BLOCK]
        k = kv[:, 0:KV_WIDTH]
        v = kv[:, KV_WIDTH:2 * KV_WIDTH]
        lane = lax.broadcasted_iota(jnp.int32, k.shape, 1)
        kcat[:, 0:BLOCK, :] = history(kcat.at[:, TILE:TILE + BLOCK, :])
        kcat[0, BLOCK:BLOCK + TILE, :] = jnp.where(lane < HEAD_DIM, k, 0.0).astype(_BF16)
        kcat[1, BLOCK:BLOCK + TILE, :] = jnp.where(lane >= HEAD_DIM, k, 0.0).astype(_BF16)
        v_t = v.T
        row = lax.broadcasted_iota(jnp.int32, v_t.shape, 0)
        vt[:, :, 0:BLOCK] = history(vt.at[:, :, TILE:TILE + BLOCK])
        vt[0, :, BLOCK:BLOCK + TILE] = jnp.where(row < HEAD_DIM, v_t, 0.0).astype(_BF16)
        vt[1, :, BLOCK:BLOCK + TILE] = jnp.where(row >= HEAD_DIM, v_t, 0.0).astype(_BF16)

    def prepare_weights():
        def stream(src, rows_per_chunk, n_chunks, width, src_row, consume, placements):
            per_iter = len(placements)
            n_iter = n_chunks // per_iter

            def copies(i, slot):
                return [pltpu.make_async_copy(src.at[pl.ds(src_row(i * per_iter + k), rows_per_chunk), :],
                                              stage.at[slot, r0:r0 + rows_per_chunk, l0:l0 + width],
                                              stage_sem.at[slot])
                        for k, (r0, l0) in enumerate(placements)]

            def start(i, slot):
                for c in copies(i, slot):
                    c.start()

            for i in range(min(WEIGHT_STAGE_SLOTS - 1, n_iter)):
                start(i, i)

            def body(i, carry):
                slot = lax.rem(i, WEIGHT_STAGE_SLOTS)
                ahead = i + WEIGHT_STAGE_SLOTS - 1

                @pl.when(ahead < n_iter)
                def _prefetch():
                    start(ahead, lax.rem(ahead, WEIGHT_STAGE_SLOTS))

                for c in copies(i, slot):
                    c.wait()
                for k, (r0, l0) in enumerate(placements):
                    consume(i * per_iter + k, stage[slot, r0:r0 + rows_per_chunk, l0:l0 + width])
                return carry

            lax.fori_loop(0, n_iter, body, 0)

        rows = WEIGHT_STAGE_ROWS

        def chunk_rows(i):
            return pl.ds(pl.multiple_of(i * rows, rows), rows)

        def consume_in(i, blk):
            qb = blk[:, POOL_WIDTH:POOL_WIDTH + ATTN_WIDTH]
            heads = [qb[:, (h * GQA_GROUP + g) * HEAD_DIM:(h * GQA_GROUP + g + 1) * HEAD_DIM]
                     for g in range(GQA_GROUP) for h in range(N_KV_HEADS)]
            q_regrouped = jnp.concatenate(heads, axis=1) * (HEAD_DIM ** -0.5)
            full = jnp.concatenate([blk[:, :POOL_WIDTH], q_regrouped, blk[:, POOL_WIDTH + ATTN_WIDTH:]], axis=1)
            win_ref[chunk_rows(i), :] = (gin_col_ref[chunk_rows(i), :] * full).astype(_BF16)

        def consume_gated(dst):
            def consume(i, blk):
                dst[chunk_rows(i), :] = (gffn_col_ref[chunk_rows(i), :] * blk).astype(_BF16)
            return consume

        def consume_down(i, blk):
            wd_ref[chunk_rows(i), :] = blk.astype(_BF16)

        def wout_src_row(j):
            head = (j % N_KV_HEADS) * GQA_GROUP + j // N_KV_HEADS
            return POOL_WIDTH + head * WOUT_BLOCK_ROWS

        def consume_out(j, blk):
            dst = pl.ds(pl.multiple_of(POOL_WIDTH + j * WOUT_BLOCK_ROWS, WOUT_BLOCK_ROWS), WOUT_BLOCK_ROWS)
            wout_ref[dst, :] = blk.astype(_BF16)

        def fold_pool_rows():
            copies = [pltpu.make_async_copy(wout_hbm.at[g * POOL_GROUP_DIM:(g + 1) * POOL_GROUP_DIM, :],
                                            stage.at[g, 0:POOL_GROUP_DIM, 0:D_MODEL], stage_sem.at[g])
                      for g in range(N_POOL_GROUPS)]
            for c in copies:
                c.start()
            for g, c in enumerate(copies):
                cols = slice(g * POOL_GROUP_DIM, (g + 1) * POOL_GROUP_DIM)
                group_map = wpool_f32_ref[g] * pscale_ref[:, cols]
                c.wait()
                folded = jnp.dot(group_map, stage[g, 0:POOL_GROUP_DIM, 0:D_MODEL],
                                 precision=lax.Precision.HIGHEST, preferred_element_type=_F32)
                wout_ref[cols, :] = folded.astype(_BF16)

        stream(win_hbm, rows, D_MODEL // rows, IN_WIDTH, lambda c: c * rows, consume_in, [(0, 0), (0, IN_WIDTH)])
        stream(wout_hbm, WOUT_BLOCK_ROWS, ATTN_WIDTH // WOUT_BLOCK_ROWS, D_MODEL, wout_src_row, consume_out,
               [(0, 0), (WOUT_BLOCK_ROWS, 0), (0, D_MODEL), (WOUT_BLOCK_ROWS, D_MODEL)])
        fold_pool_rows()
        stream(wg_hbm, rows, D_MODEL // rows, D_FF, lambda c: c * rows, consume_gated(wg_ref), [(0, 0)])
        stream(wu_hbm, rows, D_MODEL // rows, D_FF, lambda c: c * rows, consume_gated(wu_ref), [(0, 0)])
        stream(wd_hbm, rows, D_FF // rows, D_MODEL, lambda c: c * rows, consume_down, [(0, 0), (0, D_MODEL)])

    def first_step():
        x_copy(0, 0).start()
        x_copy(1, 1).start()
        prepare_weights()
        mix_buf[...] = jnp.zeros(mix_buf.shape, _F32)
        x_copy(0, 0).wait()
        stage_a_store(*stage_a_compute(x_ring.at[0]), None)
        bkt = bkt_ref[...]
        key_is_prev = lax.broadcasted_iota(jnp.int32, bkt.shape, 0) < BLOCK
        for h in range(N_KV_HEADS):
            for g in range(GQA_GROUP):
                head = h * GQA_GROUP + g

                def body(bb, tab):
                    return jnp.where(bkt == bb, relb_ref[bb, head], tab)

                tab = lax.fori_loop(0, N_BUCKETS, body, jnp.zeros(bkt.shape, _F32))
                tab = jnp.where(bkt < 0, NEG_INF, tab)
                rows = slice(h * 2 * BLOCK, (h + 1) * 2 * BLOCK)
                cols = slice(g * BLOCK, (g + 1) * BLOCK)
                bias_t[0, rows, cols] = tab
                bias_t[1, rows, cols] = jnp.where(key_is_prev, NEG_INF, tab)

    def step_body(step, carry):
        seq_tile = lax.rem(jnp.minimum(step, n_tiles - 1), tiles_per_seq)
        next_starts_seq = lax.rem(jnp.minimum(step + 1, n_tiles - 1), tiles_per_seq) == 0
        out_slot = lax.rem(step, 2)

        @pl.when(step + 2 < n_tiles)
        def _prefetch_x():
            x_copy(step + 2, lax.rem(step + 2, X_RING_SLOTS)).start()

        @pl.when(step + 1 < n_tiles)
        def _wait_x():
            x_copy(step + 1, lax.rem(step + 1, X_RING_SLOTS)).wait()

        @pl.when(step >= 3)
        def _wait_y():
            y_copy(step - 3, out_slot).wait()

        x_next = x_ring.at[lax.rem(step + 1, X_RING_SLOTS)]
        x_prev = x_ring.at[lax.rem(jnp.maximum(step - 1, 0), X_RING_SLOTS)]
        y_tile = out_buf.at[out_slot]

        def ffn_gate_up(h2, inv_rms2, c):
            cols = slice(c * FFN_CHUNK, (c + 1) * FFN_CHUNK)
            gate = _dot(h2, wg_ref[:, cols]) * inv_rms2
            up = _dot(h2, wu_ref[:, cols]) * inv_rms2
            return (gate * (1.0 / (1.0 + jnp.exp(-gate))) * up).astype(_BF16)

        def ffn_down(c, act):
            return _dot(act, wd_ref[c * FFN_CHUNK:(c + 1) * FFN_CHUNK, :])

        pos = seq_tile * TILE + lax.broadcasted_iota(jnp.int32, (TILE, 1), 0)

        def pool_group(g):
            w = POOL_WINDOWS[g]
            cols = slice(g * POOL_GROUP_DIM, (g + 1) * POOL_GROUP_DIM)
            ext = u_buf[:, cols]
            acc = ext
            lag = 1
            while lag < w:
                acc = acc + jnp.concatenate([acc[:lag], acc[:-lag]], axis=0)
                lag *= 2
            count = jnp.minimum(pos + 1, w).astype(_F32)
            pooled = acc[POOL_HALO:] / count - ext[POOL_HALO:]
            mixcat[:, cols] = pooled.astype(_BF16)

        col = lax.broadcasted_iota(jnp.int32, (1, GQA_GROUP * BLOCK), 1)
        sink_rows = []
        for h in range(N_KV_HEADS):
            row = jnp.full((1, GQA_GROUP * BLOCK), sinks_ref[h * GQA_GROUP], _F32)
            for g in range(1, GQA_GROUP):
                row = jnp.where(col >= g * BLOCK, sinks_ref[h * GQA_GROUP + g], row)
            sink_rows.append(row)
        first_variant = jnp.where(seq_tile == 0, 1, 0)

        def logits(j):
            band = slice(j * BLOCK, (j + 2) * BLOCK)
            kb = jnp.concatenate([kcat[0, band, :], kcat[1, band, :]], axis=0)
            s = lax.dot_general(kb, q_stack[j], (((1,), (1,)), ((), ())), preferred_element_type=_F32)
            return s + (bias_t[first_variant] if j == 0 else bias_t[0])

        def attend(j, s):
            band = slice(j * BLOCK, (j + 2) * BLOCK)
            probs, inv_denoms = [], []
            for h in range(N_KV_HEADS):
                sh = s[h * 2 * BLOCK:(h + 1) * 2 * BLOCK, :]
                m = jnp.maximum(jnp.max(sh, axis=0, keepdims=True), sink_rows[h])
                p = jnp.exp(sh - m)
                denom = jnp.sum(p, axis=0, keepdims=True) + jnp.exp(sink_rows[h] - m)
                probs.append(p.astype(_BF16))
                inv_denoms.append(1.0 / denom)
            v_both = jnp.concatenate([vt[h, :, band] for h in range(N_KV_HEADS)], axis=1)
            o_t = _dot(v_both, jnp.concatenate(probs, axis=0))
            o_t = jnp.concatenate([o_t[h * HEAD_DIM:(h + 1) * HEAD_DIM] * inv_denoms[h] for h in range(N_KV_HEADS)], axis=0)
            blks = []
            for g in range(GQA_GROUP):
                blks.append(o_t[:, g * BLOCK:(g + 1) * BLOCK].T.astype(_BF16))
            return jnp.concatenate(blks, axis=1)

        half = TILE // N_HALVES
        x1_halves, h2_halves, inv_halves = [], [], []
        for r in range(N_HALVES):
            rows = slice(r * half, (r + 1) * half)
            x1_r = x_prev[rows, :] + _rmsnorm(mix_buf[rows, :], gpost_ref[...])
            x1_halves.append(x1_r)
            inv_halves.append(lax.rsqrt(jnp.mean(x1_r * x1_r, axis=-1, keepdims=True) + EPS))
            h2_halves.append(x1_r.astype(_BF16))
        next_proj = stage_a_compute(x_next)
        s_blocks = [logits(j) for j in range(N_BLOCKS)]
        work = [(r, c) for r in range(N_HALVES) for c in range(N_CHUNKS)]

        def attend_and_store(j):
            mixcat[j * BLOCK:(j + 1) * BLOCK, POOL_WIDTH:] = attend(j, s_blocks[j])

        extras = {}
        for j in range(N_BLOCKS):
            extras[ATTEND_SLOTS[j]] = functools.partial(attend_and_store, j)
        for g in range(N_POOL_GROUPS):
            extras[POOL_SLOTS[g]] = functools.partial(pool_group, g)
        pool_mix = []
        extras[POOL_PROJECTION_SLOT] = lambda: pool_mix.append(
            _dot(mixcat[:, 0:POOL_WIDTH], wout_ref[0:POOL_WIDTH, :]))

        def gate_up(k):
            r, c = work[k]
            return ffn_gate_up(h2_halves[r], inv_halves[r], c)

        act_next = gate_up(0)
        acc = None
        for k, (r, c) in enumerate(work):
            act = act_next
            if k + 1 < len(work):
                act_next = gate_up(k + 1)
            part = ffn_down(c, act)
            acc = part if acc is None else acc + part
            if c == N_CHUNKS - 1:
                y_tile[r * half:(r + 1) * half, :] = x1_halves[r] + _rmsnorm(acc, gpost2_ref[...])
                acc = None
            if k in extras:
                extras[k]()
        mix_buf[...] = pool_mix[0] + _dot(mixcat[:, POOL_WIDTH:], wout_ref[POOL_WIDTH:, :])
        stage_a_store(*next_proj, next_starts_seq)

        @pl.when(step >= 1)
        def _store_y():
            y_copy(step - 1, out_slot).start()

        return carry

    first_step()
    lax.fori_loop(0, n_tiles + 1, step_body, 0)
    y_copy(n_tiles - 2, (n_tiles - 1) % 2).wait()
    y_copy(n_tiles - 1, n_tiles % 2).wait()


def _layer(x, g_pre_mix, w_in, w_pool, pool_scale, rel_bias, sinks, w_out, g_post_mix, g_pre_ffn, w_gate, w_up, w_down,
           g_post_ffn):
    batch, seq, _ = x.shape
    tokens = batch * seq
    n_tiles = tokens // TILE
    tiles_per_seq = seq // TILE
    bucket_t = jnp.asarray(_bucket_table_t())
    x2 = x.reshape(tokens, D_MODEL)
    row = lambda a: a.reshape(1, -1)
    column = lambda a: a.reshape(-1, 1)

    vmem = pl.BlockSpec(memory_space=pltpu.VMEM)
    smem = pl.BlockSpec(memory_space=pltpu.SMEM)
    hbm = pl.BlockSpec(memory_space=pl.ANY)
    out = pl.pallas_call(
        functools.partial(_layer_kernel, tiles_per_seq, n_tiles),
        out_shape=jax.ShapeDtypeStruct(x2.shape, x2.dtype),
        in_specs=[hbm, hbm, vmem, vmem, smem, smem, vmem, hbm, vmem, hbm, hbm, hbm, vmem, vmem, vmem],
        out_specs=hbm,
        scratch_shapes=[
            pltpu.VMEM((POOL_HALO + TILE, POOL_WIDTH), _F32),
            pltpu.VMEM((N_BLOCKS, GQA_GROUP * BLOCK, KV_WIDTH), _BF16),
            pltpu.VMEM((N_KV_HEADS, BLOCK + TILE, KV_WIDTH), _BF16),
            pltpu.VMEM((N_KV_HEADS, KV_WIDTH, BLOCK + TILE), _BF16),
            pltpu.VMEM((TILE, D_MODEL), _BF16),
            pltpu.VMEM((TILE, D_MODEL), _F32),
            pltpu.VMEM((2, N_KV_HEADS * 2 * BLOCK, GQA_GROUP * BLOCK), _F32),
            pltpu.VMEM((D_MODEL, IN_WIDTH), _BF16),
            pltpu.VMEM((D_MODEL, D_MODEL), _BF16),
            pltpu.VMEM((D_MODEL, D_FF), _BF16),
            pltpu.VMEM((D_MODEL, D_FF), _BF16),
            pltpu.VMEM((D_FF, D_MODEL), _BF16),
            pltpu.VMEM((WEIGHT_STAGE_SLOTS, WEIGHT_STAGE_ROWS, D_FF), _F32),
            pltpu.SemaphoreType.DMA((WEIGHT_STAGE_SLOTS,)),
            pltpu.VMEM((X_RING_SLOTS, TILE, D_MODEL), _F32),
            pltpu.SemaphoreType.DMA((X_RING_SLOTS,)),
            pltpu.VMEM((2, TILE, D_MODEL), _F32),
            pltpu.SemaphoreType.DMA((2,)),
        ],
        compiler_params=pltpu.CompilerParams(vmem_limit_bytes=VMEM_LIMIT_BYTES),
        name="hybrid_layer",
    )(x2, w_in, w_pool, row(pool_scale), rel_bias, sinks, bucket_t, w_out,
      row(g_post_mix), w_gate, w_up, w_down, row(g_post_ffn), column(g_pre_mix), column(g_pre_ffn))
    return out.reshape(x.shape)


def kernel(x, g_pre_mix, w_in, w_pool, pool_scale, rel_bias, sinks, w_out, g_post_mix, g_pre_ffn, w_gate, w_up, w_down, g_post_ffn):
    depth = g_pre_mix.shape[0]
    for l in range(depth):
        x = _layer(x, g_pre_mix[l], w_in[l], w_pool[l], pool_scale[l], rel_bias, sinks[l], w_out[l], g_post_mix[l],
                   g_pre_ffn[l], w_gate[l], w_up[l], w_down[l], g_post_ffn[l])
    return x
```

```python
import functools

import numpy as np
import jax
import jax.numpy as jnp
from jax import lax
from jax.experimental import pallas as pl
from jax.experimental.pallas import tpu as pltpu

D_MODEL = 1024
POOL_WIDTH = 512
POOL_WINDOWS = (2, 4, 8, 16)
N_POOL_GROUPS = len(POOL_WINDOWS)
POOL_GROUP_DIM = POOL_WIDTH // N_POOL_GROUPS
ATTN_WIDTH = 512
HEAD_DIM = 64
N_Q_HEADS = 8
N_KV_HEADS = 2
GQA_GROUP = N_Q_HEADS // N_KV_HEADS
WINDOW = 128
BLOCK = 128
N_BUCKETS = 32
MAX_EXACT = N_BUCKETS // 2
MAX_DISTANCE = 128
KV_WIDTH = N_KV_HEADS * HEAD_DIM
IN_WIDTH = POOL_WIDTH + ATTN_WIDTH + 2 * KV_WIDTH
D_FF = 2816
EPS = 1e-6
NEG_INF = -1e30

POOL_HALO = 16
TILE = 512
N_BLOCKS = TILE // BLOCK
FFN_CHUNK = 256
N_CHUNKS = D_FF // FFN_CHUNK
N_HALVES = 2
X_RING_SLOTS = 4
ATTEND_SLOTS = (0, 2, 4, 6)
POOL_SLOTS = (8, 10, 12, 14)
WEIGHT_STAGE_ROWS = 128
WEIGHT_STAGE_SLOTS = 4
WOUT_BLOCK_ROWS = HEAD_DIM
VMEM_LIMIT_BYTES = 60 * 1024 * 1024

_F32 = jnp.float32
_BF16 = jnp.bfloat16


def _dot(a, b):
    return jnp.dot(a, b, preferred_element_type=_F32)


def _rmsnorm(xf, g):
    ms = jnp.mean(xf * xf, axis=-1, keepdims=True)
    return xf * lax.rsqrt(ms + EPS) * g


def _bucket_table_t():
    qi = np.arange(BLOCK)[None, :]
    kj = np.arange(2 * BLOCK)[:, None]
    dist = qi + BLOCK - kj
    n = np.maximum(dist, 0)
    nf = np.maximum(n, 1).astype(np.float32)
    large = MAX_EXACT + (
        np.log(nf / np.float32(MAX_EXACT)) / np.float32(np.log(MAX_DISTANCE / MAX_EXACT)) * np.float32(N_BUCKETS - MAX_EXACT)
    ).astype(np.int32)
    large = np.minimum(large, N_BUCKETS - 1)
    bucket = np.where(n < MAX_EXACT, n, large)
    in_window = (dist >= 0) & (dist < WINDOW)
    return np.where(in_window, bucket, -1).astype(np.int32)


def _layer_kernel(tiles_per_seq, n_tiles,
                  x_hbm, win_hbm, wpool_f32_ref, pscale_ref, relb_t_ref, sinks_ref, bkt_ref, wout_hbm,
                  gpost_ref, wg_hbm, wu_hbm, wd_hbm, gpost2_ref, gin_row_ref, gffn_row_ref,
                  y_hbm, u_buf, q_stack, kcat, vt, mixcat, mix_buf, bias_t,
                  win_ref, wpool_ref, wout_ref, wg_ref, wu_ref, wd_ref, stage, stage_sem,
                  x_ring, x_sem, out_buf, out_sem, gain_cols):
    def tile_rows(tile):
        start = tile * TILE
        return pl.ds(start if isinstance(start, int) else pl.multiple_of(start, TILE), TILE)

    def x_copy(tile, slot):
        return pltpu.make_async_copy(x_hbm.at[tile_rows(tile), :], x_ring.at[slot], x_sem.at[slot])

    def y_copy(tile, slot):
        return pltpu.make_async_copy(out_buf.at[slot], y_hbm.at[tile_rows(tile), :], out_sem.at[slot])

    def stage_a_compute(x_ref):
        x = x_ref[...]
        inv_rms = lax.rsqrt(jnp.mean(x * x, axis=-1, keepdims=True) + EPS)
        xb = x.astype(_BF16)
        u = _dot(xb, win_ref[:, 0:POOL_WIDTH]) * inv_rms
        q = (_dot(xb, win_ref[:, POOL_WIDTH:POOL_WIDTH + ATTN_WIDTH]) * inv_rms).astype(_BF16)
        kv = _dot(xb, win_ref[:, POOL_WIDTH + ATTN_WIDTH:IN_WIDTH]) * inv_rms
        return u, q, kv

    def stage_a_store(u, q, kv, starts_seq):
        def history(tail):
            if starts_seq is None:
                return jnp.zeros(tail.shape, tail.dtype)
            return jnp.where(starts_seq, jnp.zeros(tail.shape, tail.dtype), tail[...])

        u_buf[0:POOL_HALO, :] = history(u_buf.at[TILE:TILE + POOL_HALO, :])
        u_buf[POOL_HALO:POOL_HALO + TILE, :] = u
        for j in range(N_BLOCKS):
            for g in range(GQA_GROUP):
                q_stack[j, g * BLOCK:(g + 1) * BLOCK, :] = q[j * BLOCK:(j + 1) * BLOCK, g * BLOCK:(g + 1) * BLOCK]
        k = kv[:, 0:KV_WIDTH]
        v = kv[:, KV_WIDTH:2 * KV_WIDTH]
        lane = lax.broadcasted_iota(jnp.int32, k.shape, 1)
        kcat[:, 0:BLOCK, :] = history(kcat.at[:, TILE:TILE + BLOCK, :])
        kcat[0, BLOCK:BLOCK + TILE, :] = jnp.where(lane < HEAD_DIM, k, 0.0).astype(_BF16)
        kcat[1, BLOCK:BLOCK + TILE, :] = jnp.where(lane >= HEAD_DIM, k, 0.0).astype(_BF16)
        v_t = v.T
        row = lax.broadcasted_iota(jnp.int32, v_t.shape, 0)
        vt[:, :, 0:BLOCK] = history(vt.at[:, :, TILE:TILE + BLOCK])
        vt[0, :, BLOCK:BLOCK + TILE] = jnp.where(row < HEAD_DIM, v_t, 0.0).astype(_BF16)
        vt[1, :, BLOCK:BLOCK + TILE] = jnp.where(row >= HEAD_DIM, v_t, 0.0).astype(_BF16)

    def prepare_weights():
        def stream(src, rows_per_chunk, n_chunks, width, src_row, consume, placements):
            per_iter = len(placements)
            n_iter = n_chunks // per_iter

            def copies(i, slot):
                return [pltpu.make_async_copy(src.at[pl.ds(src_row(i * per_iter + k), rows_per_chunk), :],
                                              stage.at[slot, r0:r0 + rows_per_chunk, l0:l0 + width],
                                              stage_sem.at[slot])
                        for k, (r0, l0) in enumerate(placements)]

            def start(i, slot):
                for c in copies(i, slot):
                    c.start()

            for i in range(min(WEIGHT_STAGE_SLOTS - 1, n_iter)):
                start(i, i)

            def body(i, carry):
                slot = lax.rem(i, WEIGHT_STAGE_SLOTS)
                ahead = i + WEIGHT_STAGE_SLOTS - 1

                @pl.when(ahead < n_iter)
                def _prefetch():
                    start(ahead, lax.rem(ahead, WEIGHT_STAGE_SLOTS))

                for c in copies(i, slot):
                    c.wait()
                for k, (r0, l0) in enumerate(placements):
                    consume(i * per_iter + k, stage[slot, r0:r0 + rows_per_chunk, l0:l0 + width])
                return carry

            lax.fori_loop(0, n_iter, body, 0)

        rows = WEIGHT_STAGE_ROWS

        def chunk_rows(i):
            return pl.ds(pl.multiple_of(i * rows, rows), rows)

        for n, gain_row_ref in enumerate((gin_row_ref, gffn_row_ref)):
            gain_cols[n] = jnp.broadcast_to(gain_row_ref[...], (BLOCK, D_MODEL)).T

        def gain_col(n, i):
            return gain_cols[n, chunk_rows(i), 0:1]

        def consume_in(i, blk):
            qb = blk[:, POOL_WIDTH:POOL_WIDTH + ATTN_WIDTH]
            heads = [qb[:, (h * GQA_GROUP + g) * HEAD_DIM:(h * GQA_GROUP + g + 1) * HEAD_DIM]
                     for g in range(GQA_GROUP) for h in range(N_KV_HEADS)]
            q_regrouped = jnp.concatenate(heads, axis=1) * (HEAD_DIM ** -0.5)
            full = jnp.concatenate([blk[:, :POOL_WIDTH], q_regrouped, blk[:, POOL_WIDTH + ATTN_WIDTH:]], axis=1)
            win_ref[chunk_rows(i), :] = (gain_col(0, i) * full).astype(_BF16)

        def consume_gated(dst):
            def consume(i, blk):
                dst[chunk_rows(i), :] = (gain_col(1, i) * blk).astype(_BF16)
            return consume

        def consume_down(i, blk):
            wd_ref[chunk_rows(i), :] = blk.astype(_BF16)

        def wout_src_row(i):
            j = i - POOL_WIDTH // WOUT_BLOCK_ROWS
            head = (j % N_KV_HEADS) * GQA_GROUP + j // N_KV_HEADS
            return jnp.where(j < 0, i, POOL_WIDTH // WOUT_BLOCK_ROWS + head) * WOUT_BLOCK_ROWS

        def consume_out(i, blk):
            dst = pl.ds(pl.multiple_of(i * WOUT_BLOCK_ROWS, WOUT_BLOCK_ROWS), WOUT_BLOCK_ROWS)
            wout_ref[dst, :] = blk.astype(_BF16)

        stream(win_hbm, rows, D_MODEL // rows, IN_WIDTH, lambda c: c * rows, consume_in, [(0, 0), (0, IN_WIDTH)])
        stream(wout_hbm, WOUT_BLOCK_ROWS, D_MODEL // WOUT_BLOCK_ROWS, D_MODEL, wout_src_row, consume_out,
               [(0, 0), (WOUT_BLOCK_ROWS, 0), (0, D_MODEL), (WOUT_BLOCK_ROWS, D_MODEL)])
        stream(wg_hbm, rows, D_MODEL // rows, D_FF, lambda c: c * rows, consume_gated(wg_ref), [(0, 0)])
        stream(wu_hbm, rows, D_MODEL // rows, D_FF, lambda c: c * rows, consume_gated(wu_ref), [(0, 0)])
        stream(wd_hbm, rows, D_FF // rows, D_MODEL, lambda c: c * rows, consume_down, [(0, 0), (0, D_MODEL)])
        wpool_ref[...] = wpool_f32_ref[...].astype(_BF16)

    def first_step():
        x_copy(0, 0).start()
        x_copy(1, 1).start()
        prepare_weights()
        mix_buf[...] = jnp.zeros(mix_buf.shape, _F32)
        x_copy(0, 0).wait()
        stage_a_store(*stage_a_compute(x_ring.at[0]), None)
        bkt = bkt_ref[...]
        key_is_prev = lax.broadcasted_iota(jnp.int32, bkt.shape, 0) < BLOCK
        for h in range(N_KV_HEADS):
            for g in range(GQA_GROUP):
                head = h * GQA_GROUP + g

                def body(bb, tab):
                    return jnp.where(bkt == bb, relb_t_ref[head, bb], tab)

                tab = lax.fori_loop(0, N_BUCKETS, body, jnp.zeros(bkt.shape, _F32))
                tab = jnp.where(bkt < 0, NEG_INF, tab)
                rows = slice(h * 2 * BLOCK, (h + 1) * 2 * BLOCK)
                cols = slice(g * BLOCK, (g + 1) * BLOCK)
                bias_t[0, rows, cols] = tab
                bias_t[1, rows, cols] = jnp.where(key_is_prev, NEG_INF, tab)

    def step_body(step, carry):
        seq_tile = lax.rem(jnp.minimum(step, n_tiles - 1), tiles_per_seq)
        next_starts_seq = lax.rem(jnp.minimum(step + 1, n_tiles - 1), tiles_per_seq) == 0
        out_slot = lax.rem(step, 2)

        @pl.when(step + 2 < n_tiles)
        def _prefetch_x():
            x_copy(step + 2, lax.rem(step + 2, X_RING_SLOTS)).start()

        @pl.when(step + 1 < n_tiles)
        def _wait_x():
            x_copy(step + 1, lax.rem(step + 1, X_RING_SLOTS)).wait()

        @pl.when(step >= 3)
        def _wait_y():
            y_copy(step - 3, out_slot).wait()

        x_next = x_ring.at[lax.rem(step + 1, X_RING_SLOTS)]
        x_prev = x_ring.at[lax.rem(jnp.maximum(step - 1, 0), X_RING_SLOTS)]
        y_tile = out_buf.at[out_slot]

        def ffn_gate_up(h2, inv_rms2, c):
            cols = slice(c * FFN_CHUNK, (c + 1) * FFN_CHUNK)
            gate = _dot(h2, wg_ref[:, cols]) * inv_rms2
            up = _dot(h2, wu_ref[:, cols]) * inv_rms2
            return (gate * (1.0 / (1.0 + jnp.exp(-gate))) * up).astype(_BF16)

        def ffn_down(c, act):
            return _dot(act, wd_ref[c * FFN_CHUNK:(c + 1) * FFN_CHUNK, :])

        pos = seq_tile * TILE + lax.broadcasted_iota(jnp.int32, (TILE, 1), 0)

        def pool_group(g):
            w = POOL_WINDOWS[g]
            cols = slice(g * POOL_GROUP_DIM, (g + 1) * POOL_GROUP_DIM)
            ext = u_buf[:, cols]
            acc = ext
            lag = 1
            while lag < w:
                acc = acc + jnp.concatenate([acc[:lag], acc[:-lag]], axis=0)
                lag *= 2
            count = jnp.minimum(pos + 1, w).astype(_F32)
            pooled = acc[POOL_HALO:] / count - ext[POOL_HALO:]
            mixed = _dot(pooled.astype(_BF16), wpool_ref[g]) * pscale_ref[:, cols]
            mixcat[:, cols] = mixed.astype(_BF16)

        col = lax.broadcasted_iota(jnp.int32, (1, GQA_GROUP * BLOCK), 1)
        sink_rows = []
        for h in range(N_KV_HEADS):
            row = jnp.full((1, GQA_GROUP * BLOCK), sinks_ref[h * GQA_GROUP], _F32)
            for g in range(1, GQA_GROUP):
                row = jnp.where(col >= g * BLOCK, sinks_ref[h * GQA_GROUP + g], row)
            sink_rows.append(row)
        first_variant = jnp.where(seq_tile == 0, 1, 0)

        def logits(j):
            band = slice(j * BLOCK, (j + 2) * BLOCK)
            kb = jnp.concatenate([kcat[0, band, :], kcat[1, band, :]], axis=0)
            s = lax.dot_general(kb, q_stack[j], (((1,), (1,)), ((), ())), preferred_element_type=_F32)
            return s + (bias_t[first_variant] if j == 0 else bias_t[0])

        def attend(j, s):
            band = slice(j * BLOCK, (j + 2) * BLOCK)
            probs, inv_denoms = [], []
            for h in range(N_KV_HEADS):
                sh = s[h * 2 * BLOCK:(h + 1) * 2 * BLOCK, :]
                m = jnp.maximum(jnp.max(sh, axis=0, keepdims=True), sink_rows[h])
                p = jnp.exp(sh - m)
                denom = jnp.sum(p, axis=0, keepdims=True) + jnp.exp(sink_rows[h] - m)
                probs.append(p.astype(_BF16))
                inv_denoms.append(1.0 / denom)
            v_both = jnp.concatenate([vt[h, :, band] for h in range(N_KV_HEADS)], axis=1)
            o_t = _dot(v_both, jnp.concatenate(probs, axis=0))
            o_t = jnp.concatenate([o_t[h * HEAD_DIM:(h + 1) * HEAD_DIM] * inv_denoms[h] for h in range(N_KV_HEADS)], axis=0)
            blks = []
            for g in range(GQA_GROUP):
                blks.append(o_t[:, g * BLOCK:(g + 1) * BLOCK].T.astype(_BF16))
            return jnp.concatenate(blks, axis=1)

        half = TILE // N_HALVES
        x1_halves, h2_halves, inv_halves = [], [], []
        for r in range(N_HALVES):
            rows = slice(r * half, (r + 1) * half)
            x1_r = x_prev[rows, :] + _rmsnorm(mix_buf[rows, :], gpost_ref[...])
            x1_halves.append(x1_r)
            inv_halves.append(lax.rsqrt(jnp.mean(x1_r * x1_r, axis=-1, keepdims=True) + EPS))
            h2_halves.append(x1_r.astype(_BF16))
        next_proj = stage_a_compute(x_next)
        s_blocks = [logits(j) for j in range(N_BLOCKS)]
        work = [(r, c) for r in range(N_HALVES) for c in range(N_CHUNKS)]

        def attend_and_store(j):
            mixcat[j * BLOCK:(j + 1) * BLOCK, POOL_WIDTH:] = attend(j, s_blocks[j])

        extras = {}
        for j in range(N_BLOCKS):
            extras[ATTEND_SLOTS[j]] = functools.partial(attend_and_store, j)
        for g in range(N_POOL_GROUPS):
            extras[POOL_SLOTS[g]] = functools.partial(pool_group, g)

        def gate_up(k):
            r, c = work[k]
            return ffn_gate_up(h2_halves[r], inv_halves[r], c)

        act_next = gate_up(0)
        acc = None
        for k, (r, c) in enumerate(work):
            act = act_next
            if k + 1 < len(work):
                act_next = gate_up(k + 1)
            part = ffn_down(c, act)
            acc = part if acc is None else acc + part
            if c == N_CHUNKS - 1:
                y_tile[r * half:(r + 1) * half, :] = x1_halves[r] + _rmsnorm(acc, gpost2_ref[...])
                acc = None
            if k in extras:
                extras[k]()
        mix_buf[...] = _dot(mixcat[...], wout_ref[...])
        stage_a_store(*next_proj, next_starts_seq)

        @pl.when(step >= 1)
        def _store_y():
            y_copy(step - 1, out_slot).start()

        return carry

    first_step()
    lax.fori_loop(0, n_tiles + 1, step_body, 0)
    y_copy(n_tiles - 2, (n_tiles - 1) % 2).wait()
    y_copy(n_tiles - 1, n_tiles % 2).wait()


def _layer(x, g_pre_mix, w_in, w_pool, pool_scale, rel_bias, sinks, w_out, g_post_mix, g_pre_ffn, w_gate, w_up, w_down,
           g_post_ffn):
    batch, seq, _ = x.shape
    tokens = batch * seq
    n_tiles = tokens // TILE
    tiles_per_seq = seq // TILE
    bucket_t = jnp.asarray(_bucket_table_t())
    x2 = x.reshape(tokens, D_MODEL)
    row = lambda a: a.reshape(1, -1)

    vmem = pl.BlockSpec(memory_space=pltpu.VMEM)
    smem = pl.BlockSpec(memory_space=pltpu.SMEM)
    hbm = pl.BlockSpec(memory_space=pl.ANY)
    out = pl.pallas_call(
        functools.partial(_layer_kernel, tiles_per_seq, n_tiles),
        out_shape=jax.ShapeDtypeStruct(x2.shape, x2.dtype),
        in_specs=[hbm, hbm, vmem, vmem, smem, smem, vmem, hbm, vmem, hbm, hbm, hbm, vmem, vmem, vmem],
        out_specs=hbm,
        scratch_shapes=[
            pltpu.VMEM((POOL_HALO + TILE, POOL_WIDTH), _F32),
            pltpu.VMEM((N_BLOCKS, GQA_GROUP * BLOCK, KV_WIDTH), _BF16),
            pltpu.VMEM((N_KV_HEADS, BLOCK + TILE, KV_WIDTH), _BF16),
            pltpu.VMEM((N_KV_HEADS, KV_WIDTH, BLOCK + TILE), _BF16),
            pltpu.VMEM((TILE, D_MODEL), _BF16),
            pltpu.VMEM((TILE, D_MODEL), _F32),
            pltpu.VMEM((2, N_KV_HEADS * 2 * BLOCK, GQA_GROUP * BLOCK), _F32),
            pltpu.VMEM((D_MODEL, IN_WIDTH), _BF16),
            pltpu.VMEM((N_POOL_GROUPS, POOL_GROUP_DIM, POOL_GROUP_DIM), _BF16),
            pltpu.VMEM((D_MODEL, D_MODEL), _BF16),
            pltpu.VMEM((D_MODEL, D_FF), _BF16),
            pltpu.VMEM((D_MODEL, D_FF), _BF16),
            pltpu.VMEM((D_FF, D_MODEL), _BF16),
            pltpu.VMEM((WEIGHT_STAGE_SLOTS, WEIGHT_STAGE_ROWS, D_FF), _F32),
            pltpu.SemaphoreType.DMA((WEIGHT_STAGE_SLOTS,)),
            pltpu.VMEM((X_RING_SLOTS, TILE, D_MODEL), _F32),
            pltpu.SemaphoreType.DMA((X_RING_SLOTS,)),
            pltpu.VMEM((2, TILE, D_MODEL), _F32),
            pltpu.SemaphoreType.DMA((2,)),
            pltpu.VMEM((2, D_MODEL, BLOCK), _F32),
        ],
        compiler_params=pltpu.CompilerParams(vmem_limit_bytes=VMEM_LIMIT_BYTES),
        name="hybrid_layer",
    )(x2, w_in, w_pool, row(pool_scale), rel_bias.T, sinks, bucket_t, w_out,
      row(g_post_mix), w_gate, w_up, w_down, row(g_post_ffn), row(g_pre_mix), row(g_pre_ffn))
    return out.reshape(x.shape)


def kernel(x, g_pre_mix, w_in, w_pool, pool_scale, rel_bias, sinks, w_out, g_post_mix, g_pre_ffn, w_gate, w_up, w_down, g_post_ffn):
    depth = g_pre_mix.shape[0]
    for l in range(depth):
        x = _layer(x, g_pre_mix[l], w_in[l], w_pool[l], pool_scale[l], rel_bias, sinks[l], w_out[l], g_post_mix[l],
                   g_pre_ffn[l], w_gate[l], w_up[l], w_down[l], g_post_ffn[l])
    return x
```

```python
import functools

import numpy as np
import jax
import jax.numpy as jnp
from jax import lax
from jax.experimental import pallas as pl
from jax.experimental.pallas import tpu as pltpu

D_MODEL = 1024
POOL_WIDTH = 512
POOL_WINDOWS = (2, 4, 8, 16)
N_POOL_GROUPS = len(POOL_WINDOWS)
POOL_GROUP_DIM = POOL_WIDTH // N_POOL_GROUPS
ATTN_WIDTH = 512
HEAD_DIM = 64
N_Q_HEADS = 8
N_KV_HEADS = 2
GQA_GROUP = N_Q_HEADS // N_KV_HEADS
WINDOW = 128
BLOCK = 128
N_BUCKETS = 32
MAX_EXACT = N_BUCKETS // 2
MAX_DISTANCE = 128
KV_WIDTH = N_KV_HEADS * HEAD_DIM
IN_WIDTH = POOL_WIDTH + ATTN_WIDTH + 2 * KV_WIDTH
D_FF = 2816
EPS = 1e-6
NEG_INF = -1e30

POOL_HALO = 16
TILE = 512
N_BLOCKS = TILE // BLOCK
FFN_CHUNK = 256
N_CHUNKS = D_FF // FFN_CHUNK
N_HALVES = 2
X_RING_SLOTS = 4
ATTEND_SLOTS = (0, 2, 4, 6)
POOL_SLOTS = (8, 10, 12, 14)
WEIGHT_STAGE_ROWS = 128
WEIGHT_STAGE_SLOTS = 4
WOUT_BLOCK_ROWS = HEAD_DIM
VMEM_LIMIT_BYTES = 63 * 1024 * 1024

_F32 = jnp.float32
_BF16 = jnp.bfloat16


def _dot(a, b):
    return jnp.dot(a, b, preferred_element_type=_F32)


def _rmsnorm(xf, g):
    ms = jnp.mean(xf * xf, axis=-1, keepdims=True)
    return xf * lax.rsqrt(ms + EPS) * g


def _bucket_table_t():
    qi = np.arange(BLOCK)[None, :]
    kj = np.arange(2 * BLOCK)[:, None]
    dist = qi + BLOCK - kj
    n = np.maximum(dist, 0)
    nf = np.maximum(n, 1).astype(np.float32)
    large = MAX_EXACT + (
        np.log(nf / np.float32(MAX_EXACT)) / np.float32(np.log(MAX_DISTANCE / MAX_EXACT)) * np.float32(N_BUCKETS - MAX_EXACT)
    ).astype(np.int32)
    large = np.minimum(large, N_BUCKETS - 1)
    bucket = np.where(n < MAX_EXACT, n, large)
    in_window = (dist >= 0) & (dist < WINDOW)
    return np.where(in_window, bucket, -1).astype(np.int32)


def _layer_kernel(tiles_per_seq, n_tiles,
                  x_hbm, win_hbm, wpool_f32_ref, pscale_ref, relb_t_ref, sinks_ref, bkt_ref, wout_hbm,
                  gpost_ref, wg_hbm, wu_hbm, wd_hbm, gpost2_ref, gin_row_ref, gffn_row_ref,
                  y_hbm, u_buf, q_stack, kcat, vt, mixcat, mix_buf, bias_t,
                  win_ref, wpool_ref, wout_ref, wg_ref, wu_ref, wd_ref, stage, stage_sem,
                  x_ring, x_sem, out_buf, out_sem, gain_cols):
    def tile_rows(tile):
        start = tile * TILE
        return pl.ds(start if isinstance(start, int) else pl.multiple_of(start, TILE), TILE)

    def x_copy(tile, slot):
        return pltpu.make_async_copy(x_hbm.at[tile_rows(tile), :], x_ring.at[slot], x_sem.at[slot])

    def y_copy(tile, slot):
        return pltpu.make_async_copy(out_buf.at[slot], y_hbm.at[tile_rows(tile), :], out_sem.at[slot])

    def stage_a_compute(x_ref):
        x = x_ref[...]
        inv_rms = lax.rsqrt(jnp.mean(x * x, axis=-1, keepdims=True) + EPS)
        xb = x.astype(_BF16)
        u = _dot(xb, win_ref[:, 0:POOL_WIDTH]) * inv_rms
        q = (_dot(xb, win_ref[:, POOL_WIDTH:POOL_WIDTH + ATTN_WIDTH]) * inv_rms).astype(_BF16)
        kv = _dot(xb, win_ref[:, POOL_WIDTH + ATTN_WIDTH:IN_WIDTH]) * inv_rms
        return u, q, kv

    def stage_a_store(u, q, kv, starts_seq):
        def history(tail):
            if starts_seq is None:
                return jnp.zeros(tail.shape, tail.dtype)
            return jnp.where(starts_seq, jnp.zeros(tail.shape, tail.dtype), tail[...])

        u_buf[0:POOL_HALO, :] = history(u_buf.at[TILE:TILE + POOL_HALO, :])
        u_buf[POOL_HALO:POOL_HALO + TILE, :] = u
        for j in range(N_BLOCKS):
            for g in range(GQA_GROUP):
                q_stack[j, g * BLOCK:(g + 1) * BLOCK, :] = q[j * BLOCK:(j + 1) * BLOCK, g * BLOCK:(g + 1) * BLOCK]
        k = kv[:, 0:KV_WIDTH]
        v = kv[:, KV_WIDTH:2 * KV_WIDTH]
        lane = lax.broadcasted_iota(jnp.int32, k.shape, 1)
        kcat[:, 0:BLOCK, :] = history(kcat.at[:, TILE:TILE + BLOCK, :])
        kcat[0, BLOCK:BLOCK + TILE, :] = jnp.where(lane < HEAD_DIM, k, 0.0).astype(_BF16)
        kcat[1, BLOCK:BLOCK + TILE, :] = jnp.where(lane >= HEAD_DIM, k, 0.0).astype(_BF16)
        v_t = v.T
        row = lax.broadcasted_iota(jnp.int32, v_t.shape, 0)
        vt[:, :, 0:BLOCK] = history(vt.at[:, :, TILE:TILE + BLOCK])
        vt[0, :, BLOCK:BLOCK + TILE] = jnp.where(row < HEAD_DIM, v_t, 0.0).astype(_BF16)
        vt[1, :, BLOCK:BLOCK + TILE] = jnp.where(row >= HEAD_DIM, v_t, 0.0).astype(_BF16)

    def prepare_weights():
        def stream(src, rows_per_chunk, n_chunks, width, src_row, consume, placements):
            per_iter = len(placements)
            n_iter = n_chunks // per_iter

            def copies(i, slot):
                return [pltpu.make_async_copy(src.at[pl.ds(src_row(i * per_iter + k), rows_per_chunk), :],
                                              stage.at[slot, r0:r0 + rows_per_chunk, l0:l0 + width],
                                              stage_sem.at[slot])
                        for k, (r0, l0) in enumerate(placements)]

            def start(i, slot):
                for c in copies(i, slot):
                    c.start()

            for i in range(min(WEIGHT_STAGE_SLOTS - 1, n_iter)):
                start(i, i)

            def body(i, carry):
                slot = lax.rem(i, WEIGHT_STAGE_SLOTS)
                ahead = i + WEIGHT_STAGE_SLOTS - 1

                @pl.when(ahead < n_iter)
                def _prefetch():
                    start(ahead, lax.rem(ahead, WEIGHT_STAGE_SLOTS))

                for c in copies(i, slot):
                    c.wait()
                for k, (r0, l0) in enumerate(placements):
                    consume(i * per_iter + k, stage[slot, r0:r0 + rows_per_chunk, l0:l0 + width])
                return carry

            lax.fori_loop(0, n_iter, body, 0)

        rows = WEIGHT_STAGE_ROWS

        def chunk_rows(i):
            return pl.ds(pl.multiple_of(i * rows, rows), rows)

        for n, gain_row_ref in enumerate((gin_row_ref, gffn_row_ref)):
            gain_cols[n] = jnp.broadcast_to(gain_row_ref[...], (BLOCK, D_MODEL)).T

        def gain_col(n, i):
            return gain_cols[n, chunk_rows(i), 0:1]

        def consume_in(i, blk):
            qb = blk[:, POOL_WIDTH:POOL_WIDTH + ATTN_WIDTH]
            heads = [qb[:, (h * GQA_GROUP + g) * HEAD_DIM:(h * GQA_GROUP + g + 1) * HEAD_DIM]
                     for g in range(GQA_GROUP) for h in range(N_KV_HEADS)]
            q_regrouped = jnp.concatenate(heads, axis=1) * (HEAD_DIM ** -0.5)
            full = jnp.concatenate([blk[:, :POOL_WIDTH], q_regrouped, blk[:, POOL_WIDTH + ATTN_WIDTH:]], axis=1)
            win_ref[chunk_rows(i), :] = (gain_col(0, i) * full).astype(_BF16)

        def consume_gated(dst):
            def consume(i, blk):
                dst[chunk_rows(i), :] = (gain_col(1, i) * blk).astype(_BF16)
            return consume

        def consume_down(i, blk):
            wd_ref[chunk_rows(i), :] = blk.astype(_BF16)

        def wout_src_row(i):
            j = i - POOL_WIDTH // WOUT_BLOCK_ROWS
            head = (j % N_KV_HEADS) * GQA_GROUP + j // N_KV_HEADS
            return jnp.where(j < 0, i, POOL_WIDTH // WOUT_BLOCK_ROWS + head) * WOUT_BLOCK_ROWS

        def consume_out(i, blk):
            dst = pl.ds(pl.multiple_of(i * WOUT_BLOCK_ROWS, WOUT_BLOCK_ROWS), WOUT_BLOCK_ROWS)
            wout_ref[dst, :] = blk.astype(_BF16)

        stream(win_hbm, rows, D_MODEL // rows, IN_WIDTH, lambda c: c * rows, consume_in, [(0, 0), (0, IN_WIDTH)])
        stream(wout_hbm, WOUT_BLOCK_ROWS, D_MODEL // WOUT_BLOCK_ROWS, D_MODEL, wout_src_row, consume_out,
               [(0, 0), (WOUT_BLOCK_ROWS, 0), (0, D_MODEL), (WOUT_BLOCK_ROWS, D_MODEL)])
        stream(wg_hbm, rows, D_MODEL // rows, D_FF, lambda c: c * rows, consume_gated(wg_ref), [(0, 0)])
        stream(wu_hbm, rows, D_MODEL // rows, D_FF, lambda c: c * rows, consume_gated(wu_ref), [(0, 0)])
        stream(wd_hbm, rows, D_FF // rows, D_MODEL, lambda c: c * rows, consume_down, [(0, 0), (0, D_MODEL)])
        wpool_ref[...] = wpool_f32_ref[...].astype(_BF16)

    def first_step():
        x_copy(0, 0).start()
        x_copy(1, 1).start()
        prepare_weights()
        x_copy(0, 0).wait()
        stage_a_store(*stage_a_compute(x_ring.at[0]), None)
        bkt = bkt_ref[...]
        key_is_prev = lax.broadcasted_iota(jnp.int32, bkt.shape, 0) < BLOCK
        for h in range(N_KV_HEADS):
            for g in range(GQA_GROUP):
                head = h * GQA_GROUP + g

                def body(bb, tab):
                    return jnp.where(bkt == bb, relb_t_ref[head, bb], tab)

                tab = lax.fori_loop(0, N_BUCKETS, body, jnp.zeros(bkt.shape, _F32))
                tab = jnp.where(bkt < 0, NEG_INF, tab)
                rows = slice(h * 2 * BLOCK, (h + 1) * 2 * BLOCK)
                cols = slice(g * BLOCK, (g + 1) * BLOCK)
                bias_t[0, rows, cols] = tab
                bias_t[1, rows, cols] = jnp.where(key_is_prev, NEG_INF, tab)

    def step_body(step, carry, with_ab=True, with_c=True):
        seq_tile = lax.rem(jnp.minimum(step, n_tiles - 1), tiles_per_seq)
        next_starts_seq = lax.rem(jnp.minimum(step + 1, n_tiles - 1), tiles_per_seq) == 0
        out_slot = lax.rem(step, 2)

        @pl.when(step + 2 < n_tiles)
        def _prefetch_x():
            x_copy(step + 2, lax.rem(step + 2, X_RING_SLOTS)).start()

        @pl.when(step + 1 < n_tiles)
        def _wait_x():
            x_copy(step + 1, lax.rem(step + 1, X_RING_SLOTS)).wait()

        @pl.when(step >= 3)
        def _wait_y():
            y_copy(step - 3, out_slot).wait()

        x_next = x_ring.at[lax.rem(step + 1, X_RING_SLOTS)]
        x_prev = x_ring.at[lax.rem(jnp.maximum(step - 1, 0), X_RING_SLOTS)]
        y_tile = out_buf.at[out_slot]

        def ffn_gate_up(h2, inv_rms2, c):
            cols = slice(c * FFN_CHUNK, (c + 1) * FFN_CHUNK)
            gate = _dot(h2, wg_ref[:, cols]) * inv_rms2
            up = _dot(h2, wu_ref[:, cols]) * inv_rms2
            return (gate * (1.0 / (1.0 + jnp.exp(-gate))) * up).astype(_BF16)

        def ffn_down(c, act):
            return _dot(act, wd_ref[c * FFN_CHUNK:(c + 1) * FFN_CHUNK, :])

        pos = seq_tile * TILE + lax.broadcasted_iota(jnp.int32, (TILE, 1), 0)

        def pool_group(g):
            w = POOL_WINDOWS[g]
            cols = slice(g * POOL_GROUP_DIM, (g + 1) * POOL_GROUP_DIM)
            ext = u_buf[:, cols]
            acc = ext
            lag = 1
            while lag < w:
                acc = acc + jnp.concatenate([acc[:lag], acc[:-lag]], axis=0)
                lag *= 2
            count = jnp.minimum(pos + 1, w).astype(_F32)
            pooled = acc[POOL_HALO:] / count - ext[POOL_HALO:]
            mixed = _dot(pooled.astype(_BF16), wpool_ref[g]) * pscale_ref[:, cols]
            mixcat[:, cols] = mixed.astype(_BF16)

        col = lax.broadcasted_iota(jnp.int32, (1, GQA_GROUP * BLOCK), 1)
        sink_rows = []
        for h in range(N_KV_HEADS):
            row = jnp.full((1, GQA_GROUP * BLOCK), sinks_ref[h * GQA_GROUP], _F32)
            for g in range(1, GQA_GROUP):
                row = jnp.where(col >= g * BLOCK, sinks_ref[h * GQA_GROUP + g], row)
            sink_rows.append(row)
        first_variant = jnp.where(seq_tile == 0, 1, 0)

        def logits(j):
            band = slice(j * BLOCK, (j + 2) * BLOCK)
            kb = jnp.concatenate([kcat[0, band, :], kcat[1, band, :]], axis=0)
            s = lax.dot_general(kb, q_stack[j], (((1,), (1,)), ((), ())), preferred_element_type=_F32)
            return s + (bias_t[first_variant] if j == 0 else bias_t[0])

        def attend(j, s):
            band = slice(j * BLOCK, (j + 2) * BLOCK)
            probs, inv_denoms = [], []
            for h in range(N_KV_HEADS):
                sh = s[h * 2 * BLOCK:(h + 1) * 2 * BLOCK, :]
                m = jnp.maximum(jnp.max(sh, axis=0, keepdims=True), sink_rows[h])
                p = jnp.exp(sh - m)
                denom = jnp.sum(p, axis=0, keepdims=True) + jnp.exp(sink_rows[h] - m)
                probs.append(p.astype(_BF16))
                inv_denoms.append(1.0 / denom)
            v_both = jnp.concatenate([vt[h, :, band] for h in range(N_KV_HEADS)], axis=1)
            o_t = _dot(v_both, jnp.concatenate(probs, axis=0))
            o_t = jnp.concatenate([o_t[h * HEAD_DIM:(h + 1) * HEAD_DIM] * inv_denoms[h] for h in range(N_KV_HEADS)], axis=0)
            blks = []
            for g in range(GQA_GROUP):
                blks.append(o_t[:, g * BLOCK:(g + 1) * BLOCK].T.astype(_BF16))
            return jnp.concatenate(blks, axis=1)

        half = TILE // N_HALVES
        x1_halves, h2_halves, inv_halves = [], [], []
        for r in range(N_HALVES if with_c else 0):
            rows = slice(r * half, (r + 1) * half)
            x1_r = x_prev[rows, :] + _rmsnorm(mix_buf[rows, :], gpost_ref[...])
            x1_halves.append(x1_r)
            inv_halves.append(lax.rsqrt(jnp.mean(x1_r * x1_r, axis=-1, keepdims=True) + EPS))
            h2_halves.append(x1_r.astype(_BF16))
        next_proj = stage_a_compute(x_next) if with_ab else None
        s_blocks = [logits(j) for j in range(N_BLOCKS)] if with_ab else []
        work = [(r, c) for r in range(N_HALVES) for c in range(N_CHUNKS)]

        def attend_and_store(j):
            mixcat[j * BLOCK:(j + 1) * BLOCK, POOL_WIDTH:] = attend(j, s_blocks[j])

        extras = {}
        if with_ab:
            for j in range(N_BLOCKS):
                extras[ATTEND_SLOTS[j]] = functools.partial(attend_and_store, j)
            for g in range(N_POOL_GROUPS):
                extras[POOL_SLOTS[g]] = functools.partial(pool_group, g)

        def gate_up(k):
            r, c = work[k]
            return ffn_gate_up(h2_halves[r], inv_halves[r], c)

        if not with_c:
            work = []
            for k in sorted(extras):
                extras[k]()
        act_next = gate_up(0) if with_c else None
        acc = None
        for k, (r, c) in enumerate(work):
            act = act_next
            if k + 1 < len(work):
                act_next = gate_up(k + 1)
            part = ffn_down(c, act)
            acc = part if acc is None else acc + part
            if c == N_CHUNKS - 1:
                y_tile[r * half:(r + 1) * half, :] = x1_halves[r] + _rmsnorm(acc, gpost2_ref[...])
                acc = None
            if k in extras:
                extras[k]()
        if with_ab:
            mix_buf[...] = _dot(mixcat[...], wout_ref[...])
            stage_a_store(*next_proj, next_starts_seq)

        @pl.when(step >= 1)
        def _store_y():
            y_copy(step - 1, out_slot).start()

        return carry

    first_step()
    step_body(0, 0, with_c=False)
    lax.fori_loop(1, n_tiles, step_body, 0)
    step_body(n_tiles, 0, with_ab=False)
    y_copy(n_tiles - 2, (n_tiles - 1) % 2).wait()
    y_copy(n_tiles - 1, n_tiles % 2).wait()


def _layer(x, g_pre_mix, w_in, w_pool, pool_scale, rel_bias, sinks, w_out, g_post_mix, g_pre_ffn, w_gate, w_up, w_down,
           g_post_ffn):
    batch, seq, _ = x.shape
    tokens = batch * seq
    n_tiles = tokens // TILE
    tiles_per_seq = seq // TILE
    bucket_t = jnp.asarray(_bucket_table_t())
    x2 = x.reshape(tokens, D_MODEL)
    row = lambda a: a.reshape(1, -1)

    vmem = pl.BlockSpec(memory_space=pltpu.VMEM)
    smem = pl.BlockSpec(memory_space=pltpu.SMEM)
    hbm = pl.BlockSpec(memory_space=pl.ANY)
    out = pl.pallas_call(
        functools.partial(_layer_kernel, tiles_per_seq, n_tiles),
        out_shape=jax.ShapeDtypeStruct(x2.shape, x2.dtype),
        in_specs=[hbm, hbm, vmem, vmem, smem, smem, vmem, hbm, vmem, hbm, hbm, hbm, vmem, vmem, vmem],
        out_specs=hbm,
        scratch_shapes=[
            pltpu.VMEM((POOL_HALO + TILE, POOL_WIDTH), _F32),
            pltpu.VMEM((N_BLOCKS, GQA_GROUP * BLOCK, KV_WIDTH), _BF16),
            pltpu.VMEM((N_KV_HEADS, BLOCK + TILE, KV_WIDTH), _BF16),
            pltpu.VMEM((N_KV_HEADS, KV_WIDTH, BLOCK + TILE), _BF16),
            pltpu.VMEM((TILE, D_MODEL), _BF16),
            pltpu.VMEM((TILE, D_MODEL), _F32),
            pltpu.VMEM((2, N_KV_HEADS * 2 * BLOCK, GQA_GROUP * BLOCK), _F32),
            pltpu.VMEM((D_MODEL, IN_WIDTH), _BF16),
            pltpu.VMEM((N_POOL_GROUPS, POOL_GROUP_DIM, POOL_GROUP_DIM), _BF16),
            pltpu.VMEM((D_MODEL, D_MODEL), _BF16),
            pltpu.VMEM((D_MODEL, D_FF), _BF16),
            pltpu.VMEM((D_MODEL, D_FF), _BF16),
            pltpu.VMEM((D_FF, D_MODEL), _BF16),
            pltpu.VMEM((WEIGHT_STAGE_SLOTS, WEIGHT_STAGE_ROWS, D_FF), _F32),
            pltpu.SemaphoreType.DMA((WEIGHT_STAGE_SLOTS,)),
            pltpu.VMEM((X_RING_SLOTS, TILE, D_MODEL), _F32),
            pltpu.SemaphoreType.DMA((X_RING_SLOTS,)),
            pltpu.VMEM((2, TILE, D_MODEL), _F32),
            pltpu.SemaphoreType.DMA((2,)),
            pltpu.VMEM((2, D_MODEL, BLOCK), _F32),
        ],
        compiler_params=pltpu.CompilerParams(vmem_limit_bytes=VMEM_LIMIT_BYTES),
        name="hybrid_layer",
    )(x2, w_in, w_pool, row(pool_scale), rel_bias.T, sinks, bucket_t, w_out,
      row(g_post_mix), w_gate, w_up, w_down, row(g_post_ffn), row(g_pre_mix), row(g_pre_ffn))
    return out.reshape(x.shape)


def kernel(x, g_pre_mix, w_in, w_pool, pool_scale, rel_bias, sinks, w_out, g_post_mix, g_pre_ffn, w_gate, w_up, w_down, g_post_ffn):
    depth = g_pre_mix.shape[0]
    for l in range(depth):
        x = _layer(x, g_pre_mix[l], w_in[l], w_pool[l], pool_scale[l], rel_bias, sinks[l], w_out[l], g_post_mix[l],
                   g_pre_ffn[l], w_gate[l], w_up[l], w_down[l], g_post_ffn[l])
    return x
```

```python
import functools

import numpy as np
import jax
import jax.numpy as jnp
from jax import lax
from jax.experimental import pallas as pl
from jax.experimental.pallas import tpu as pltpu

D_MODEL = 1024
POOL_WIDTH = 512
POOL_WINDOWS = (2, 4, 8, 16)
N_POOL_GROUPS = len(POOL_WINDOWS)
POOL_GROUP_DIM = POOL_WIDTH // N_POOL_GROUPS
ATTN_WIDTH = 512
HEAD_DIM = 64
N_Q_HEADS = 8
N_KV_HEADS = 2
GQA_GROUP = N_Q_HEADS // N_KV_HEADS
WINDOW = 128
BLOCK = 128
N_BUCKETS = 32
MAX_EXACT = N_BUCKETS // 2
MAX_DISTANCE = 128
KV_WIDTH = N_KV_HEADS * HEAD_DIM
IN_WIDTH = POOL_WIDTH + ATTN_WIDTH + 2 * KV_WIDTH
D_FF = 2816
EPS = 1e-6
NEG_INF = -1e30

POOL_HALO = 16
TILE = 512
N_BLOCKS = TILE // BLOCK
FFN_CHUNK = 256
N_CHUNKS = D_FF // FFN_CHUNK
N_HALVES = 2
X_RING_SLOTS = 4
ATTEND_SLOTS = (0, 2, 4, 6)
POOL_SLOTS = (8, 10, 12, 14)
WEIGHT_STAGE_ROWS = 128
WEIGHT_STAGE_SLOTS = 4
WOUT_BLOCK_ROWS = HEAD_DIM
VMEM_LIMIT_BYTES = 63 * 1024 * 1024

_F32 = jnp.float32
_BF16 = jnp.bfloat16


def _dot(a, b):
    return jnp.dot(a, b, preferred_element_type=_F32)


def _rmsnorm(xf, g):
    ms = jnp.mean(xf * xf, axis=-1, keepdims=True)
    return xf * lax.rsqrt(ms + EPS) * g


def _bucket_table_t():
    qi = np.arange(BLOCK)[None, :]
    kj = np.arange(2 * BLOCK)[:, None]
    dist = qi + BLOCK - kj
    n = np.maximum(dist, 0)
    nf = np.maximum(n, 1).astype(np.float32)
    large = MAX_EXACT + (
        np.log(nf / np.float32(MAX_EXACT)) / np.float32(np.log(MAX_DISTANCE / MAX_EXACT)) * np.float32(N_BUCKETS - MAX_EXACT)
    ).astype(np.int32)
    large = np.minimum(large, N_BUCKETS - 1)
    bucket = np.where(n < MAX_EXACT, n, large)
    in_window = (dist >= 0) & (dist < WINDOW)
    return np.where(in_window, bucket, -1).astype(np.int32)


def _layer_kernel(tiles_per_seq, n_tiles,
                  x_hbm, win_hbm, wpool_f32_ref, pscale_ref, relb_t_ref, sinks_ref, bkt_ref, wout_hbm,
                  gpost_ref, wg_hbm, wu_hbm, wd_hbm, gpost2_ref, gin_row_ref, gffn_row_ref,
                  y_hbm, u_buf, q_stack, kcat, vt, mixcat, mix_buf, bias_t,
                  win_ref, wpool_ref, wout_ref, wg_ref, wu_ref, wd_ref, stage, stage_sem,
                  x_ring, x_sem, out_buf, out_sem, gain_cols):
    def tile_rows(tile):
        start = tile * TILE
        return pl.ds(start if isinstance(start, int) else pl.multiple_of(start, TILE), TILE)

    def x_copy(tile, slot):
        return pltpu.make_async_copy(x_hbm.at[tile_rows(tile), :], x_ring.at[slot], x_sem.at[slot])

    def y_copy(tile, slot):
        return pltpu.make_async_copy(out_buf.at[slot], y_hbm.at[tile_rows(tile), :], out_sem.at[slot])

    def stage_a_compute(x_ref):
        x = x_ref[...]
        inv_rms = lax.rsqrt(jnp.mean(x * x, axis=-1, keepdims=True) + EPS)
        xb = x.astype(_BF16)
        u = _dot(xb, win_ref[:, 0:POOL_WIDTH]) * inv_rms
        q = (_dot(xb, win_ref[:, POOL_WIDTH:POOL_WIDTH + ATTN_WIDTH]) * inv_rms).astype(_BF16)
        kv = _dot(xb, win_ref[:, POOL_WIDTH + ATTN_WIDTH:IN_WIDTH]) * inv_rms
        return u, q, kv

    def stage_a_store(u, q, kv, starts_seq):
        def history(tail):
            if starts_seq is None:
                return jnp.zeros(tail.shape, tail.dtype)
            return jnp.where(starts_seq, jnp.zeros(tail.shape, tail.dtype), tail[...])

        u_buf[0:POOL_HALO, :] = history(u_buf.at[TILE:TILE + POOL_HALO, :])
        u_buf[POOL_HALO:POOL_HALO + TILE, :] = u
        for j in range(N_BLOCKS):
            for g in range(GQA_GROUP):
                q_stack[j, g * BLOCK:(g + 1) * BLOCK, :] = q[j * BLOCK:(j + 1) * BLOCK, g * BLOCK:(g + 1) * BLOCK]
        k = kv[:, 0:KV_WIDTH]
        v = kv[:, KV_WIDTH:2 * KV_WIDTH]
        lane = lax.broadcasted_iota(jnp.int32, k.shape, 1)
        kcat[:, 0:BLOCK, :] = history(kcat.at[:, TILE:TILE + BLOCK, :])
        kcat[0, BLOCK:BLOCK + TILE, :] = jnp.where(lane < HEAD_DIM, k, 0.0).astype(_BF16)
        kcat[1, BLOCK:BLOCK + TILE, :] = jnp.where(lane >= HEAD_DIM, k, 0.0).astype(_BF16)
        v_t = v.T
        row = lax.broadcasted_iota(jnp.int32, v_t.shape, 0)
        vt[:, :, 0:BLOCK] = history(vt.at[:, :, TILE:TILE + BLOCK])
        vt[0, :, BLOCK:BLOCK + TILE] = jnp.where(row < HEAD_DIM, v_t, 0.0).astype(_BF16)
        vt[1, :, BLOCK:BLOCK + TILE] = jnp.where(row >= HEAD_DIM, v_t, 0.0).astype(_BF16)

    def prepare_weights():
        def stream(src, rows_per_chunk, n_chunks, width, src_row, consume, placements):
            per_iter = len(placements)
            n_iter = n_chunks // per_iter

            def copies(i, slot):
                return [pltpu.make_async_copy(src.at[pl.ds(src_row(i * per_iter + k), rows_per_chunk), :],
                                              stage.at[slot, r0:r0 + rows_per_chunk, l0:l0 + width],
                                              stage_sem.at[slot])
                        for k, (r0, l0) in enumerate(placements)]

            def start(i, slot):
                for c in copies(i, slot):
                    c.start()

            for i in range(min(WEIGHT_STAGE_SLOTS - 1, n_iter)):
                start(i, i)

            def body(i, carry):
                slot = lax.rem(i, WEIGHT_STAGE_SLOTS)
                ahead = i + WEIGHT_STAGE_SLOTS - 1

                @pl.when(ahead < n_iter)
                def _prefetch():
                    start(ahead, lax.rem(ahead, WEIGHT_STAGE_SLOTS))

                for c in copies(i, slot):
                    c.wait()
                for k, (r0, l0) in enumerate(placements):
                    consume(i * per_iter + k, stage[slot, r0:r0 + rows_per_chunk, l0:l0 + width])
                return carry

            lax.fori_loop(0, n_iter, body, 0)

        rows = WEIGHT_STAGE_ROWS

        def chunk_rows(i):
            return pl.ds(pl.multiple_of(i * rows, rows), rows)

        for n, gain_row_ref in enumerate((gin_row_ref, gffn_row_ref)):
            gain_cols[n] = jnp.broadcast_to(gain_row_ref[...], (BLOCK, D_MODEL)).T

        def gain_col(n, i):
            return gain_cols[n, chunk_rows(i), 0:1]

        def consume_in(i, blk):
            qb = blk[:, POOL_WIDTH:POOL_WIDTH + ATTN_WIDTH]
            heads = [qb[:, (h * GQA_GROUP + g) * HEAD_DIM:(h * GQA_GROUP + g + 1) * HEAD_DIM]
                     for g in range(GQA_GROUP) for h in range(N_KV_HEADS)]
            q_regrouped = jnp.concatenate(heads, axis=1) * (HEAD_DIM ** -0.5)
            full = jnp.concatenate([blk[:, :POOL_WIDTH], q_regrouped, blk[:, POOL_WIDTH + ATTN_WIDTH:]], axis=1)
            win_ref[chunk_rows(i), :] = (gain_col(0, i) * full).astype(_BF16)

        def consume_gated(dst):
            def consume(i, blk):
                dst[chunk_rows(i), :] = (gain_col(1, i) * blk).astype(_BF16)
            return consume

        def consume_down(i, blk):
            wd_ref[chunk_rows(i), :] = blk.astype(_BF16)

        def wout_src_row(i):
            j = i - POOL_WIDTH // WOUT_BLOCK_ROWS
            head = (j % N_KV_HEADS) * GQA_GROUP + j // N_KV_HEADS
            return jnp.where(j < 0, i, POOL_WIDTH // WOUT_BLOCK_ROWS + head) * WOUT_BLOCK_ROWS

        def consume_out(i, blk):
            dst = pl.ds(pl.multiple_of(i * WOUT_BLOCK_ROWS, WOUT_BLOCK_ROWS), WOUT_BLOCK_ROWS)
            wout_ref[dst, :] = blk.astype(_BF16)

        stream(win_hbm, rows, D_MODEL // rows, IN_WIDTH, lambda c: c * rows, consume_in, [(0, 0), (0, IN_WIDTH)])
        stream(wout_hbm, WOUT_BLOCK_ROWS, D_MODEL // WOUT_BLOCK_ROWS, D_MODEL, wout_src_row, consume_out,
               [(0, 0), (WOUT_BLOCK_ROWS, 0), (0, D_MODEL), (WOUT_BLOCK_ROWS, D_MODEL)])
        stream(wg_hbm, rows, D_MODEL // rows, D_FF, lambda c: c * rows, consume_gated(wg_ref), [(0, 0)])
        stream(wu_hbm, rows, D_MODEL // rows, D_FF, lambda c: c * rows, consume_gated(wu_ref), [(0, 0)])
        stream(wd_hbm, rows, D_FF // rows, D_MODEL, lambda c: c * rows, consume_down, [(0, 0), (0, D_MODEL)])
        wpool_ref[...] = wpool_f32_ref[...].astype(_BF16)

    def first_step():
        x_copy(0, 0).start()
        x_copy(1, 1).start()
        prepare_weights()
        x_copy(0, 0).wait()
        stage_a_store(*stage_a_compute(x_ring.at[0]), None)
        bkt = bkt_ref[...]
        key_is_prev = lax.broadcasted_iota(jnp.int32, bkt.shape, 0) < BLOCK
        dist_bucket = jnp.broadcast_to(bkt_ref[BLOCK:BLOCK + 1, :], (8, BLOCK))
        for h in range(N_KV_HEADS):
            for g in range(GQA_GROUP):
                head = h * GQA_GROUP + g
                by_dist = jnp.zeros(dist_bucket.shape, _F32)
                for bb in range(N_BUCKETS):
                    by_dist = jnp.where(dist_bucket == bb, relb_t_ref[head, bb], by_dist)
                tab = pltpu.roll(jnp.broadcast_to(by_dist[0:1, :], bkt.shape), 0, 1, stride=1, stride_axis=0)
                tab = jnp.where(bkt < 0, NEG_INF, tab)
                rows = slice(h * 2 * BLOCK, (h + 1) * 2 * BLOCK)
                cols = slice(g * BLOCK, (g + 1) * BLOCK)
                bias_t[0, rows, cols] = tab
                bias_t[1, rows, cols] = jnp.where(key_is_prev, NEG_INF, tab)

    def step_body(step, carry, with_ab=True, with_c=True):
        seq_tile = lax.rem(jnp.minimum(step, n_tiles - 1), tiles_per_seq)
        next_starts_seq = lax.rem(jnp.minimum(step + 1, n_tiles - 1), tiles_per_seq) == 0
        out_slot = lax.rem(step, 2)

        @pl.when(step + 2 < n_tiles)
        def _prefetch_x():
            x_copy(step + 2, lax.rem(step + 2, X_RING_SLOTS)).start()

        @pl.when(step + 1 < n_tiles)
        def _wait_x():
            x_copy(step + 1, lax.rem(step + 1, X_RING_SLOTS)).wait()

        @pl.when(step >= 3)
        def _wait_y():
            y_copy(step - 3, out_slot).wait()

        x_next = x_ring.at[lax.rem(step + 1, X_RING_SLOTS)]
        x_prev = x_ring.at[lax.rem(jnp.maximum(step - 1, 0), X_RING_SLOTS)]
        y_tile = out_buf.at[out_slot]

        def ffn_gate_up(h2, inv_rms2, c):
            cols = slice(c * FFN_CHUNK, (c + 1) * FFN_CHUNK)
            gate = _dot(h2, wg_ref[:, cols]) * inv_rms2
            up = _dot(h2, wu_ref[:, cols]) * inv_rms2
            return (gate * (1.0 / (1.0 + jnp.exp(-gate))) * up).astype(_BF16)

        def ffn_down(c, act):
            return _dot(act, wd_ref[c * FFN_CHUNK:(c + 1) * FFN_CHUNK, :])

        pos = seq_tile * TILE + lax.broadcasted_iota(jnp.int32, (TILE, 1), 0)

        def pool_group(g):
            w = POOL_WINDOWS[g]
            cols = slice(g * POOL_GROUP_DIM, (g + 1) * POOL_GROUP_DIM)
            ext = u_buf[:, cols]
            acc = ext
            lag = 1
            while lag < w:
                acc = acc + jnp.concatenate([acc[:lag], acc[:-lag]], axis=0)
                lag *= 2
            count = jnp.minimum(pos + 1, w).astype(_F32)
            pooled = acc[POOL_HALO:] / count - ext[POOL_HALO:]
            mixed = _dot(pooled.astype(_BF16), wpool_ref[g]) * pscale_ref[:, cols]
            mixcat[:, cols] = mixed.astype(_BF16)

        col = lax.broadcasted_iota(jnp.int32, (1, GQA_GROUP * BLOCK), 1)
        sink_rows = []
        for h in range(N_KV_HEADS):
            row = jnp.full((1, GQA_GROUP * BLOCK), sinks_ref[h * GQA_GROUP], _F32)
            for g in range(1, GQA_GROUP):
                row = jnp.where(col >= g * BLOCK, sinks_ref[h * GQA_GROUP + g], row)
            sink_rows.append(row)
        first_variant = jnp.where(seq_tile == 0, 1, 0)

        def logits(j):
            band = slice(j * BLOCK, (j + 2) * BLOCK)
            kb = jnp.concatenate([kcat[0, band, :], kcat[1, band, :]], axis=0)
            s = lax.dot_general(kb, q_stack[j], (((1,), (1,)), ((), ())), preferred_element_type=_F32)
            return s + (bias_t[first_variant] if j == 0 else bias_t[0])

        def attend(j, s):
            band = slice(j * BLOCK, (j + 2) * BLOCK)
            probs, inv_denoms = [], []
            for h in range(N_KV_HEADS):
                sh = s[h * 2 * BLOCK:(h + 1) * 2 * BLOCK, :]
                m = jnp.maximum(jnp.max(sh, axis=0, keepdims=True), sink_rows[h])
                p = jnp.exp(sh - m)
                denom = jnp.sum(p, axis=0, keepdims=True) + jnp.exp(sink_rows[h] - m)
                probs.append(p.astype(_BF16))
                inv_denoms.append(1.0 / denom)
            v_both = jnp.concatenate([vt[h, :, band] for h in range(N_KV_HEADS)], axis=1)
            o_t = _dot(v_both, jnp.concatenate(probs, axis=0))
            o_t = jnp.concatenate([o_t[h * HEAD_DIM:(h + 1) * HEAD_DIM] * inv_denoms[h] for h in range(N_KV_HEADS)], axis=0)
            blks = []
            for g in range(GQA_GROUP):
                blks.append(o_t[:, g * BLOCK:(g + 1) * BLOCK].T.astype(_BF16))
            return jnp.concatenate(blks, axis=1)

        half = TILE // N_HALVES
        x1_halves, h2_halves, inv_halves = [], [], []
        for r in range(N_HALVES if with_c else 0):
            rows = slice(r * half, (r + 1) * half)
            x1_r = x_prev[rows, :] + _rmsnorm(mix_buf[rows, :], gpost_ref[...])
            x1_halves.append(x1_r)
            inv_halves.append(lax.rsqrt(jnp.mean(x1_r * x1_r, axis=-1, keepdims=True) + EPS))
            h2_halves.append(x1_r.astype(_BF16))
        next_proj = stage_a_compute(x_next) if with_ab else None
        s_blocks = [logits(j) for j in range(N_BLOCKS)] if with_ab else []
        work = [(r, c) for r in range(N_HALVES) for c in range(N_CHUNKS)]

        def attend_and_store(j):
            mixcat[j * BLOCK:(j + 1) * BLOCK, POOL_WIDTH:] = attend(j, s_blocks[j])

        extras = {}
        if with_ab:
            for j in range(N_BLOCKS):
                extras[ATTEND_SLOTS[j]] = functools.partial(attend_and_store, j)
            for g in range(N_POOL_GROUPS):
                extras[POOL_SLOTS[g]] = functools.partial(pool_group, g)

        def gate_up(k):
            r, c = work[k]
            return ffn_gate_up(h2_halves[r], inv_halves[r], c)

        if not with_c:
            work = []
            for k in sorted(extras):
                extras[k]()
        act_next = gate_up(0) if with_c else None
        acc = None
        for k, (r, c) in enumerate(work):
            act = act_next
            if k + 1 < len(work):
                act_next = gate_up(k + 1)
            part = ffn_down(c, act)
            acc = part if acc is None else acc + part
            if c == N_CHUNKS - 1:
                y_tile[r * half:(r + 1) * half, :] = x1_halves[r] + _rmsnorm(acc, gpost2_ref[...])
                acc = None
            if k in extras:
                extras[k]()
        if with_ab:
            mix_buf[...] = _dot(mixcat[...], wout_ref[...])
            stage_a_store(*next_proj, next_starts_seq)

        @pl.when(step >= 1)
        def _store_y():
            y_copy(step - 1, out_slot).start()

        return carry

    first_step()
    step_body(0, 0, with_c=False)
    lax.fori_loop(1, n_tiles, step_body, 0)
    step_body(n_tiles, 0, with_ab=False)
    y_copy(n_tiles - 2, (n_tiles - 1) % 2).wait()
    y_copy(n_tiles - 1, n_tiles % 2).wait()


def _layer(x, g_pre_mix, w_in, w_pool, pool_scale, rel_bias, sinks, w_out, g_post_mix, g_pre_ffn, w_gate, w_up, w_down,
           g_post_ffn):
    batch, seq, _ = x.shape
    tokens = batch * seq
    n_tiles = tokens // TILE
    tiles_per_seq = seq // TILE
    bucket_t = jnp.asarray(_bucket_table_t())
    x2 = x.reshape(tokens, D_MODEL)
    row = lambda a: a.reshape(1, -1)

    vmem = pl.BlockSpec(memory_space=pltpu.VMEM)
    smem = pl.BlockSpec(memory_space=pltpu.SMEM)
    hbm = pl.BlockSpec(memory_space=pl.ANY)
    out = pl.pallas_call(
        functools.partial(_layer_kernel, tiles_per_seq, n_tiles),
        out_shape=jax.ShapeDtypeStruct(x2.shape, x2.dtype),
        in_specs=[hbm, hbm, vmem, vmem, smem, smem, vmem, hbm, vmem, hbm, hbm, hbm, vmem, vmem, vmem],
        out_specs=hbm,
        scratch_shapes=[
            pltpu.VMEM((POOL_HALO + TILE, POOL_WIDTH), _F32),
            pltpu.VMEM((N_BLOCKS, GQA_GROUP * BLOCK, KV_WIDTH), _BF16),
            pltpu.VMEM((N_KV_HEADS, BLOCK + TILE, KV_WIDTH), _BF16),
            pltpu.VMEM((N_KV_HEADS, KV_WIDTH, BLOCK + TILE), _BF16),
            pltpu.VMEM((TILE, D_MODEL), _BF16),
            pltpu.VMEM((TILE, D_MODEL), _F32),
            pltpu.VMEM((2, N_KV_HEADS * 2 * BLOCK, GQA_GROUP * BLOCK), _F32),
            pltpu.VMEM((D_MODEL, IN_WIDTH), _BF16),
            pltpu.VMEM((N_POOL_GROUPS, POOL_GROUP_DIM, POOL_GROUP_DIM), _BF16),
            pltpu.VMEM((D_MODEL, D_MODEL), _BF16),
            pltpu.VMEM((D_MODEL, D_FF), _BF16),
            pltpu.VMEM((D_MODEL, D_FF), _BF16),
            pltpu.VMEM((D_FF, D_MODEL), _BF16),
            pltpu.VMEM((WEIGHT_STAGE_SLOTS, WEIGHT_STAGE_ROWS, D_FF), _F32),
            pltpu.SemaphoreType.DMA((WEIGHT_STAGE_SLOTS,)),
            pltpu.VMEM((X_RING_SLOTS, TILE, D_MODEL), _F32),
            pltpu.SemaphoreType.DMA((X_RING_SLOTS,)),
            pltpu.VMEM((2, TILE, D_MODEL), _F32),
            pltpu.SemaphoreType.DMA((2,)),
            pltpu.VMEM((2, D_MODEL, BLOCK), _F32),
        ],
        compiler_params=pltpu.CompilerParams(vmem_limit_bytes=VMEM_LIMIT_BYTES),
        name="hybrid_layer",
    )(x2, w_in, w_pool, row(pool_scale), rel_bias.T, sinks, bucket_t, w_out,
      row(g_post_mix), w_gate, w_up, w_down, row(g_post_ffn), row(g_pre_mix), row(g_pre_ffn))
    return out.reshape(x.shape)


def kernel(x, g_pre_mix, w_in, w_pool, pool_scale, rel_bias, sinks, w_out, g_post_mix, g_pre_ffn, w_gate, w_up, w_down, g_post_ffn):
    depth = g_pre_mix.shape[0]
    for l in range(depth):
        x = _layer(x, g_pre_mix[l], w_in[l], w_pool[l], pool_scale[l], rel_bias, sinks[l], w_out[l], g_post_mix[l],
                   g_pre_ffn[l], w_gate[l], w_up[l], w_down[l], g_post_ffn[l])
    return x
```

```python
import functools

import numpy as np
import jax
import jax.numpy as jnp
from jax import lax
from jax.experimental import pallas as pl
from jax.experimental.pallas import tpu as pltpu

D_MODEL = 1024
POOL_WIDTH = 512
POOL_WINDOWS = (2, 4, 8, 16)
N_POOL_GROUPS = len(POOL_WINDOWS)
POOL_GROUP_DIM = POOL_WIDTH // N_POOL_GROUPS
ATTN_WIDTH = 512
HEAD_DIM = 64
N_Q_HEADS = 8
N_KV_HEADS = 2
GQA_GROUP = N_Q_HEADS // N_KV_HEADS
WINDOW = 128
BLOCK = 128
N_BUCKETS = 32
MAX_EXACT = N_BUCKETS // 2
MAX_DISTANCE = 128
KV_WIDTH = N_KV_HEADS * HEAD_DIM
IN_WIDTH = POOL_WIDTH + ATTN_WIDTH + 2 * KV_WIDTH
D_FF = 2816
EPS = 1e-6
NEG_INF = -1e30

POOL_HALO = 16
TILE = 512
N_BLOCKS = TILE // BLOCK
FFN_CHUNK = 256
N_CHUNKS = D_FF // FFN_CHUNK
N_HALVES = 2
X_RING_SLOTS = 4
ATTEND_SLOTS = (0, 2, 4, 6)
POOL_SLOTS = (8, 10, 12, 14)
WEIGHT_STAGE_ROWS = 128
WEIGHT_STAGE_SLOTS = 4
WOUT_BLOCK_ROWS = HEAD_DIM
VMEM_LIMIT_BYTES = 63 * 1024 * 1024

_F32 = jnp.float32
_BF16 = jnp.bfloat16


def _dot(a, b):
    return jnp.dot(a, b, preferred_element_type=_F32)


def _rmsnorm(xf, g):
    ms = jnp.mean(xf * xf, axis=-1, keepdims=True)
    return xf * lax.rsqrt(ms + EPS) * g


def _bucket_table_t():
    qi = np.arange(BLOCK)[None, :]
    kj = np.arange(2 * BLOCK)[:, None]
    dist = qi + BLOCK - kj
    n = np.maximum(dist, 0)
    nf = np.maximum(n, 1).astype(np.float32)
    large = MAX_EXACT + (
        np.log(nf / np.float32(MAX_EXACT)) / np.float32(np.log(MAX_DISTANCE / MAX_EXACT)) * np.float32(N_BUCKETS - MAX_EXACT)
    ).astype(np.int32)
    large = np.minimum(large, N_BUCKETS - 1)
    bucket = np.where(n < MAX_EXACT, n, large)
    in_window = (dist >= 0) & (dist < WINDOW)
    return np.where(in_window, bucket, -1).astype(np.int32)


def _layer_kernel(tiles_per_seq, n_tiles,
                  x_hbm, win_hbm, wpool_f32_ref, pscale_ref, relb_t_ref, sinks_ref, bkt_ref, wout_hbm,
                  gpost_ref, wg_hbm, wu_hbm, wd_hbm, gpost2_ref, gin_row_ref, gffn_row_ref,
                  y_hbm, u_buf, q_stack, kcat, vt, mixcat, mix_buf, bias_t,
                  win_ref, wpool_ref, wout_ref, wg_ref, wu_ref, wd_ref, stage, stage_sem,
                  x_ring, x_sem, out_buf, out_sem, gain_cols, x1_first_buf, h2_first_buf, inv_first_buf):
    def tile_rows(tile):
        start = tile * TILE
        return pl.ds(start if isinstance(start, int) else pl.multiple_of(start, TILE), TILE)

    def x_copy(tile, slot):
        return pltpu.make_async_copy(x_hbm.at[tile_rows(tile), :], x_ring.at[slot], x_sem.at[slot])

    def y_copy(tile, slot):
        return pltpu.make_async_copy(out_buf.at[slot], y_hbm.at[tile_rows(tile), :], out_sem.at[slot])

    def stage_a_compute(x_ref):
        x = x_ref[...]
        inv_rms = lax.rsqrt(jnp.mean(x * x, axis=-1, keepdims=True) + EPS)
        xb = x.astype(_BF16)
        u = _dot(xb, win_ref[:, 0:POOL_WIDTH]) * inv_rms
        q = (_dot(xb, win_ref[:, POOL_WIDTH:POOL_WIDTH + ATTN_WIDTH]) * inv_rms).astype(_BF16)
        kv = _dot(xb, win_ref[:, POOL_WIDTH + ATTN_WIDTH:IN_WIDTH]) * inv_rms
        return u, q, kv

    def stage_a_store(u, q, kv, starts_seq):
        def history(tail):
            if starts_seq is None:
                return jnp.zeros(tail.shape, tail.dtype)
            return jnp.where(starts_seq, jnp.zeros(tail.shape, tail.dtype), tail[...])

        u_buf[0:POOL_HALO, :] = history(u_buf.at[TILE:TILE + POOL_HALO, :])
        u_buf[POOL_HALO:POOL_HALO + TILE, :] = u
        for j in range(N_BLOCKS):
            for g in range(GQA_GROUP):
                q_stack[j, g * BLOCK:(g + 1) * BLOCK, :] = q[j * BLOCK:(j + 1) * BLOCK, g * BLOCK:(g + 1) * BLOCK]
        k = kv[:, 0:KV_WIDTH]
        v = kv[:, KV_WIDTH:2 * KV_WIDTH]
        lane = lax.broadcasted_iota(jnp.int32, k.shape, 1)
        kcat[:, 0:BLOCK, :] = history(kcat.at[:, TILE:TILE + BLOCK, :])
        kcat[0, BLOCK:BLOCK + TILE, :] = jnp.where(lane < HEAD_DIM, k, 0.0).astype(_BF16)
        kcat[1, BLOCK:BLOCK + TILE, :] = jnp.where(lane >= HEAD_DIM, k, 0.0).astype(_BF16)
        v_t = v.T
        row = lax.broadcasted_iota(jnp.int32, v_t.shape, 0)
        vt[:, :, 0:BLOCK] = history(vt.at[:, :, TILE:TILE + BLOCK])
        vt[0, :, BLOCK:BLOCK + TILE] = jnp.where(row < HEAD_DIM, v_t, 0.0).astype(_BF16)
        vt[1, :, BLOCK:BLOCK + TILE] = jnp.where(row >= HEAD_DIM, v_t, 0.0).astype(_BF16)

    def prepare_weights():
        def stream(src, rows_per_chunk, n_chunks, width, src_row, consume, placements):
            per_iter = len(placements)
            n_iter = n_chunks // per_iter

            def copies(i, slot):
                return [pltpu.make_async_copy(src.at[pl.ds(src_row(i * per_iter + k), rows_per_chunk), :],
                                              stage.at[slot, r0:r0 + rows_per_chunk, l0:l0 + width],
                                              stage_sem.at[slot])
                        for k, (r0, l0) in enumerate(placements)]

            def start(i, slot):
                for c in copies(i, slot):
                    c.start()

            for i in range(min(WEIGHT_STAGE_SLOTS - 1, n_iter)):
                start(i, i)

            def body(i, carry):
                slot = lax.rem(i, WEIGHT_STAGE_SLOTS)
                ahead = i + WEIGHT_STAGE_SLOTS - 1

                @pl.when(ahead < n_iter)
                def _prefetch():
                    start(ahead, lax.rem(ahead, WEIGHT_STAGE_SLOTS))

                for c in copies(i, slot):
                    c.wait()
                for k, (r0, l0) in enumerate(placements):
                    consume(i * per_iter + k, stage[slot, r0:r0 + rows_per_chunk, l0:l0 + width])
                return carry

            lax.fori_loop(0, n_iter, body, 0)

        rows = WEIGHT_STAGE_ROWS

        def chunk_rows(i):
            return pl.ds(pl.multiple_of(i * rows, rows), rows)

        for n, gain_row_ref in enumerate((gin_row_ref, gffn_row_ref)):
            gain_cols[n] = jnp.broadcast_to(gain_row_ref[...], (BLOCK, D_MODEL)).T

        def gain_col(n, i):
            return gain_cols[n, chunk_rows(i), 0:1]

        def consume_in(i, blk):
            qb = blk[:, POOL_WIDTH:POOL_WIDTH + ATTN_WIDTH]
            heads = [qb[:, (h * GQA_GROUP + g) * HEAD_DIM:(h * GQA_GROUP + g + 1) * HEAD_DIM]
                     for g in range(GQA_GROUP) for h in range(N_KV_HEADS)]
            q_regrouped = jnp.concatenate(heads, axis=1) * (HEAD_DIM ** -0.5)
            full = jnp.concatenate([blk[:, :POOL_WIDTH], q_regrouped, blk[:, POOL_WIDTH + ATTN_WIDTH:]], axis=1)
            win_ref[chunk_rows(i), :] = (gain_col(0, i) * full).astype(_BF16)

        def consume_gated(dst):
            def consume(i, blk):
                dst[chunk_rows(i), :] = (gain_col(1, i) * blk).astype(_BF16)
            return consume

        def consume_down(i, blk):
            wd_ref[chunk_rows(i), :] = blk.astype(_BF16)

        def wout_src_row(i):
            j = i - POOL_WIDTH // WOUT_BLOCK_ROWS
            head = (j % N_KV_HEADS) * GQA_GROUP + j // N_KV_HEADS
            return jnp.where(j < 0, i, POOL_WIDTH // WOUT_BLOCK_ROWS + head) * WOUT_BLOCK_ROWS

        def consume_out(i, blk):
            dst = pl.ds(pl.multiple_of(i * WOUT_BLOCK_ROWS, WOUT_BLOCK_ROWS), WOUT_BLOCK_ROWS)
            wout_ref[dst, :] = blk.astype(_BF16)

        stream(win_hbm, rows, D_MODEL // rows, IN_WIDTH, lambda c: c * rows, consume_in, [(0, 0), (0, IN_WIDTH)])
        stream(wout_hbm, WOUT_BLOCK_ROWS, D_MODEL // WOUT_BLOCK_ROWS, D_MODEL, wout_src_row, consume_out,
               [(0, 0), (WOUT_BLOCK_ROWS, 0), (0, D_MODEL), (WOUT_BLOCK_ROWS, D_MODEL)])
        stream(wg_hbm, rows, D_MODEL // rows, D_FF, lambda c: c * rows, consume_gated(wg_ref), [(0, 0)])
        stream(wu_hbm, rows, D_MODEL // rows, D_FF, lambda c: c * rows, consume_gated(wu_ref), [(0, 0)])
        stream(wd_hbm, rows, D_FF // rows, D_MODEL, lambda c: c * rows, consume_down, [(0, 0), (0, D_MODEL)])
        wpool_ref[...] = wpool_f32_ref[...].astype(_BF16)

    def first_step():
        x_copy(0, 0).start()
        x_copy(1, 1).start()
        prepare_weights()
        x_copy(0, 0).wait()
        stage_a_store(*stage_a_compute(x_ring.at[0]), None)
        bkt = bkt_ref[...]
        key_is_prev = lax.broadcasted_iota(jnp.int32, bkt.shape, 0) < BLOCK
        for h in range(N_KV_HEADS):
            for g in range(GQA_GROUP):
                head = h * GQA_GROUP + g

                def body(bb, tab):
                    return jnp.where(bkt == bb, relb_t_ref[head, bb], tab)

                tab = lax.fori_loop(0, N_BUCKETS, body, jnp.zeros(bkt.shape, _F32))
                tab = jnp.where(bkt < 0, NEG_INF, tab)
                rows = slice(h * 2 * BLOCK, (h + 1) * 2 * BLOCK)
                cols = slice(g * BLOCK, (g + 1) * BLOCK)
                bias_t[0, rows, cols] = tab
                bias_t[1, rows, cols] = jnp.where(key_is_prev, NEG_INF, tab)

    def step_body(step, carry, with_ab=True, with_c=True):
        seq_tile = lax.rem(jnp.minimum(step, n_tiles - 1), tiles_per_seq)
        next_starts_seq = lax.rem(jnp.minimum(step + 1, n_tiles - 1), tiles_per_seq) == 0
        out_slot = lax.rem(step, 2)

        @pl.when(step + 2 < n_tiles)
        def _prefetch_x():
            x_copy(step + 2, lax.rem(step + 2, X_RING_SLOTS)).start()

        @pl.when(step + 1 < n_tiles)
        def _wait_x():
            x_copy(step + 1, lax.rem(step + 1, X_RING_SLOTS)).wait()

        @pl.when(step >= 3)
        def _wait_y():
            y_copy(step - 3, out_slot).wait()

        x_next = x_ring.at[lax.rem(step + 1, X_RING_SLOTS)]
        x_prev = x_ring.at[lax.rem(jnp.maximum(step - 1, 0), X_RING_SLOTS)]
        x_cur = x_ring.at[lax.rem(jnp.minimum(step, n_tiles - 1), X_RING_SLOTS)]
        y_tile = out_buf.at[out_slot]

        def ffn_gate_up(h2, inv_rms2, c):
            cols = slice(c * FFN_CHUNK, (c + 1) * FFN_CHUNK)
            gate = _dot(h2, wg_ref[:, cols]) * inv_rms2
            up = _dot(h2, wu_ref[:, cols]) * inv_rms2
            return (gate * (1.0 / (1.0 + jnp.exp(-gate))) * up).astype(_BF16)

        def ffn_down(c, act):
            return _dot(act, wd_ref[c * FFN_CHUNK:(c + 1) * FFN_CHUNK, :])

        pos = seq_tile * TILE + lax.broadcasted_iota(jnp.int32, (TILE, 1), 0)

        def pool_group(g):
            w = POOL_WINDOWS[g]
            cols = slice(g * POOL_GROUP_DIM, (g + 1) * POOL_GROUP_DIM)
            ext = u_buf[:, cols]
            acc = ext
            lag = 1
            while lag < w:
                acc = acc + jnp.concatenate([acc[:lag], acc[:-lag]], axis=0)
                lag *= 2
            count = jnp.minimum(pos + 1, w).astype(_F32)
            pooled = acc[POOL_HALO:] / count - ext[POOL_HALO:]
            mixed = _dot(pooled.astype(_BF16), wpool_ref[g]) * pscale_ref[:, cols]
            mixcat[:, cols] = mixed.astype(_BF16)

        col = lax.broadcasted_iota(jnp.int32, (1, GQA_GROUP * BLOCK), 1)
        sink_rows = []
        for h in range(N_KV_HEADS):
            row = jnp.full((1, GQA_GROUP * BLOCK), sinks_ref[h * GQA_GROUP], _F32)
            for g in range(1, GQA_GROUP):
                row = jnp.where(col >= g * BLOCK, sinks_ref[h * GQA_GROUP + g], row)
            sink_rows.append(row)
        first_variant = jnp.where(seq_tile == 0, 1, 0)

        def logits(j):
            band = slice(j * BLOCK, (j + 2) * BLOCK)
            kb = jnp.concatenate([kcat[0, band, :], kcat[1, band, :]], axis=0)
            s = lax.dot_general(kb, q_stack[j], (((1,), (1,)), ((), ())), preferred_element_type=_F32)
            return s + (bias_t[first_variant] if j == 0 else bias_t[0])

        def attend(j, s):
            band = slice(j * BLOCK, (j + 2) * BLOCK)
            probs, inv_denoms = [], []
            for h in range(N_KV_HEADS):
                sh = s[h * 2 * BLOCK:(h + 1) * 2 * BLOCK, :]
                m = jnp.maximum(jnp.max(sh, axis=0, keepdims=True), sink_rows[h])
                p = jnp.exp(sh - m)
                denom = jnp.sum(p, axis=0, keepdims=True) + jnp.exp(sink_rows[h] - m)
                probs.append(p.astype(_BF16))
                inv_denoms.append(1.0 / denom)
            v_both = jnp.concatenate([vt[h, :, band] for h in range(N_KV_HEADS)], axis=1)
            o_t = _dot(v_both, jnp.concatenate(probs, axis=0))
            o_t = jnp.concatenate([o_t[h * HEAD_DIM:(h + 1) * HEAD_DIM] * inv_denoms[h] for h in range(N_KV_HEADS)], axis=0)
            blks = []
            for g in range(GQA_GROUP):
                blks.append(o_t[:, g * BLOCK:(g + 1) * BLOCK].T.astype(_BF16))
            return jnp.concatenate(blks, axis=1)

        half = TILE // N_HALVES
        x1_halves, h2_halves, inv_halves = [], [], []
        if with_c:
            x1_second = x_prev[half:TILE, :] + _rmsnorm(mix_buf[...], gpost_ref[...])
            x1_halves = [None, x1_second]
            h2_halves = [h2_first_buf[...], x1_second.astype(_BF16)]
            inv_halves = [inv_first_buf[...],
                          lax.rsqrt(jnp.mean(x1_second * x1_second, axis=-1, keepdims=True) + EPS)]
        next_proj = stage_a_compute(x_next) if with_ab else None
        s_blocks = [logits(j) for j in range(N_BLOCKS)] if with_ab else []
        work = [(r, c) for r in range(N_HALVES) for c in range(N_CHUNKS)]

        def attend_and_store(j):
            mixcat[j * BLOCK:(j + 1) * BLOCK, POOL_WIDTH:] = attend(j, s_blocks[j])

        extras = {}
        if with_ab:
            for j in range(N_BLOCKS):
                extras[ATTEND_SLOTS[j]] = functools.partial(attend_and_store, j)
            for g in range(N_POOL_GROUPS):
                extras[POOL_SLOTS[g]] = functools.partial(pool_group, g)

        def gate_up(k):
            r, c = work[k]
            return ffn_gate_up(h2_halves[r], inv_halves[r], c)

        if not with_c:
            work = []
            for k in sorted(extras):
                extras[k]()
        act_next = gate_up(0) if with_c else None
        acc = None
        for k, (r, c) in enumerate(work):
            act = act_next
            if k + 1 < len(work):
                act_next = gate_up(k + 1)
            part = ffn_down(c, act)
            acc = part if acc is None else acc + part
            if c == N_CHUNKS - 1:
                x1_r = x1_first_buf[...] if r == 0 else x1_halves[r]
                y_tile[r * half:(r + 1) * half, :] = x1_r + _rmsnorm(acc, gpost2_ref[...])
                acc = None
            if k in extras:
                extras[k]()
        if with_ab:
            mix_first = _dot(mixcat[0:half, :], wout_ref[...])
            x1_first = x_cur[0:half, :] + _rmsnorm(mix_first, gpost_ref[...])
            x1_first_buf[...] = x1_first
            h2_first_buf[...] = x1_first.astype(_BF16)
            inv_first_buf[...] = lax.rsqrt(jnp.mean(x1_first * x1_first, axis=-1, keepdims=True) + EPS)
            mix_buf[...] = _dot(mixcat[half:TILE, :], wout_ref[...])
            stage_a_store(*next_proj, next_starts_seq)

        @pl.when(step >= 1)
        def _store_y():
            y_copy(step - 1, out_slot).start()

        return carry

    first_step()
    step_body(0, 0, with_c=False)
    lax.fori_loop(1, n_tiles, step_body, 0)
    step_body(n_tiles, 0, with_ab=False)
    y_copy(n_tiles - 2, (n_tiles - 1) % 2).wait()
    y_copy(n_tiles - 1, n_tiles % 2).wait()


def _layer(x, g_pre_mix, w_in, w_pool, pool_scale, rel_bias, sinks, w_out, g_post_mix, g_pre_ffn, w_gate, w_up, w_down,
           g_post_ffn):
    batch, seq, _ = x.shape
    tokens = batch * seq
    n_tiles = tokens // TILE
    tiles_per_seq = seq // TILE
    bucket_t = jnp.asarray(_bucket_table_t())
    x2 = x.reshape(tokens, D_MODEL)
    row = lambda a: a.reshape(1, -1)

    vmem = pl.BlockSpec(memory_space=pltpu.VMEM)
    smem = pl.BlockSpec(memory_space=pltpu.SMEM)
    hbm = pl.BlockSpec(memory_space=pl.ANY)
    out = pl.pallas_call(
        functools.partial(_layer_kernel, tiles_per_seq, n_tiles),
        out_shape=jax.ShapeDtypeStruct(x2.shape, x2.dtype),
        in_specs=[hbm, hbm, vmem, vmem, smem, smem, vmem, hbm, vmem, hbm, hbm, hbm, vmem, vmem, vmem],
        out_specs=hbm,
        scratch_shapes=[
            pltpu.VMEM((POOL_HALO + TILE, POOL_WIDTH), _F32),
            pltpu.VMEM((N_BLOCKS, GQA_GROUP * BLOCK, KV_WIDTH), _BF16),
            pltpu.VMEM((N_KV_HEADS, BLOCK + TILE, KV_WIDTH), _BF16),
            pltpu.VMEM((N_KV_HEADS, KV_WIDTH, BLOCK + TILE), _BF16),
            pltpu.VMEM((TILE, D_MODEL), _BF16),
            pltpu.VMEM((TILE // N_HALVES, D_MODEL), _F32),
            pltpu.VMEM((2, N_KV_HEADS * 2 * BLOCK, GQA_GROUP * BLOCK), _F32),
            pltpu.VMEM((D_MODEL, IN_WIDTH), _BF16),
            pltpu.VMEM((N_POOL_GROUPS, POOL_GROUP_DIM, POOL_GROUP_DIM), _BF16),
            pltpu.VMEM((D_MODEL, D_MODEL), _BF16),
            pltpu.VMEM((D_MODEL, D_FF), _BF16),
            pltpu.VMEM((D_MODEL, D_FF), _BF16),
            pltpu.VMEM((D_FF, D_MODEL), _BF16),
            pltpu.VMEM((WEIGHT_STAGE_SLOTS, WEIGHT_STAGE_ROWS, D_FF), _F32),
            pltpu.SemaphoreType.DMA((WEIGHT_STAGE_SLOTS,)),
            pltpu.VMEM((X_RING_SLOTS, TILE, D_MODEL), _F32),
            pltpu.SemaphoreType.DMA((X_RING_SLOTS,)),
            pltpu.VMEM((2, TILE, D_MODEL), _F32),
            pltpu.SemaphoreType.DMA((2,)),
            pltpu.VMEM((2, D_MODEL, BLOCK), _F32),
            pltpu.VMEM((TILE // N_HALVES, D_MODEL), _F32),
            pltpu.VMEM((TILE // N_HALVES, D_MODEL), _BF16),
            pltpu.VMEM((TILE // N_HALVES, 1), _F32),
        ],
        compiler_params=pltpu.CompilerParams(vmem_limit_bytes=VMEM_LIMIT_BYTES),
        name="hybrid_layer",
    )(x2, w_in, w_pool, row(pool_scale), rel_bias.T, sinks, bucket_t, w_out,
      row(g_post_mix), w_gate, w_up, w_down, row(g_post_ffn), row(g_pre_mix), row(g_pre_ffn))
    return out.reshape(x.shape)


def kernel(x, g_pre_mix, w_in, w_pool, pool_scale, rel_bias, sinks, w_out, g_post_mix, g_pre_ffn, w_gate, w_up, w_down, g_post_ffn):
    depth = g_pre_mix.shape[0]
    for l in range(depth):
        x = _layer(x, g_pre_mix[l], w_in[l], w_pool[l], pool_scale[l], rel_bias, sinks[l], w_out[l], g_post_mix[l],
                   g_pre_ffn[l], w_gate[l], w_up[l], w_down[l], g_post_ffn[l])
    return x
```

```python
import functools

import numpy as np
import jax
import jax.numpy as jnp
from jax import lax
from jax.experimental import pallas as pl
from jax.experimental.pallas import tpu as pltpu

D_MODEL = 1024
POOL_WIDTH = 512
POOL_WINDOWS = (2, 4, 8, 16)
N_POOL_GROUPS = len(POOL_WINDOWS)
POOL_GROUP_DIM = POOL_WIDTH // N_POOL_GROUPS
ATTN_WIDTH = 512
HEAD_DIM = 64
N_Q_HEADS = 8
N_KV_HEADS = 2
GQA_GROUP = N_Q_HEADS // N_KV_HEADS
WINDOW = 128
BLOCK = 128
N_BUCKETS = 32
MAX_EXACT = N_BUCKETS // 2
MAX_DISTANCE = 128
KV_WIDTH = N_KV_HEADS * HEAD_DIM
IN_WIDTH = POOL_WIDTH + ATTN_WIDTH + 2 * KV_WIDTH
D_FF = 2816
EPS = 1e-6
NEG_INF = -1e30
LOG2E = 1.4426950408889634

POOL_HALO = 16
TILE = 512
N_BLOCKS = TILE // BLOCK
FFN_CHUNK = 256
N_CHUNKS = D_FF // FFN_CHUNK
N_HALVES = 2
X_RING_SLOTS = 4
ATTEND_SLOTS = (0, 2, 4, 6)
POOL_SLOTS = (8, 10, 12, 14)
WEIGHT_STAGE_ROWS = 128
WEIGHT_STAGE_SLOTS = 4
WOUT_BLOCK_ROWS = HEAD_DIM
VMEM_LIMIT_BYTES = 63 * 1024 * 1024

_F32 = jnp.float32
_BF16 = jnp.bfloat16


def _dot(a, b):
    return jnp.dot(a, b, preferred_element_type=_F32)


def _rmsnorm(xf, g):
    ms = jnp.mean(xf * xf, axis=-1, keepdims=True)
    return xf * lax.rsqrt(ms + EPS) * g


def _bucket_table_t():
    qi = np.arange(BLOCK)[None, :]
    kj = np.arange(2 * BLOCK)[:, None]
    dist = qi + BLOCK - kj
    n = np.maximum(dist, 0)
    nf = np.maximum(n, 1).astype(np.float32)
    large = MAX_EXACT + (
        np.log(nf / np.float32(MAX_EXACT)) / np.float32(np.log(MAX_DISTANCE / MAX_EXACT)) * np.float32(N_BUCKETS - MAX_EXACT)
    ).astype(np.int32)
    large = np.minimum(large, N_BUCKETS - 1)
    bucket = np.where(n < MAX_EXACT, n, large)
    in_window = (dist >= 0) & (dist < WINDOW)
    return np.where(in_window, bucket, -1).astype(np.int32)


def _layer_kernel(tiles_per_seq, n_tiles,
                  x_hbm, win_hbm, wpool_f32_ref, pscale_ref, relb_t_ref, sinks_ref, bkt_ref, wout_hbm,
                  gpost_ref, wg_hbm, wu_hbm, wd_hbm, gpost2_ref, gin_row_ref, gffn_row_ref,
                  y_hbm, u_buf, q_stack, kcat, vt, mixcat, mix_buf, bias_t,
                  win_ref, wpool_ref, wout_ref, wg_ref, wu_ref, wd_ref, stage, stage_sem,
                  x_ring, x_sem, out_buf, out_sem, gain_cols):
    def tile_rows(tile):
        start = tile * TILE
        return pl.ds(start if isinstance(start, int) else pl.multiple_of(start, TILE), TILE)

    def x_copy(tile, slot):
        return pltpu.make_async_copy(x_hbm.at[tile_rows(tile), :], x_ring.at[slot], x_sem.at[slot])

    def y_copy(tile, slot):
        return pltpu.make_async_copy(out_buf.at[slot], y_hbm.at[tile_rows(tile), :], out_sem.at[slot])

    def stage_a_compute(x_ref):
        x = x_ref[...]
        inv_rms = lax.rsqrt(jnp.mean(x * x, axis=-1, keepdims=True) + EPS)
        xb = x.astype(_BF16)
        u = _dot(xb, win_ref[:, 0:POOL_WIDTH]) * inv_rms
        q = (_dot(xb, win_ref[:, POOL_WIDTH:POOL_WIDTH + ATTN_WIDTH]) * (inv_rms * LOG2E)).astype(_BF16)
        kv = _dot(xb, win_ref[:, POOL_WIDTH + ATTN_WIDTH:IN_WIDTH]) * inv_rms
        return u, q, kv

    def stage_a_store(u, q, kv, starts_seq):
        def history(tail):
            if starts_seq is None:
                return jnp.zeros(tail.shape, tail.dtype)
            return jnp.where(starts_seq, jnp.zeros(tail.shape, tail.dtype), tail[...])

        u_buf[0:POOL_HALO, :] = history(u_buf.at[TILE:TILE + POOL_HALO, :])
        u_buf[POOL_HALO:POOL_HALO + TILE, :] = u
        for j in range(N_BLOCKS):
            for g in range(GQA_GROUP):
                q_stack[j, g * BLOCK:(g + 1) * BLOCK, :] = q[j * BLOCK:(j + 1) * BLOCK, g * BLOCK:(g + 1) * BLOCK]
        k = kv[:, 0:KV_WIDTH]
        v = kv[:, KV_WIDTH:2 * KV_WIDTH]
        lane = lax.broadcasted_iota(jnp.int32, k.shape, 1)
        kcat[:, 0:BLOCK, :] = history(kcat.at[:, TILE:TILE + BLOCK, :])
        kcat[0, BLOCK:BLOCK + TILE, :] = jnp.where(lane < HEAD_DIM, k, 0.0).astype(_BF16)
        kcat[1, BLOCK:BLOCK + TILE, :] = jnp.where(lane >= HEAD_DIM, k, 0.0).astype(_BF16)
        v_t = v.T
        row = lax.broadcasted_iota(jnp.int32, v_t.shape, 0)
        vt[:, :, 0:BLOCK] = history(vt.at[:, :, TILE:TILE + BLOCK])
        vt[0, :, BLOCK:BLOCK + TILE] = jnp.where(row < HEAD_DIM, v_t, 0.0).astype(_BF16)
        vt[1, :, BLOCK:BLOCK + TILE] = jnp.where(row >= HEAD_DIM, v_t, 0.0).astype(_BF16)

    def prepare_weights():
        def stream(src, rows_per_chunk, n_chunks, width, src_row, consume, placements):
            per_iter = len(placements)
            n_iter = n_chunks // per_iter

            def copies(i, slot):
                return [pltpu.make_async_copy(src.at[pl.ds(src_row(i * per_iter + k), rows_per_chunk), :],
                                              stage.at[slot, r0:r0 + rows_per_chunk, l0:l0 + width],
                                              stage_sem.at[slot])
                        for k, (r0, l0) in enumerate(placements)]

            def start(i, slot):
                for c in copies(i, slot):
                    c.start()

            for i in range(min(WEIGHT_STAGE_SLOTS - 1, n_iter)):
                start(i, i)

            def body(i, carry):
                slot = lax.rem(i, WEIGHT_STAGE_SLOTS)
                ahead = i + WEIGHT_STAGE_SLOTS - 1

                @pl.when(ahead < n_iter)
                def _prefetch():
                    start(ahead, lax.rem(ahead, WEIGHT_STAGE_SLOTS))

                for c in copies(i, slot):
                    c.wait()
                for k, (r0, l0) in enumerate(placements):
                    consume(i * per_iter + k, stage[slot, r0:r0 + rows_per_chunk, l0:l0 + width])
                return carry

            lax.fori_loop(0, n_iter, body, 0)

        rows = WEIGHT_STAGE_ROWS

        def chunk_rows(i):
            return pl.ds(pl.multiple_of(i * rows, rows), rows)

        for n, gain_row_ref in enumerate((gin_row_ref, gffn_row_ref)):
            gain_cols[n] = jnp.broadcast_to(gain_row_ref[...], (BLOCK, D_MODEL)).T

        def gain_col(n, i):
            return gain_cols[n, chunk_rows(i), 0:1]

        def consume_in(i, blk):
            qb = blk[:, POOL_WIDTH:POOL_WIDTH + ATTN_WIDTH]
            heads = [qb[:, (h * GQA_GROUP + g) * HEAD_DIM:(h * GQA_GROUP + g + 1) * HEAD_DIM]
                     for g in range(GQA_GROUP) for h in range(N_KV_HEADS)]
            q_regrouped = jnp.concatenate(heads, axis=1) * (HEAD_DIM ** -0.5)
            full = jnp.concatenate([blk[:, :POOL_WIDTH], q_regrouped, blk[:, POOL_WIDTH + ATTN_WIDTH:]], axis=1)
            win_ref[chunk_rows(i), :] = (gain_col(0, i) * full).astype(_BF16)

        def consume_gated(dst):
            def consume(i, blk):
                dst[chunk_rows(i), :] = (gain_col(1, i) * blk).astype(_BF16)
            return consume

        def consume_down(i, blk):
            wd_ref[chunk_rows(i), :] = blk.astype(_BF16)

        def wout_src_row(i):
            j = i - POOL_WIDTH // WOUT_BLOCK_ROWS
            head = (j % N_KV_HEADS) * GQA_GROUP + j // N_KV_HEADS
            return jnp.where(j < 0, i, POOL_WIDTH // WOUT_BLOCK_ROWS + head) * WOUT_BLOCK_ROWS

        def consume_out(i, blk):
            dst = pl.ds(pl.multiple_of(i * WOUT_BLOCK_ROWS, WOUT_BLOCK_ROWS), WOUT_BLOCK_ROWS)
            wout_ref[dst, :] = blk.astype(_BF16)

        stream(win_hbm, rows, D_MODEL // rows, IN_WIDTH, lambda c: c * rows, consume_in, [(0, 0), (0, IN_WIDTH)])
        stream(wout_hbm, WOUT_BLOCK_ROWS, D_MODEL // WOUT_BLOCK_ROWS, D_MODEL, wout_src_row, consume_out,
               [(0, 0), (WOUT_BLOCK_ROWS, 0), (0, D_MODEL), (WOUT_BLOCK_ROWS, D_MODEL)])
        stream(wg_hbm, rows, D_MODEL // rows, D_FF, lambda c: c * rows, consume_gated(wg_ref), [(0, 0)])
        stream(wu_hbm, rows, D_MODEL // rows, D_FF, lambda c: c * rows, consume_gated(wu_ref), [(0, 0)])
        stream(wd_hbm, rows, D_FF // rows, D_MODEL, lambda c: c * rows, consume_down, [(0, 0), (0, D_MODEL)])
        wpool_ref[...] = wpool_f32_ref[...].astype(_BF16)

    def first_step():
        x_copy(0, 0).start()
        x_copy(1, 1).start()
        prepare_weights()
        x_copy(0, 0).wait()
        stage_a_store(*stage_a_compute(x_ring.at[0]), None)
        bkt = bkt_ref[...]
        key_is_prev = lax.broadcasted_iota(jnp.int32, bkt.shape, 0) < BLOCK
        for h in range(N_KV_HEADS):
            for g in range(GQA_GROUP):
                head = h * GQA_GROUP + g

                def body(bb, tab):
                    return jnp.where(bkt == bb, relb_t_ref[head, bb], tab)

                tab = lax.fori_loop(0, N_BUCKETS, body, jnp.zeros(bkt.shape, _F32))
                tab = jnp.where(bkt < 0, NEG_INF, tab * LOG2E)
                rows = slice(h * 2 * BLOCK, (h + 1) * 2 * BLOCK)
                cols = slice(g * BLOCK, (g + 1) * BLOCK)
                bias_t[0, rows, cols] = tab
                bias_t[1, rows, cols] = jnp.where(key_is_prev, NEG_INF, tab)

    def step_body(step, carry, with_ab=True, with_c=True):
        seq_tile = lax.rem(jnp.minimum(step, n_tiles - 1), tiles_per_seq)
        next_starts_seq = lax.rem(jnp.minimum(step + 1, n_tiles - 1), tiles_per_seq) == 0
        out_slot = lax.rem(step, 2)

        @pl.when(step + 2 < n_tiles)
        def _prefetch_x():
            x_copy(step + 2, lax.rem(step + 2, X_RING_SLOTS)).start()

        @pl.when(step + 1 < n_tiles)
        def _wait_x():
            x_copy(step + 1, lax.rem(step + 1, X_RING_SLOTS)).wait()

        @pl.when(step >= 3)
        def _wait_y():
            y_copy(step - 3, out_slot).wait()

        x_next = x_ring.at[lax.rem(step + 1, X_RING_SLOTS)]
        x_prev = x_ring.at[lax.rem(jnp.maximum(step - 1, 0), X_RING_SLOTS)]
        y_tile = out_buf.at[out_slot]

        def ffn_gate_up(h2, inv_rms2, c):
            cols = slice(c * FFN_CHUNK, (c + 1) * FFN_CHUNK)
            gate = _dot(h2, wg_ref[:, cols]) * inv_rms2
            up = _dot(h2, wu_ref[:, cols]) * inv_rms2
            return (gate * (1.0 / (1.0 + jnp.exp(-gate))) * up).astype(_BF16)

        def ffn_down(c, act):
            return _dot(act, wd_ref[c * FFN_CHUNK:(c + 1) * FFN_CHUNK, :])

        pos = seq_tile * TILE + lax.broadcasted_iota(jnp.int32, (TILE, 1), 0)

        def pool_group(g):
            w = POOL_WINDOWS[g]
            cols = slice(g * POOL_GROUP_DIM, (g + 1) * POOL_GROUP_DIM)
            ext = u_buf[:, cols]
            acc = ext
            lag = 1
            while lag < w:
                acc = acc + jnp.concatenate([acc[:lag], acc[:-lag]], axis=0)
                lag *= 2
            count = jnp.minimum(pos + 1, w).astype(_F32)
            pooled = acc[POOL_HALO:] / count - ext[POOL_HALO:]
            mixed = _dot(pooled.astype(_BF16), wpool_ref[g]) * pscale_ref[:, cols]
            mixcat[:, cols] = mixed.astype(_BF16)

        col = lax.broadcasted_iota(jnp.int32, (1, GQA_GROUP * BLOCK), 1)
        sink_rows = []
        for h in range(N_KV_HEADS):
            row = jnp.full((1, GQA_GROUP * BLOCK), sinks_ref[h * GQA_GROUP], _F32)
            for g in range(1, GQA_GROUP):
                row = jnp.where(col >= g * BLOCK, sinks_ref[h * GQA_GROUP + g], row)
            sink_rows.append(row * LOG2E)
        first_variant = jnp.where(seq_tile == 0, 1, 0)

        def logits(j):
            band = slice(j * BLOCK, (j + 2) * BLOCK)
            kb = jnp.concatenate([kcat[0, band, :], kcat[1, band, :]], axis=0)
            s = lax.dot_general(kb, q_stack[j], (((1,), (1,)), ((), ())), preferred_element_type=_F32)
            return s + (bias_t[first_variant] if j == 0 else bias_t[0])

        def attend(j, s):
            band = slice(j * BLOCK, (j + 2) * BLOCK)
            probs, inv_denoms = [], []
            for h in range(N_KV_HEADS):
                sh = s[h * 2 * BLOCK:(h + 1) * 2 * BLOCK, :]
                m = jnp.maximum(jnp.max(sh, axis=0, keepdims=True), sink_rows[h])
                p = jnp.exp2(sh - m)
                denom = jnp.sum(p, axis=0, keepdims=True) + jnp.exp2(sink_rows[h] - m)
                probs.append(p.astype(_BF16))
                inv_denoms.append(1.0 / denom)
            v_both = jnp.concatenate([vt[h, :, band] for h in range(N_KV_HEADS)], axis=1)
            o_t = _dot(v_both, jnp.concatenate(probs, axis=0))
            o_t = jnp.concatenate([o_t[h * HEAD_DIM:(h + 1) * HEAD_DIM] * inv_denoms[h] for h in range(N_KV_HEADS)], axis=0)
            blks = []
            for g in range(GQA_GROUP):
                blks.append(o_t[:, g * BLOCK:(g + 1) * BLOCK].T.astype(_BF16))
            return jnp.concatenate(blks, axis=1)

        half = TILE // N_HALVES
        x1_halves, h2_halves, inv_halves = [], [], []
        for r in range(N_HALVES if with_c else 0):
            rows = slice(r * half, (r + 1) * half)
            x1_r = x_prev[rows, :] + _rmsnorm(mix_buf[rows, :], gpost_ref[...])
            x1_halves.append(x1_r)
            inv_halves.append(lax.rsqrt(jnp.mean(x1_r * x1_r, axis=-1, keepdims=True) + EPS))
            h2_halves.append(x1_r.astype(_BF16))
        next_proj = stage_a_compute(x_next) if with_ab else None
        s_blocks = [logits(j) for j in range(N_BLOCKS)] if with_ab else []
        work = [(r, c) for r in range(N_HALVES) for c in range(N_CHUNKS)]

        def attend_and_store(j):
            mixcat[j * BLOCK:(j + 1) * BLOCK, POOL_WIDTH:] = attend(j, s_blocks[j])

        extras = {}
        if with_ab:
            for j in range(N_BLOCKS):
                extras[ATTEND_SLOTS[j]] = functools.partial(attend_and_store, j)
            for g in range(N_POOL_GROUPS):
                extras[POOL_SLOTS[g]] = functools.partial(pool_group, g)

        def gate_up(k):
            r, c = work[k]
            return ffn_gate_up(h2_halves[r], inv_halves[r], c)

        if not with_c:
            work = []
            for k in sorted(extras):
                extras[k]()
        act_next = gate_up(0) if with_c else None
        acc = None
        for k, (r, c) in enumerate(work):
            act = act_next
            if k + 1 < len(work):
                act_next = gate_up(k + 1)
            part = ffn_down(c, act)
            acc = part if acc is None else acc + part
            if c == N_CHUNKS - 1:
                y_tile[r * half:(r + 1) * half, :] = x1_halves[r] + _rmsnorm(acc, gpost2_ref[...])
                acc = None
            if k in extras:
                extras[k]()
        if with_ab:
            mix_buf[...] = _dot(mixcat[...], wout_ref[...])
            stage_a_store(*next_proj, next_starts_seq)

        @pl.when(step >= 1)
        def _store_y():
            y_copy(step - 1, out_slot).start()

        return carry

    first_step()
    step_body(0, 0, with_c=False)
    lax.fori_loop(1, n_tiles, step_body, 0)
    step_body(n_tiles, 0, with_ab=False)
    y_copy(n_tiles - 2, (n_tiles - 1) % 2).wait()
    y_copy(n_tiles - 1, n_tiles % 2).wait()


def _layer(x, g_pre_mix, w_in, w_pool, pool_scale, rel_bias, sinks, w_out, g_post_mix, g_pre_ffn, w_gate, w_up, w_down,
           g_post_ffn):
    batch, seq, _ = x.shape
    tokens = batch * seq
    n_tiles = tokens // TILE
    tiles_per_seq = seq // TILE
    bucket_t = jnp.asarray(_bucket_table_t())
    x2 = x.reshape(tokens, D_MODEL)
    row = lambda a: a.reshape(1, -1)

    vmem = pl.BlockSpec(memory_space=pltpu.VMEM)
    smem = pl.BlockSpec(memory_space=pltpu.SMEM)
    hbm = pl.BlockSpec(memory_space=pl.ANY)
    out = pl.pallas_call(
        functools.partial(_layer_kernel, tiles_per_seq, n_tiles),
        out_shape=jax.ShapeDtypeStruct(x2.shape, x2.dtype),
        in_specs=[hbm, hbm, vmem, vmem, smem, smem, vmem, hbm, vmem, hbm, hbm, hbm, vmem, vmem, vmem],
        out_specs=hbm,
        scratch_shapes=[
            pltpu.VMEM((POOL_HALO + TILE, POOL_WIDTH), _F32),
            pltpu.VMEM((N_BLOCKS, GQA_GROUP * BLOCK, KV_WIDTH), _BF16),
            pltpu.VMEM((N_KV_HEADS, BLOCK + TILE, KV_WIDTH), _BF16),
            pltpu.VMEM((N_KV_HEADS, KV_WIDTH, BLOCK + TILE), _BF16),
            pltpu.VMEM((TILE, D_MODEL), _BF16),
            pltpu.VMEM((TILE, D_MODEL), _F32),
            pltpu.VMEM((2, N_KV_HEADS * 2 * BLOCK, GQA_GROUP * BLOCK), _F32),
            pltpu.VMEM((D_MODEL, IN_WIDTH), _BF16),
            pltpu.VMEM((N_POOL_GROUPS, POOL_GROUP_DIM, POOL_GROUP_DIM), _BF16),
            pltpu.VMEM((D_MODEL, D_MODEL), _BF16),
            pltpu.VMEM((D_MODEL, D_FF), _BF16),
            pltpu.VMEM((D_MODEL, D_FF), _BF16),
            pltpu.VMEM((D_FF, D_MODEL), _BF16),
            pltpu.VMEM((WEIGHT_STAGE_SLOTS, WEIGHT_STAGE_ROWS, D_FF), _F32),
            pltpu.SemaphoreType.DMA((WEIGHT_STAGE_SLOTS,)),
            pltpu.VMEM((X_RING_SLOTS, TILE, D_MODEL), _F32),
            pltpu.SemaphoreType.DMA((X_RING_SLOTS,)),
            pltpu.VMEM((2, TILE, D_MODEL), _F32),
            pltpu.SemaphoreType.DMA((2,)),
            pltpu.VMEM((2, D_MODEL, BLOCK), _F32),
        ],
        compiler_params=pltpu.CompilerParams(vmem_limit_bytes=VMEM_LIMIT_BYTES),
        name="hybrid_layer",
    )(x2, w_in, w_pool, row(pool_scale), rel_bias.T, sinks, bucket_t, w_out,
      row(g_post_mix), w_gate, w_up, w_down, row(g_post_ffn), row(g_pre_mix), row(g_pre_ffn))
    return out.reshape(x.shape)


def kernel(x, g_pre_mix, w_in, w_pool, pool_scale, rel_bias, sinks, w_out, g_post_mix, g_pre_ffn, w_gate, w_up, w_down, g_post_ffn):
    depth = g_pre_mix.shape[0]
    for l in range(depth):
        x = _layer(x, g_pre_mix[l], w_in[l], w_pool[l], pool_scale[l], rel_bias, sinks[l], w_out[l], g_post_mix[l],
                   g_pre_ffn[l], w_gate[l], w_up[l], w_down[l], g_post_ffn[l])
    return x
```

```python
import functools

import numpy as np
import jax
import jax.numpy as jnp
from jax import lax
from jax.experimental import pallas as pl
from jax.experimental.pallas import tpu as pltpu

D_MODEL = 1024
POOL_WIDTH = 512
POOL_WINDOWS = (2, 4, 8, 16)
N_POOL_GROUPS = len(POOL_WINDOWS)
POOL_GROUP_DIM = POOL_WIDTH // N_POOL_GROUPS
ATTN_WIDTH = 512
HEAD_DIM = 64
N_Q_HEADS = 8
N_KV_HEADS = 2
GQA_GROUP = N_Q_HEADS // N_KV_HEADS
WINDOW = 128
BLOCK = 128
N_BUCKETS = 32
MAX_EXACT = N_BUCKETS // 2
MAX_DISTANCE = 128
KV_WIDTH = N_KV_HEADS * HEAD_DIM
IN_WIDTH = POOL_WIDTH + ATTN_WIDTH + 2 * KV_WIDTH
D_FF = 2816
EPS = 1e-6
NEG_INF = -1e30
LOG2E = 1.4426950408889634

POOL_HALO = 16
TILE = 512
N_BLOCKS = TILE // BLOCK
FFN_CHUNK = 256
N_CHUNKS = D_FF // FFN_CHUNK
N_HALVES = 2
X_RING_SLOTS = 4
ATTEND_SLOTS = (0, 2, 4, 6)
POOL_SLOTS = (8, 10, 12, 14)
WEIGHT_STAGE_ROWS = 128
WEIGHT_STAGE_SLOTS = 4
WOUT_BLOCK_ROWS = HEAD_DIM
VMEM_LIMIT_BYTES = 63 * 1024 * 1024

_F32 = jnp.float32
_BF16 = jnp.bfloat16


def _dot(a, b):
    return jnp.dot(a, b, preferred_element_type=_F32)


def _rmsnorm(xf, g):
    ms = jnp.mean(xf * xf, axis=-1, keepdims=True)
    return xf * lax.rsqrt(ms + EPS) * g


def _bucket_table_t():
    qi = np.arange(BLOCK)[None, :]
    kj = np.arange(2 * BLOCK)[:, None]
    dist = qi + BLOCK - kj
    n = np.maximum(dist, 0)
    nf = np.maximum(n, 1).astype(np.float32)
    large = MAX_EXACT + (
        np.log(nf / np.float32(MAX_EXACT)) / np.float32(np.log(MAX_DISTANCE / MAX_EXACT)) * np.float32(N_BUCKETS - MAX_EXACT)
    ).astype(np.int32)
    large = np.minimum(large, N_BUCKETS - 1)
    bucket = np.where(n < MAX_EXACT, n, large)
    in_window = (dist >= 0) & (dist < WINDOW)
    return np.where(in_window, bucket, -1).astype(np.int32)


def _layer_kernel(tiles_per_seq, n_tiles,
                  x_hbm, win_hbm, wpool_f32_ref, pscale_ref, relb_t_ref, sinks_ref, bkt_ref, wout_hbm,
                  gpost_ref, wg_hbm, wu_hbm, wd_hbm, gpost2_ref, gin_row_ref, gffn_row_ref,
                  y_hbm, u_buf, q_stack, kcat, vt, mixcat, mix_buf, bias_t,
                  win_ref, wpool_ref, wout_ref, wg_ref, wu_ref, wd_ref, stage, stage_sem,
                  x_ring, x_sem, out_buf, out_sem, gain_cols):
    def tile_rows(tile):
        start = tile * TILE
        return pl.ds(start if isinstance(start, int) else pl.multiple_of(start, TILE), TILE)

    def x_copy(tile, slot):
        return pltpu.make_async_copy(x_hbm.at[tile_rows(tile), :], x_ring.at[slot], x_sem.at[slot])

    def y_copy(tile, slot):
        return pltpu.make_async_copy(out_buf.at[slot], y_hbm.at[tile_rows(tile), :], out_sem.at[slot])

    def stage_a_compute(x_ref):
        x = x_ref[...]
        inv_rms = lax.rsqrt(jnp.mean(x * x, axis=-1, keepdims=True) + EPS)
        xb = x.astype(_BF16)
        u = _dot(xb, win_ref[:, 0:POOL_WIDTH]) * inv_rms
        q = (_dot(xb, win_ref[:, POOL_WIDTH:POOL_WIDTH + ATTN_WIDTH]) * (inv_rms * LOG2E)).astype(_BF16)
        kv = _dot(xb, win_ref[:, POOL_WIDTH + ATTN_WIDTH:IN_WIDTH]) * inv_rms
        return u, q, kv

    def stage_a_store(u, q, kv, starts_seq):
        def history(tail):
            if starts_seq is None:
                return jnp.zeros(tail.shape, tail.dtype)
            return jnp.where(starts_seq, jnp.zeros(tail.shape, tail.dtype), tail[...])

        u_buf[0:POOL_HALO, :] = history(u_buf.at[TILE:TILE + POOL_HALO, :])
        u_buf[POOL_HALO:POOL_HALO + TILE, :] = u
        for j in range(N_BLOCKS):
            for g in range(GQA_GROUP):
                q_stack[j, g * BLOCK:(g + 1) * BLOCK, :] = q[j * BLOCK:(j + 1) * BLOCK, g * BLOCK:(g + 1) * BLOCK]
        k = kv[:, 0:KV_WIDTH]
        v = kv[:, KV_WIDTH:2 * KV_WIDTH]
        lane = lax.broadcasted_iota(jnp.int32, k.shape, 1)
        kcat[:, 0:BLOCK, :] = history(kcat.at[:, TILE:TILE + BLOCK, :])
        kcat[0, BLOCK:BLOCK + TILE, :] = jnp.where(lane < HEAD_DIM, k, 0.0).astype(_BF16)
        kcat[1, BLOCK:BLOCK + TILE, :] = jnp.where(lane >= HEAD_DIM, k, 0.0).astype(_BF16)
        v_t = v.T
        row = lax.broadcasted_iota(jnp.int32, v_t.shape, 0)
        vt[:, :, 0:BLOCK] = history(vt.at[:, :, TILE:TILE + BLOCK])
        vt[0, :, BLOCK:BLOCK + TILE] = jnp.where(row < HEAD_DIM, v_t, 0.0).astype(_BF16)
        vt[1, :, BLOCK:BLOCK + TILE] = jnp.where(row >= HEAD_DIM, v_t, 0.0).astype(_BF16)

    def prepare_weights():
        def stream(src, rows_per_chunk, n_chunks, width, src_row, consume, placements):
            per_iter = len(placements)
            n_iter = n_chunks // per_iter

            def copies(i, slot):
                return [pltpu.make_async_copy(src.at[pl.ds(src_row(i * per_iter + k), rows_per_chunk), :],
                                              stage.at[slot, r0:r0 + rows_per_chunk, l0:l0 + width],
                                              stage_sem.at[slot])
                        for k, (r0, l0) in enumerate(placements)]

            def start(i, slot):
                for c in copies(i, slot):
                    c.start()

            for i in range(min(WEIGHT_STAGE_SLOTS - 1, n_iter)):
                start(i, i)

            def body(i, carry):
                slot = lax.rem(i, WEIGHT_STAGE_SLOTS)
                ahead = i + WEIGHT_STAGE_SLOTS - 1

                @pl.when(ahead < n_iter)
                def _prefetch():
                    start(ahead, lax.rem(ahead, WEIGHT_STAGE_SLOTS))

                for c in copies(i, slot):
                    c.wait()
                for k, (r0, l0) in enumerate(placements):
                    consume(i * per_iter + k, stage[slot, r0:r0 + rows_per_chunk, l0:l0 + width])
                return carry

            lax.fori_loop(0, n_iter, body, 0)

        rows = WEIGHT_STAGE_ROWS

        def chunk_rows(i):
            return pl.ds(pl.multiple_of(i * rows, rows), rows)

        for n, gain_row_ref in enumerate((gin_row_ref, gffn_row_ref)):
            gain_cols[n] = jnp.broadcast_to(gain_row_ref[...], (BLOCK, D_MODEL)).T

        def gain_col(n, i):
            return gain_cols[n, chunk_rows(i), 0:1]

        def consume_in(i, blk):
            qb = blk[:, POOL_WIDTH:POOL_WIDTH + ATTN_WIDTH]
            heads = [qb[:, (h * GQA_GROUP + g) * HEAD_DIM:(h * GQA_GROUP + g + 1) * HEAD_DIM]
                     for g in range(GQA_GROUP) for h in range(N_KV_HEADS)]
            q_regrouped = jnp.concatenate(heads, axis=1) * (HEAD_DIM ** -0.5)
            full = jnp.concatenate([blk[:, :POOL_WIDTH], q_regrouped, blk[:, POOL_WIDTH + ATTN_WIDTH:]], axis=1)
            win_ref[chunk_rows(i), :] = (gain_col(0, i) * full).astype(_BF16)

        def consume_gated(dst):
            def consume(i, blk):
                dst[chunk_rows(i), :] = (gain_col(1, i) * blk).astype(_BF16)
            return consume

        def consume_down(i, blk):
            wd_ref[chunk_rows(i), :] = blk.astype(_BF16)

        def wout_src_row(i):
            j = i - POOL_WIDTH // WOUT_BLOCK_ROWS
            head = (j % N_KV_HEADS) * GQA_GROUP + j // N_KV_HEADS
            return jnp.where(j < 0, i, POOL_WIDTH // WOUT_BLOCK_ROWS + head) * WOUT_BLOCK_ROWS

        def consume_out(i, blk):
            dst = pl.ds(pl.multiple_of(i * WOUT_BLOCK_ROWS, WOUT_BLOCK_ROWS), WOUT_BLOCK_ROWS)
            wout_ref[dst, :] = blk.astype(_BF16)

        stream(win_hbm, rows, D_MODEL // rows, IN_WIDTH, lambda c: c * rows, consume_in, [(0, 0), (0, IN_WIDTH)])
        stream(wout_hbm, WOUT_BLOCK_ROWS, D_MODEL // WOUT_BLOCK_ROWS, D_MODEL, wout_src_row, consume_out,
               [(0, 0), (WOUT_BLOCK_ROWS, 0), (0, D_MODEL), (WOUT_BLOCK_ROWS, D_MODEL)])
        stream(wg_hbm, rows, D_MODEL // rows, D_FF, lambda c: c * rows, consume_gated(wg_ref), [(0, 0)])
        stream(wu_hbm, rows, D_MODEL // rows, D_FF, lambda c: c * rows, consume_gated(wu_ref), [(0, 0)])
        stream(wd_hbm, rows, D_FF // rows, D_MODEL, lambda c: c * rows, consume_down, [(0, 0), (0, D_MODEL)])
        wpool_ref[...] = wpool_f32_ref[...].astype(_BF16)

    def first_step():
        x_copy(0, 0).start()
        x_copy(1, 1).start()
        prepare_weights()
        x_copy(0, 0).wait()
        stage_a_store(*stage_a_compute(x_ring.at[0]), None)
        bkt = bkt_ref[...]
        key_is_prev = lax.broadcasted_iota(jnp.int32, bkt.shape, 0) < BLOCK
        for h in range(N_KV_HEADS):
            for g in range(GQA_GROUP):
                head = h * GQA_GROUP + g

                def body(bb, tab):
                    return jnp.where(bkt == bb, relb_t_ref[head, bb], tab)

                tab = lax.fori_loop(0, N_BUCKETS, body, jnp.zeros(bkt.shape, _F32))
                tab = jnp.where(bkt < 0, NEG_INF, tab * LOG2E)
                rows = slice(h * 2 * BLOCK, (h + 1) * 2 * BLOCK)
                cols = slice(g * BLOCK, (g + 1) * BLOCK)
                bias_t[0, rows, cols] = tab
                bias_t[1, rows, cols] = jnp.where(key_is_prev, NEG_INF, tab)

    def step_body(step, carry, with_ab=True, with_c=True):
        seq_tile = lax.rem(jnp.minimum(step, n_tiles - 1), tiles_per_seq)
        next_starts_seq = lax.rem(jnp.minimum(step + 1, n_tiles - 1), tiles_per_seq) == 0
        out_slot = lax.rem(step, 2)

        @pl.when(step + 2 < n_tiles)
        def _prefetch_x():
            x_copy(step + 2, lax.rem(step + 2, X_RING_SLOTS)).start()

        @pl.when(step + 1 < n_tiles)
        def _wait_x():
            x_copy(step + 1, lax.rem(step + 1, X_RING_SLOTS)).wait()

        @pl.when(step >= 3)
        def _wait_y():
            y_copy(step - 3, out_slot).wait()

        x_next = x_ring.at[lax.rem(step + 1, X_RING_SLOTS)]
        x_prev = x_ring.at[lax.rem(jnp.maximum(step - 1, 0), X_RING_SLOTS)]
        y_tile = out_buf.at[out_slot]

        def ffn_gate_up(h2, inv_rms2, c):
            cols = slice(c * FFN_CHUNK, (c + 1) * FFN_CHUNK)
            dg = _dot(h2, wg_ref[:, cols])
            du = _dot(h2, wu_ref[:, cols])
            return (dg * du * (1.0 / (1.0 + jnp.exp2(dg * (inv_rms2 * (-LOG2E)))))).astype(_BF16)

        def ffn_down(c, act):
            return _dot(act, wd_ref[c * FFN_CHUNK:(c + 1) * FFN_CHUNK, :])

        pos = seq_tile * TILE + lax.broadcasted_iota(jnp.int32, (TILE, 1), 0)

        def pool_group(g):
            w = POOL_WINDOWS[g]
            cols = slice(g * POOL_GROUP_DIM, (g + 1) * POOL_GROUP_DIM)
            ext = u_buf[:, cols]
            acc = ext
            lag = 1
            while lag < w:
                acc = acc + jnp.concatenate([acc[:lag], acc[:-lag]], axis=0)
                lag *= 2
            count = jnp.minimum(pos + 1, w).astype(_F32)
            pooled = acc[POOL_HALO:] / count - ext[POOL_HALO:]
            mixed = _dot(pooled.astype(_BF16), wpool_ref[g]) * pscale_ref[:, cols]
            mixcat[:, cols] = mixed.astype(_BF16)

        col = lax.broadcasted_iota(jnp.int32, (1, GQA_GROUP * BLOCK), 1)
        sink_rows = []
        for h in range(N_KV_HEADS):
            row = jnp.full((1, GQA_GROUP * BLOCK), sinks_ref[h * GQA_GROUP], _F32)
            for g in range(1, GQA_GROUP):
                row = jnp.where(col >= g * BLOCK, sinks_ref[h * GQA_GROUP + g], row)
            sink_rows.append(row * LOG2E)
        first_variant = jnp.where(seq_tile == 0, 1, 0)

        def logits(j):
            band = slice(j * BLOCK, (j + 2) * BLOCK)
            kb = jnp.concatenate([kcat[0, band, :], kcat[1, band, :]], axis=0)
            s = lax.dot_general(kb, q_stack[j], (((1,), (1,)), ((), ())), preferred_element_type=_F32)
            return s + (bias_t[first_variant] if j == 0 else bias_t[0])

        def attend(j, s):
            band = slice(j * BLOCK, (j + 2) * BLOCK)
            probs, inv_denoms = [], []
            for h in range(N_KV_HEADS):
                sh = s[h * 2 * BLOCK:(h + 1) * 2 * BLOCK, :]
                m = jnp.maximum(jnp.max(sh, axis=0, keepdims=True), sink_rows[h])
                p = jnp.exp2(sh - m)
                denom = jnp.sum(p, axis=0, keepdims=True) + jnp.exp2(sink_rows[h] - m)
                probs.append(p.astype(_BF16))
                inv_denoms.append(1.0 / denom)
            v_both = jnp.concatenate([vt[h, :, band] for h in range(N_KV_HEADS)], axis=1)
            o_t = _dot(v_both, jnp.concatenate(probs, axis=0))
            o_t = jnp.concatenate([o_t[h * HEAD_DIM:(h + 1) * HEAD_DIM] * inv_denoms[h] for h in range(N_KV_HEADS)], axis=0)
            blks = []
            for g in range(GQA_GROUP):
                blks.append(o_t[:, g * BLOCK:(g + 1) * BLOCK].T.astype(_BF16))
            return jnp.concatenate(blks, axis=1)

        half = TILE // N_HALVES
        x1_halves, h2_halves, inv_halves = [], [], []
        for r in range(N_HALVES if with_c else 0):
            rows = slice(r * half, (r + 1) * half)
            x1_r = x_prev[rows, :] + _rmsnorm(mix_buf[rows, :], gpost_ref[...])
            x1_halves.append(x1_r)
            inv_halves.append(lax.rsqrt(jnp.mean(x1_r * x1_r, axis=-1, keepdims=True) + EPS))
            h2_halves.append(x1_r.astype(_BF16))
        next_proj = stage_a_compute(x_next) if with_ab else None
        s_blocks = [logits(j) for j in range(N_BLOCKS)] if with_ab else []
        work = [(r, c) for r in range(N_HALVES) for c in range(N_CHUNKS)]

        def attend_and_store(j):
            mixcat[j * BLOCK:(j + 1) * BLOCK, POOL_WIDTH:] = attend(j, s_blocks[j])

        extras = {}
        if with_ab:
            for j in range(N_BLOCKS):
                extras[ATTEND_SLOTS[j]] = functools.partial(attend_and_store, j)
            for g in range(N_POOL_GROUPS):
                extras[POOL_SLOTS[g]] = functools.partial(pool_group, g)

        def gate_up(k):
            r, c = work[k]
            return ffn_gate_up(h2_halves[r], inv_halves[r], c)

        if not with_c:
            work = []
            for k in sorted(extras):
                extras[k]()
        act_next = gate_up(0) if with_c else None
        acc = None
        for k, (r, c) in enumerate(work):
            act = act_next
            if k + 1 < len(work):
                act_next = gate_up(k + 1)
            part = ffn_down(c, act)
            acc = part if acc is None else acc + part
            if c == N_CHUNKS - 1:
                f = acc * (inv_halves[r] * inv_halves[r])
                y_tile[r * half:(r + 1) * half, :] = x1_halves[r] + _rmsnorm(f, gpost2_ref[...])
                acc = None
            if k in extras:
                extras[k]()
        if with_ab:
            mix_buf[...] = _dot(mixcat[...], wout_ref[...])
            stage_a_store(*next_proj, next_starts_seq)

        @pl.when(step >= 1)
        def _store_y():
            y_copy(step - 1, out_slot).start()

        return carry

    first_step()
    step_body(0, 0, with_c=False)
    lax.fori_loop(1, n_tiles, step_body, 0)
    step_body(n_tiles, 0, with_ab=False)
    y_copy(n_tiles - 2, (n_tiles - 1) % 2).wait()
    y_copy(n_tiles - 1, n_tiles % 2).wait()


def _layer(x, g_pre_mix, w_in, w_pool, pool_scale, rel_bias, sinks, w_out, g_post_mix, g_pre_ffn, w_gate, w_up, w_down,
           g_post_ffn):
    batch, seq, _ = x.shape
    tokens = batch * seq
    n_tiles = tokens // TILE
    tiles_per_seq = seq // TILE
    bucket_t = jnp.asarray(_bucket_table_t())
    x2 = x.reshape(tokens, D_MODEL)
    row = lambda a: a.reshape(1, -1)

    vmem = pl.BlockSpec(memory_space=pltpu.VMEM)
    smem = pl.BlockSpec(memory_space=pltpu.SMEM)
    hbm = pl.BlockSpec(memory_space=pl.ANY)
    out = pl.pallas_call(
        functools.partial(_layer_kernel, tiles_per_seq, n_tiles),
        out_shape=jax.ShapeDtypeStruct(x2.shape, x2.dtype),
        in_specs=[hbm, hbm, vmem, vmem, smem, smem, vmem, hbm, vmem, hbm, hbm, hbm, vmem, vmem, vmem],
        out_specs=hbm,
        scratch_shapes=[
            pltpu.VMEM((POOL_HALO + TILE, POOL_WIDTH), _F32),
            pltpu.VMEM((N_BLOCKS, GQA_GROUP * BLOCK, KV_WIDTH), _BF16),
            pltpu.VMEM((N_KV_HEADS, BLOCK + TILE, KV_WIDTH), _BF16),
            pltpu.VMEM((N_KV_HEADS, KV_WIDTH, BLOCK + TILE), _BF16),
            pltpu.VMEM((TILE, D_MODEL), _BF16),
            pltpu.VMEM((TILE, D_MODEL), _F32),
            pltpu.VMEM((2, N_KV_HEADS * 2 * BLOCK, GQA_GROUP * BLOCK), _F32),
            pltpu.VMEM((D_MODEL, IN_WIDTH), _BF16),
            pltpu.VMEM((N_POOL_GROUPS, POOL_GROUP_DIM, POOL_GROUP_DIM), _BF16),
            pltpu.VMEM((D_MODEL, D_MODEL), _BF16),
            pltpu.VMEM((D_MODEL, D_FF), _BF16),
            pltpu.VMEM((D_MODEL, D_FF), _BF16),
            pltpu.VMEM((D_FF, D_MODEL), _BF16),
            pltpu.VMEM((WEIGHT_STAGE_SLOTS, WEIGHT_STAGE_ROWS, D_FF), _F32),
            pltpu.SemaphoreType.DMA((WEIGHT_STAGE_SLOTS,)),
            pltpu.VMEM((X_RING_SLOTS, TILE, D_MODEL), _F32),
            pltpu.SemaphoreType.DMA((X_RING_SLOTS,)),
            pltpu.VMEM((2, TILE, D_MODEL), _F32),
            pltpu.SemaphoreType.DMA((2,)),
            pltpu.VMEM((2, D_MODEL, BLOCK), _F32),
        ],
        compiler_params=pltpu.CompilerParams(vmem_limit_bytes=VMEM_LIMIT_BYTES),
        name="hybrid_layer",
    )(x2, w_in, w_pool, row(pool_scale), rel_bias.T, sinks, bucket_t, w_out,
      row(g_post_mix), w_gate, w_up, w_down, row(g_post_ffn), row(g_pre_mix), row(g_pre_ffn))
    return out.reshape(x.shape)


def kernel(x, g_pre_mix, w_in, w_pool, pool_scale, rel_bias, sinks, w_out, g_post_mix, g_pre_ffn, w_gate, w_up, w_down, g_post_ffn):
    depth = g_pre_mix.shape[0]
    for l in range(depth):
        x = _layer(x, g_pre_mix[l], w_in[l], w_pool[l], pool_scale[l], rel_bias, sinks[l], w_out[l], g_post_mix[l],
                   g_pre_ffn[l], w_gate[l], w_up[l], w_down[l], g_post_ffn[l])
    return x
```

```python
import functools

import numpy as np
import jax
import jax.numpy as jnp
from jax import lax
from jax.experimental import pallas as pl
from jax.experimental.pallas import tpu as pltpu

D_MODEL = 1024
POOL_WIDTH = 512
POOL_WINDOWS = (2, 4, 8, 16)
N_POOL_GROUPS = len(POOL_WINDOWS)
POOL_GROUP_DIM = POOL_WIDTH // N_POOL_GROUPS
ATTN_WIDTH = 512
HEAD_DIM = 64
N_Q_HEADS = 8
N_KV_HEADS = 2
GQA_GROUP = N_Q_HEADS // N_KV_HEADS
WINDOW = 128
BLOCK = 128
N_BUCKETS = 32
MAX_EXACT = N_BUCKETS // 2
MAX_DISTANCE = 128
KV_WIDTH = N_KV_HEADS * HEAD_DIM
IN_WIDTH = POOL_WIDTH + ATTN_WIDTH + 2 * KV_WIDTH
D_FF = 2816
EPS = 1e-6
NEG_INF = -1e30
LOG2E = 1.4426950408889634

POOL_HALO = 16
TILE = 512
N_BLOCKS = TILE // BLOCK
FFN_CHUNK = 256
N_CHUNKS = D_FF // FFN_CHUNK
N_HALVES = 2
X_RING_SLOTS = 4
ATTEND_SLOTS = (0, 3, 6, 9)
POOL_SLOTS = (12, 14, 16, 18)
WEIGHT_STAGE_ROWS = 128
WEIGHT_STAGE_SLOTS = 4
WOUT_BLOCK_ROWS = HEAD_DIM
VMEM_LIMIT_BYTES = 63 * 1024 * 1024

_F32 = jnp.float32
_BF16 = jnp.bfloat16


def _dot(a, b):
    return jnp.dot(a, b, preferred_element_type=_F32)


def _rmsnorm(xf, g):
    ms = jnp.mean(xf * xf, axis=-1, keepdims=True)
    return xf * lax.rsqrt(ms + EPS) * g


def _bucket_table_t():
    qi = np.arange(BLOCK)[None, :]
    kj = np.arange(2 * BLOCK)[:, None]
    dist = qi + BLOCK - kj
    n = np.maximum(dist, 0)
    nf = np.maximum(n, 1).astype(np.float32)
    large = MAX_EXACT + (
        np.log(nf / np.float32(MAX_EXACT)) / np.float32(np.log(MAX_DISTANCE / MAX_EXACT)) * np.float32(N_BUCKETS - MAX_EXACT)
    ).astype(np.int32)
    large = np.minimum(large, N_BUCKETS - 1)
    bucket = np.where(n < MAX_EXACT, n, large)
    in_window = (dist >= 0) & (dist < WINDOW)
    return np.where(in_window, bucket, -1).astype(np.int32)


def _layer_kernel(tiles_per_seq, n_tiles,
                  x_hbm, win_hbm, wpool_f32_ref, pscale_ref, relb_t_ref, sinks_ref, bkt_ref, wout_hbm,
                  gpost_ref, wg_hbm, wu_hbm, wd_hbm, gpost2_ref, gin_row_ref, gffn_row_ref,
                  y_hbm, u_buf, q_stack, kcat, vt, mixcat, mix_buf, bias_t,
                  win_ref, wpool_ref, wout_ref, wg_ref, wu_ref, wd_ref, stage, stage_sem,
                  x_ring, x_sem, out_buf, out_sem, gain_cols):
    def tile_rows(tile):
        start = tile * TILE
        return pl.ds(start if isinstance(start, int) else pl.multiple_of(start, TILE), TILE)

    def x_copy(tile, slot):
        return pltpu.make_async_copy(x_hbm.at[tile_rows(tile), :], x_ring.at[slot], x_sem.at[slot])

    def y_copy(tile, slot):
        return pltpu.make_async_copy(out_buf.at[slot], y_hbm.at[tile_rows(tile), :], out_sem.at[slot])

    def stage_a_compute(x_ref):
        x = x_ref[...]
        inv_rms = lax.rsqrt(jnp.mean(x * x, axis=-1, keepdims=True) + EPS)
        xb = x.astype(_BF16)
        u = _dot(xb, win_ref[:, 0:POOL_WIDTH]) * inv_rms
        q = (_dot(xb, win_ref[:, POOL_WIDTH:POOL_WIDTH + ATTN_WIDTH]) * (inv_rms * LOG2E)).astype(_BF16)
        kv = _dot(xb, win_ref[:, POOL_WIDTH + ATTN_WIDTH:IN_WIDTH]) * inv_rms
        return u, q, kv

    def stage_a_store(u, q, kv, starts_seq):
        def history(tail):
            if starts_seq is None:
                return jnp.zeros(tail.shape, tail.dtype)
            return jnp.where(starts_seq, jnp.zeros(tail.shape, tail.dtype), tail[...])

        u_buf[0:POOL_HALO, :] = history(u_buf.at[TILE:TILE + POOL_HALO, :])
        u_buf[POOL_HALO:POOL_HALO + TILE, :] = u
        for j in range(N_BLOCKS):
            for g in range(GQA_GROUP):
                q_stack[j, g * BLOCK:(g + 1) * BLOCK, :] = q[j * BLOCK:(j + 1) * BLOCK, g * BLOCK:(g + 1) * BLOCK]
        k = kv[:, 0:KV_WIDTH]
        v = kv[:, KV_WIDTH:2 * KV_WIDTH]
        lane = lax.broadcasted_iota(jnp.int32, k.shape, 1)
        kcat[:, 0:BLOCK, :] = history(kcat.at[:, TILE:TILE + BLOCK, :])
        kcat[0, BLOCK:BLOCK + TILE, :] = jnp.where(lane < HEAD_DIM, k, 0.0).astype(_BF16)
        kcat[1, BLOCK:BLOCK + TILE, :] = jnp.where(lane >= HEAD_DIM, k, 0.0).astype(_BF16)
        v_t = v.T
        row = lax.broadcasted_iota(jnp.int32, v_t.shape, 0)
        vt[:, :, 0:BLOCK] = history(vt.at[:, :, TILE:TILE + BLOCK])
        vt[0, :, BLOCK:BLOCK + TILE] = jnp.where(row < HEAD_DIM, v_t, 0.0).astype(_BF16)
        vt[1, :, BLOCK:BLOCK + TILE] = jnp.where(row >= HEAD_DIM, v_t, 0.0).astype(_BF16)

    def prepare_weights():
        def stream(src, rows_per_chunk, n_chunks, width, src_row, consume, placements):
            per_iter = len(placements)
            n_iter = n_chunks // per_iter

            def copies(i, slot):
                return [pltpu.make_async_copy(src.at[pl.ds(src_row(i * per_iter + k), rows_per_chunk), :],
                                              stage.at[slot, r0:r0 + rows_per_chunk, l0:l0 + width],
                                              stage_sem.at[slot])
                        for k, (r0, l0) in enumerate(placements)]

            def start(i, slot):
                for c in copies(i, slot):
                    c.start()

            for i in range(min(WEIGHT_STAGE_SLOTS - 1, n_iter)):
                start(i, i)

            def body(i, carry):
                slot = lax.rem(i, WEIGHT_STAGE_SLOTS)
                ahead = i + WEIGHT_STAGE_SLOTS - 1

                @pl.when(ahead < n_iter)
                def _prefetch():
                    start(ahead, lax.rem(ahead, WEIGHT_STAGE_SLOTS))

                for c in copies(i, slot):
                    c.wait()
                for k, (r0, l0) in enumerate(placements):
                    consume(i * per_iter + k, stage[slot, r0:r0 + rows_per_chunk, l0:l0 + width])
                return carry

            lax.fori_loop(0, n_iter, body, 0)

        rows = WEIGHT_STAGE_ROWS

        def chunk_rows(i):
            return pl.ds(pl.multiple_of(i * rows, rows), rows)

        for n, gain_row_ref in enumerate((gin_row_ref, gffn_row_ref)):
            gain_cols[n] = jnp.broadcast_to(gain_row_ref[...], (BLOCK, D_MODEL)).T

        def gain_col(n, i):
            return gain_cols[n, chunk_rows(i), 0:1]

        def consume_in(i, blk):
            qb = blk[:, POOL_WIDTH:POOL_WIDTH + ATTN_WIDTH]
            heads = [qb[:, (h * GQA_GROUP + g) * HEAD_DIM:(h * GQA_GROUP + g + 1) * HEAD_DIM]
                     for g in range(GQA_GROUP) for h in range(N_KV_HEADS)]
            q_regrouped = jnp.concatenate(heads, axis=1) * (HEAD_DIM ** -0.5)
            full = jnp.concatenate([blk[:, :POOL_WIDTH], q_regrouped, blk[:, POOL_WIDTH + ATTN_WIDTH:]], axis=1)
            win_ref[chunk_rows(i), :] = (gain_col(0, i) * full).astype(_BF16)

        def consume_gated(dst):
            def consume(i, blk):
                dst[chunk_rows(i), :] = (gain_col(1, i) * blk).astype(_BF16)
            return consume

        def consume_down(i, blk):
            wd_ref[chunk_rows(i), :] = blk.astype(_BF16)

        def wout_src_row(i):
            j = i - POOL_WIDTH // WOUT_BLOCK_ROWS
            head = (j % N_KV_HEADS) * GQA_GROUP + j // N_KV_HEADS
            return jnp.where(j < 0, i, POOL_WIDTH // WOUT_BLOCK_ROWS + head) * WOUT_BLOCK_ROWS

        def consume_out(i, blk):
            dst = pl.ds(pl.multiple_of(i * WOUT_BLOCK_ROWS, WOUT_BLOCK_ROWS), WOUT_BLOCK_ROWS)
            wout_ref[dst, :] = blk.astype(_BF16)

        stream(win_hbm, rows, D_MODEL // rows, IN_WIDTH, lambda c: c * rows, consume_in, [(0, 0), (0, IN_WIDTH)])
        stream(wout_hbm, WOUT_BLOCK_ROWS, D_MODEL // WOUT_BLOCK_ROWS, D_MODEL, wout_src_row, consume_out,
               [(0, 0), (WOUT_BLOCK_ROWS, 0), (0, D_MODEL), (WOUT_BLOCK_ROWS, D_MODEL)])
        stream(wg_hbm, rows, D_MODEL // rows, D_FF, lambda c: c * rows, consume_gated(wg_ref), [(0, 0)])
        stream(wu_hbm, rows, D_MODEL // rows, D_FF, lambda c: c * rows, consume_gated(wu_ref), [(0, 0)])
        stream(wd_hbm, rows, D_FF // rows, D_MODEL, lambda c: c * rows, consume_down, [(0, 0), (0, D_MODEL)])
        wpool_ref[...] = wpool_f32_ref[...].astype(_BF16)

    def first_step():
        x_copy(0, 0).start()
        x_copy(1, 1).start()
        prepare_weights()
        x_copy(0, 0).wait()
        stage_a_store(*stage_a_compute(x_ring.at[0]), None)
        bkt = bkt_ref[...]
        key_is_prev = lax.broadcasted_iota(jnp.int32, bkt.shape, 0) < BLOCK
        for h in range(N_KV_HEADS):
            for g in range(GQA_GROUP):
                head = h * GQA_GROUP + g

                def body(bb, tab):
                    return jnp.where(bkt == bb, relb_t_ref[head, bb], tab)

                tab = lax.fori_loop(0, N_BUCKETS, body, jnp.zeros(bkt.shape, _F32))
                tab = jnp.where(bkt < 0, NEG_INF, tab * LOG2E)
                rows = slice(h * 2 * BLOCK, (h + 1) * 2 * BLOCK)
                cols = slice(g * BLOCK, (g + 1) * BLOCK)
                bias_t[0, rows, cols] = tab
                bias_t[1, rows, cols] = jnp.where(key_is_prev, NEG_INF, tab)

    def step_body(step, carry, with_ab=True, with_c=True):
        seq_tile = lax.rem(jnp.minimum(step, n_tiles - 1), tiles_per_seq)
        next_starts_seq = lax.rem(jnp.minimum(step + 1, n_tiles - 1), tiles_per_seq) == 0
        out_slot = lax.rem(step, 2)

        @pl.when(step + 2 < n_tiles)
        def _prefetch_x():
            x_copy(step + 2, lax.rem(step + 2, X_RING_SLOTS)).start()

        @pl.when(step + 1 < n_tiles)
        def _wait_x():
            x_copy(step + 1, lax.rem(step + 1, X_RING_SLOTS)).wait()

        @pl.when(step >= 3)
        def _wait_y():
            y_copy(step - 3, out_slot).wait()

        x_next = x_ring.at[lax.rem(step + 1, X_RING_SLOTS)]
        x_prev = x_ring.at[lax.rem(jnp.maximum(step - 1, 0), X_RING_SLOTS)]
        y_tile = out_buf.at[out_slot]

        def ffn_gate_up(h2, inv_rms2, c):
            cols = slice(c * FFN_CHUNK, (c + 1) * FFN_CHUNK)
            dg = _dot(h2, wg_ref[:, cols])
            du = _dot(h2, wu_ref[:, cols])
            return (dg * du * (1.0 / (1.0 + jnp.exp2(dg * (inv_rms2 * (-LOG2E)))))).astype(_BF16)

        def ffn_down(c, act):
            return _dot(act, wd_ref[c * FFN_CHUNK:(c + 1) * FFN_CHUNK, :])

        pos = seq_tile * TILE + lax.broadcasted_iota(jnp.int32, (TILE, 1), 0)

        def pool_group(g):
            w = POOL_WINDOWS[g]
            cols = slice(g * POOL_GROUP_DIM, (g + 1) * POOL_GROUP_DIM)
            ext = u_buf[:, cols]
            acc = ext
            lag = 1
            while lag < w:
                acc = acc + jnp.concatenate([acc[:lag], acc[:-lag]], axis=0)
                lag *= 2
            count = jnp.minimum(pos + 1, w).astype(_F32)
            pooled = acc[POOL_HALO:] / count - ext[POOL_HALO:]
            mixed = _dot(pooled.astype(_BF16), wpool_ref[g]) * pscale_ref[:, cols]
            mixcat[:, cols] = mixed.astype(_BF16)

        col = lax.broadcasted_iota(jnp.int32, (1, GQA_GROUP * BLOCK), 1)
        sink_rows = []
        for h in range(N_KV_HEADS):
            row = jnp.full((1, GQA_GROUP * BLOCK), sinks_ref[h * GQA_GROUP], _F32)
            for g in range(1, GQA_GROUP):
                row = jnp.where(col >= g * BLOCK, sinks_ref[h * GQA_GROUP + g], row)
            sink_rows.append(row * LOG2E)
        first_variant = jnp.where(seq_tile == 0, 1, 0)

        def logits(j):
            band = slice(j * BLOCK, (j + 2) * BLOCK)
            kb = jnp.concatenate([kcat[0, band, :], kcat[1, band, :]], axis=0)
            s = lax.dot_general(kb, q_stack[j], (((1,), (1,)), ((), ())), preferred_element_type=_F32)
            return s + (bias_t[first_variant] if j == 0 else bias_t[0])

        def attend(j, s):
            band = slice(j * BLOCK, (j + 2) * BLOCK)
            probs, inv_denoms = [], []
            for h in range(N_KV_HEADS):
                sh = s[h * 2 * BLOCK:(h + 1) * 2 * BLOCK, :]
                m = jnp.maximum(jnp.max(sh, axis=0, keepdims=True), sink_rows[h])
                p = jnp.exp2(sh - m)
                denom = jnp.sum(p, axis=0, keepdims=True) + jnp.exp2(sink_rows[h] - m)
                probs.append(p.astype(_BF16))
                inv_denoms.append(1.0 / denom)
            v_both = jnp.concatenate([vt[h, :, band] for h in range(N_KV_HEADS)], axis=1)
            o_t = _dot(v_both, jnp.concatenate(probs, axis=0))
            o_t = jnp.concatenate([o_t[h * HEAD_DIM:(h + 1) * HEAD_DIM] * inv_denoms[h] for h in range(N_KV_HEADS)], axis=0)
            blks = []
            for g in range(GQA_GROUP):
                blks.append(o_t[:, g * BLOCK:(g + 1) * BLOCK].T.astype(_BF16))
            return jnp.concatenate(blks, axis=1)

        half = TILE // N_HALVES
        x1_halves, h2_halves, inv_halves = [], [], []
        for r in range(N_HALVES if with_c else 0):
            rows = slice(r * half, (r + 1) * half)
            x1_r = x_prev[rows, :] + _rmsnorm(mix_buf[rows, :], gpost_ref[...])
            x1_halves.append(x1_r)
            inv_halves.append(lax.rsqrt(jnp.mean(x1_r * x1_r, axis=-1, keepdims=True) + EPS))
            h2_halves.append(x1_r.astype(_BF16))
        next_proj = stage_a_compute(x_next) if with_ab else None
        s_blocks = [logits(j) for j in range(N_BLOCKS)] if with_ab else []
        work = [(r, c) for r in range(N_HALVES) for c in range(N_CHUNKS)]

        def attend_and_store(j):
            mixcat[j * BLOCK:(j + 1) * BLOCK, POOL_WIDTH:] = attend(j, s_blocks[j])

        extras = {}
        if with_ab:
            for j in range(N_BLOCKS):
                extras[ATTEND_SLOTS[j]] = functools.partial(attend_and_store, j)
            for g in range(N_POOL_GROUPS):
                extras[POOL_SLOTS[g]] = functools.partial(pool_group, g)

        def gate_up(k):
            r, c = work[k]
            return ffn_gate_up(h2_halves[r], inv_halves[r], c)

        if not with_c:
            work = []
            for k in sorted(extras):
                extras[k]()
        act_next = gate_up(0) if with_c else None
        acc = None
        for k, (r, c) in enumerate(work):
            act = act_next
            if k + 1 < len(work):
                act_next = gate_up(k + 1)
            part = ffn_down(c, act)
            acc = part if acc is None else acc + part
            if c == N_CHUNKS - 1:
                f = acc * (inv_halves[r] * inv_halves[r])
                y_tile[r * half:(r + 1) * half, :] = x1_halves[r] + _rmsnorm(f, gpost2_ref[...])
                acc = None
            if k in extras:
                extras[k]()
        if with_ab:
            mix_buf[...] = _dot(mixcat[...], wout_ref[...])
            stage_a_store(*next_proj, next_starts_seq)

        @pl.when(step >= 1)
        def _store_y():
            y_copy(step - 1, out_slot).start()

        return carry

    first_step()
    step_body(0, 0, with_c=False)
    lax.fori_loop(1, n_tiles, step_body, 0)
    step_body(n_tiles, 0, with_ab=False)
    y_copy(n_tiles - 2, (n_tiles - 1) % 2).wait()
    y_copy(n_tiles - 1, n_tiles % 2).wait()


def _layer(x, g_pre_mix, w_in, w_pool, pool_scale, rel_bias, sinks, w_out, g_post_mix, g_pre_ffn, w_gate, w_up, w_down,
           g_post_ffn):
    batch, seq, _ = x.shape
    tokens = batch * seq
    n_tiles = tokens // TILE
    tiles_per_seq = seq // TILE
    bucket_t = jnp.asarray(_bucket_table_t())
    x2 = x.reshape(tokens, D_MODEL)
    row = lambda a: a.reshape(1, -1)

    vmem = pl.BlockSpec(memory_space=pltpu.VMEM)
    smem = pl.BlockSpec(memory_space=pltpu.SMEM)
    hbm = pl.BlockSpec(memory_space=pl.ANY)
    out = pl.pallas_call(
        functools.partial(_layer_kernel, tiles_per_seq, n_tiles),
        out_shape=jax.ShapeDtypeStruct(x2.shape, x2.dtype),
        in_specs=[hbm, hbm, vmem, vmem, smem, smem, vmem, hbm, vmem, hbm, hbm, hbm, vmem, vmem, vmem],
        out_specs=hbm,
        scratch_shapes=[
            pltpu.VMEM((POOL_HALO + TILE, POOL_WIDTH), _F32),
            pltpu.VMEM((N_BLOCKS, GQA_GROUP * BLOCK, KV_WIDTH), _BF16),
            pltpu.VMEM((N_KV_HEADS, BLOCK + TILE, KV_WIDTH), _BF16),
            pltpu.VMEM((N_KV_HEADS, KV_WIDTH, BLOCK + TILE), _BF16),
            pltpu.VMEM((TILE, D_MODEL), _BF16),
            pltpu.VMEM((TILE, D_MODEL), _F32),
            pltpu.VMEM((2, N_KV_HEADS * 2 * BLOCK, GQA_GROUP * BLOCK), _F32),
            pltpu.VMEM((D_MODEL, IN_WIDTH), _BF16),
            pltpu.VMEM((N_POOL_GROUPS, POOL_GROUP_DIM, POOL_GROUP_DIM), _BF16),
            pltpu.VMEM((D_MODEL, D_MODEL), _BF16),
            pltpu.VMEM((D_MODEL, D_FF), _BF16),
            pltpu.VMEM((D_MODEL, D_FF), _BF16),
            pltpu.VMEM((D_FF, D_MODEL), _BF16),
            pltpu.VMEM((WEIGHT_STAGE_SLOTS, WEIGHT_STAGE_ROWS, D_FF), _F32),
            pltpu.SemaphoreType.DMA((WEIGHT_STAGE_SLOTS,)),
            pltpu.VMEM((X_RING_SLOTS, TILE, D_MODEL), _F32),
            pltpu.SemaphoreType.DMA((X_RING_SLOTS,)),
            pltpu.VMEM((2, TILE, D_MODEL), _F32),
            pltpu.SemaphoreType.DMA((2,)),
            pltpu.VMEM((2, D_MODEL, BLOCK), _F32),
        ],
        compiler_params=pltpu.CompilerParams(vmem_limit_bytes=VMEM_LIMIT_BYTES),
        name="hybrid_layer",
    )(x2, w_in, w_pool, row(pool_scale), rel_bias.T, sinks, bucket_t, w_out,
      row(g_post_mix), w_gate, w_up, w_down, row(g_post_ffn), row(g_pre_mix), row(g_pre_ffn))
    return out.reshape(x.shape)


def kernel(x, g_pre_mix, w_in, w_pool, pool_scale, rel_bias, sinks, w_out, g_post_mix, g_pre_ffn, w_gate, w_up, w_down, g_post_ffn):
    depth = g_pre_mix.shape[0]
    for l in range(depth):
        x = _layer(x, g_pre_mix[l], w_in[l], w_pool[l], pool_scale[l], rel_bias, sinks[l], w_out[l], g_post_mix[l],
                   g_pre_ffn[l], w_gate[l], w_up[l], w_down[l], g_post_ffn[l])
    return x
```

```python
import functools

import numpy as np
import jax
import jax.numpy as jnp
from jax import lax
from jax.experimental import pallas as pl
from jax.experimental.pallas import tpu as pltpu

D_MODEL = 1024
POOL_WIDTH = 512
POOL_WINDOWS = (2, 4, 8, 16)
N_POOL_GROUPS = len(POOL_WINDOWS)
POOL_GROUP_DIM = POOL_WIDTH // N_POOL_GROUPS
ATTN_WIDTH = 512
HEAD_DIM = 64
N_Q_HEADS = 8
N_KV_HEADS = 2
GQA_GROUP = N_Q_HEADS // N_KV_HEADS
WINDOW = 128
BLOCK = 128
N_BUCKETS = 32
MAX_EXACT = N_BUCKETS // 2
MAX_DISTANCE = 128
KV_WIDTH = N_KV_HEADS * HEAD_DIM
IN_WIDTH = POOL_WIDTH + ATTN_WIDTH + 2 * KV_WIDTH
D_FF = 2816
EPS = 1e-6
NEG_INF = -1e30
LOG2E = 1.4426950408889634

POOL_HALO = 16
TILE = 512
N_BLOCKS = TILE // BLOCK
FFN_CHUNK = 256
N_CHUNKS = D_FF // FFN_CHUNK
N_HALVES = 2
X_RING_SLOTS = 4
ATTEND_SLOTS = (1, 5, 9, 13)
POOL_SLOTS = (3, 7, 11, 15)
WEIGHT_STAGE_ROWS = 128
WEIGHT_STAGE_SLOTS = 4
WOUT_BLOCK_ROWS = HEAD_DIM
VMEM_LIMIT_BYTES = 63 * 1024 * 1024

_F32 = jnp.float32
_BF16 = jnp.bfloat16


def _dot(a, b):
    return jnp.dot(a, b, preferred_element_type=_F32)


def _rmsnorm(xf, g):
    ms = jnp.mean(xf * xf, axis=-1, keepdims=True)
    return xf * lax.rsqrt(ms + EPS) * g


def _bucket_table_t():
    qi = np.arange(BLOCK)[None, :]
    kj = np.arange(2 * BLOCK)[:, None]
    dist = qi + BLOCK - kj
    n = np.maximum(dist, 0)
    nf = np.maximum(n, 1).astype(np.float32)
    large = MAX_EXACT + (
        np.log(nf / np.float32(MAX_EXACT)) / np.float32(np.log(MAX_DISTANCE / MAX_EXACT)) * np.float32(N_BUCKETS - MAX_EXACT)
    ).astype(np.int32)
    large = np.minimum(large, N_BUCKETS - 1)
    bucket = np.where(n < MAX_EXACT, n, large)
    in_window = (dist >= 0) & (dist < WINDOW)
    return np.where(in_window, bucket, -1).astype(np.int32)


def _layer_kernel(tiles_per_seq, n_tiles,
                  x_hbm, win_hbm, wpool_f32_ref, pscale_ref, relb_t_ref, sinks_ref, bkt_ref, wout_hbm,
                  gpost_ref, wg_hbm, wu_hbm, wd_hbm, gpost2_ref, gin_row_ref, gffn_row_ref,
                  y_hbm, u_buf, q_stack, kcat, vt, mixcat, mix_buf, bias_t,
                  win_ref, wpool_ref, wout_ref, wg_ref, wu_ref, wd_ref, stage, stage_sem,
                  x_ring, x_sem, out_buf, out_sem, gain_cols):
    def tile_rows(tile):
        start = tile * TILE
        return pl.ds(start if isinstance(start, int) else pl.multiple_of(start, TILE), TILE)

    def x_copy(tile, slot):
        return pltpu.make_async_copy(x_hbm.at[tile_rows(tile), :], x_ring.at[slot], x_sem.at[slot])

    def y_copy(tile, slot):
        return pltpu.make_async_copy(out_buf.at[slot], y_hbm.at[tile_rows(tile), :], out_sem.at[slot])

    def stage_a_compute(x_ref):
        x = x_ref[...]
        inv_rms = lax.rsqrt(jnp.mean(x * x, axis=-1, keepdims=True) + EPS)
        xb = x.astype(_BF16)
        u = _dot(xb, win_ref[:, 0:POOL_WIDTH]) * inv_rms
        q = (_dot(xb, win_ref[:, POOL_WIDTH:POOL_WIDTH + ATTN_WIDTH]) * (inv_rms * LOG2E)).astype(_BF16)
        kv = _dot(xb, win_ref[:, POOL_WIDTH + ATTN_WIDTH:IN_WIDTH]) * inv_rms
        return u, q, kv

    def stage_a_store(u, q, kv, starts_seq):
        def history(tail):
            if starts_seq is None:
                return jnp.zeros(tail.shape, tail.dtype)
            return jnp.where(starts_seq, jnp.zeros(tail.shape, tail.dtype), tail[...])

        u_buf[0:POOL_HALO, :] = history(u_buf.at[TILE:TILE + POOL_HALO, :])
        u_buf[POOL_HALO:POOL_HALO + TILE, :] = u
        for j in range(N_BLOCKS):
            for g in range(GQA_GROUP):
                q_stack[j, g * BLOCK:(g + 1) * BLOCK, :] = q[j * BLOCK:(j + 1) * BLOCK, g * BLOCK:(g + 1) * BLOCK]
        k = kv[:, 0:KV_WIDTH]
        v = kv[:, KV_WIDTH:2 * KV_WIDTH]
        lane = lax.broadcasted_iota(jnp.int32, k.shape, 1)
        kcat[:, 0:BLOCK, :] = history(kcat.at[:, TILE:TILE + BLOCK, :])
        kcat[0, BLOCK:BLOCK + TILE, :] = jnp.where(lane < HEAD_DIM, k, 0.0).astype(_BF16)
        kcat[1, BLOCK:BLOCK + TILE, :] = jnp.where(lane >= HEAD_DIM, k, 0.0).astype(_BF16)
        v_t = v.T
        row = lax.broadcasted_iota(jnp.int32, v_t.shape, 0)
        vt[:, :, 0:BLOCK] = history(vt.at[:, :, TILE:TILE + BLOCK])
        vt[0, :, BLOCK:BLOCK + TILE] = jnp.where(row < HEAD_DIM, v_t, 0.0).astype(_BF16)
        vt[1, :, BLOCK:BLOCK + TILE] = jnp.where(row >= HEAD_DIM, v_t, 0.0).astype(_BF16)

    def prepare_weights():
        def stream(src, rows_per_chunk, n_chunks, width, src_row, consume, placements):
            per_iter = len(placements)
            n_iter = n_chunks // per_iter

            def copies(i, slot):
                return [pltpu.make_async_copy(src.at[pl.ds(src_row(i * per_iter + k), rows_per_chunk), :],
                                              stage.at[slot, r0:r0 + rows_per_chunk, l0:l0 + width],
                                              stage_sem.at[slot])
                        for k, (r0, l0) in enumerate(placements)]

            def start(i, slot):
                for c in copies(i, slot):
                    c.start()

            for i in range(min(WEIGHT_STAGE_SLOTS - 1, n_iter)):
                start(i, i)

            def body(i, carry):
                slot = lax.rem(i, WEIGHT_STAGE_SLOTS)
                ahead = i + WEIGHT_STAGE_SLOTS - 1

                @pl.when(ahead < n_iter)
                def _prefetch():
                    start(ahead, lax.rem(ahead, WEIGHT_STAGE_SLOTS))

                for c in copies(i, slot):
                    c.wait()
                for k, (r0, l0) in enumerate(placements):
                    consume(i * per_iter + k, stage[slot, r0:r0 + rows_per_chunk, l0:l0 + width])
                return carry

            lax.fori_loop(0, n_iter, body, 0)

        rows = WEIGHT_STAGE_ROWS

        def chunk_rows(i):
            return pl.ds(pl.multiple_of(i * rows, rows), rows)

        for n, gain_row_ref in enumerate((gin_row_ref, gffn_row_ref)):
            gain_cols[n] = jnp.broadcast_to(gain_row_ref[...], (BLOCK, D_MODEL)).T

        def gain_col(n, i):
            return gain_cols[n, chunk_rows(i), 0:1]

        def consume_in(i, blk):
            qb = blk[:, POOL_WIDTH:POOL_WIDTH + ATTN_WIDTH]
            heads = [qb[:, (h * GQA_GROUP + g) * HEAD_DIM:(h * GQA_GROUP + g + 1) * HEAD_DIM]
                     for g in range(GQA_GROUP) for h in range(N_KV_HEADS)]
            q_regrouped = jnp.concatenate(heads, axis=1) * (HEAD_DIM ** -0.5)
            full = jnp.concatenate([blk[:, :POOL_WIDTH], q_regrouped, blk[:, POOL_WIDTH + ATTN_WIDTH:]], axis=1)
            win_ref[chunk_rows(i), :] = (gain_col(0, i) * full).astype(_BF16)

        def consume_gated(dst):
            def consume(i, blk):
                dst[chunk_rows(i), :] = (gain_col(1, i) * blk).astype(_BF16)
            return consume

        def consume_down(i, blk):
            wd_ref[chunk_rows(i), :] = blk.astype(_BF16)

        def wout_src_row(i):
            j = i - POOL_WIDTH // WOUT_BLOCK_ROWS
            head = (j % N_KV_HEADS) * GQA_GROUP + j // N_KV_HEADS
            return jnp.where(j < 0, i, POOL_WIDTH // WOUT_BLOCK_ROWS + head) * WOUT_BLOCK_ROWS

        def consume_out(i, blk):
            dst = pl.ds(pl.multiple_of(i * WOUT_BLOCK_ROWS, WOUT_BLOCK_ROWS), WOUT_BLOCK_ROWS)
            wout_ref[dst, :] = blk.astype(_BF16)

        stream(win_hbm, rows, D_MODEL // rows, IN_WIDTH, lambda c: c * rows, consume_in, [(0, 0), (0, IN_WIDTH)])
        stream(wout_hbm, WOUT_BLOCK_ROWS, D_MODEL // WOUT_BLOCK_ROWS, D_MODEL, wout_src_row, consume_out,
               [(0, 0), (WOUT_BLOCK_ROWS, 0), (0, D_MODEL), (WOUT_BLOCK_ROWS, D_MODEL)])
        stream(wg_hbm, rows, D_MODEL // rows, D_FF, lambda c: c * rows, consume_gated(wg_ref), [(0, 0)])
        stream(wu_hbm, rows, D_MODEL // rows, D_FF, lambda c: c * rows, consume_gated(wu_ref), [(0, 0)])
        stream(wd_hbm, rows, D_FF // rows, D_MODEL, lambda c: c * rows, consume_down, [(0, 0), (0, D_MODEL)])
        wpool_ref[...] = wpool_f32_ref[...].astype(_BF16)

    def first_step():
        x_copy(0, 0).start()
        x_copy(1, 1).start()
        prepare_weights()
        x_copy(0, 0).wait()
        stage_a_store(*stage_a_compute(x_ring.at[0]), None)
        bkt = bkt_ref[...]
        key_is_prev = lax.broadcasted_iota(jnp.int32, bkt.shape, 0) < BLOCK
        for h in range(N_KV_HEADS):
            for g in range(GQA_GROUP):
                head = h * GQA_GROUP + g

                def body(bb, tab):
                    return jnp.where(bkt == bb, relb_t_ref[head, bb], tab)

                tab = lax.fori_loop(0, N_BUCKETS, body, jnp.zeros(bkt.shape, _F32))
                tab = jnp.where(bkt < 0, NEG_INF, tab * LOG2E)
                rows = slice(h * 2 * BLOCK, (h + 1) * 2 * BLOCK)
                cols = slice(g * BLOCK, (g + 1) * BLOCK)
                bias_t[0, rows, cols] = tab
                bias_t[1, rows, cols] = jnp.where(key_is_prev, NEG_INF, tab)

    def step_body(step, carry, with_ab=True, with_c=True):
        seq_tile = lax.rem(jnp.minimum(step, n_tiles - 1), tiles_per_seq)
        next_starts_seq = lax.rem(jnp.minimum(step + 1, n_tiles - 1), tiles_per_seq) == 0
        out_slot = lax.rem(step, 2)

        @pl.when(step + 2 < n_tiles)
        def _prefetch_x():
            x_copy(step + 2, lax.rem(step + 2, X_RING_SLOTS)).start()

        @pl.when(step + 1 < n_tiles)
        def _wait_x():
            x_copy(step + 1, lax.rem(step + 1, X_RING_SLOTS)).wait()

        @pl.when(step >= 3)
        def _wait_y():
            y_copy(step - 3, out_slot).wait()

        x_next = x_ring.at[lax.rem(step + 1, X_RING_SLOTS)]
        x_prev = x_ring.at[lax.rem(jnp.maximum(step - 1, 0), X_RING_SLOTS)]
        y_tile = out_buf.at[out_slot]

        def ffn_gate_up(h2, inv_rms2, c):
            cols = slice(c * FFN_CHUNK, (c + 1) * FFN_CHUNK)
            dg = _dot(h2, wg_ref[:, cols])
            du = _dot(h2, wu_ref[:, cols])
            return (dg * du * (1.0 / (1.0 + jnp.exp2(dg * (inv_rms2 * (-LOG2E)))))).astype(_BF16)

        def ffn_down(c, act):
            return _dot(act, wd_ref[c * FFN_CHUNK:(c + 1) * FFN_CHUNK, :])

        pos = seq_tile * TILE + lax.broadcasted_iota(jnp.int32, (TILE, 1), 0)

        def pool_group(g):
            w = POOL_WINDOWS[g]
            cols = slice(g * POOL_GROUP_DIM, (g + 1) * POOL_GROUP_DIM)
            ext = u_buf[:, cols]
            acc = ext
            lag = 1
            while lag < w:
                acc = acc + jnp.concatenate([acc[:lag], acc[:-lag]], axis=0)
                lag *= 2
            count = jnp.minimum(pos + 1, w).astype(_F32)
            pooled = acc[POOL_HALO:] / count - ext[POOL_HALO:]
            mixed = _dot(pooled.astype(_BF16), wpool_ref[g]) * pscale_ref[:, cols]
            mixcat[:, cols] = mixed.astype(_BF16)

        col = lax.broadcasted_iota(jnp.int32, (1, GQA_GROUP * BLOCK), 1)
        sink_rows = []
        for h in range(N_KV_HEADS):
            row = jnp.full((1, GQA_GROUP * BLOCK), sinks_ref[h * GQA_GROUP], _F32)
            for g in range(1, GQA_GROUP):
                row = jnp.where(col >= g * BLOCK, sinks_ref[h * GQA_GROUP + g], row)
            sink_rows.append(row * LOG2E)
        first_variant = jnp.where(seq_tile == 0, 1, 0)

        def logits(j):
            band = slice(j * BLOCK, (j + 2) * BLOCK)
            kb = jnp.concatenate([kcat[0, band, :], kcat[1, band, :]], axis=0)
            s = lax.dot_general(kb, q_stack[j], (((1,), (1,)), ((), ())), preferred_element_type=_F32)
            return s + (bias_t[first_variant] if j == 0 else bias_t[0])

        def attend(j, s):
            band = slice(j * BLOCK, (j + 2) * BLOCK)
            probs, inv_denoms = [], []
            for h in range(N_KV_HEADS):
                sh = s[h * 2 * BLOCK:(h + 1) * 2 * BLOCK, :]
                m = jnp.maximum(jnp.max(sh, axis=0, keepdims=True), sink_rows[h])
                p = jnp.exp2(sh - m)
                denom = jnp.sum(p, axis=0, keepdims=True) + jnp.exp2(sink_rows[h] - m)
                probs.append(p.astype(_BF16))
                inv_denoms.append(1.0 / denom)
            v_both = jnp.concatenate([vt[h, :, band] for h in range(N_KV_HEADS)], axis=1)
            o_t = _dot(v_both, jnp.concatenate(probs, axis=0))
            o_t = jnp.concatenate([o_t[h * HEAD_DIM:(h + 1) * HEAD_DIM] * inv_denoms[h] for h in range(N_KV_HEADS)], axis=0)
            blks = []
            for g in range(GQA_GROUP):
                blks.append(o_t[:, g * BLOCK:(g + 1) * BLOCK].T.astype(_BF16))
            return jnp.concatenate(blks, axis=1)

        half = TILE // N_HALVES
        x1_halves, h2_halves, inv_halves = [], [], []
        for r in range(N_HALVES if with_c else 0):
            rows = slice(r * half, (r + 1) * half)
            x1_r = x_prev[rows, :] + _rmsnorm(mix_buf[rows, :], gpost_ref[...])
            x1_halves.append(x1_r)
            inv_halves.append(lax.rsqrt(jnp.mean(x1_r * x1_r, axis=-1, keepdims=True) + EPS))
            h2_halves.append(x1_r.astype(_BF16))
        next_proj = stage_a_compute(x_next) if with_ab else None
        s_blocks = [logits(j) for j in range(N_BLOCKS)] if with_ab else []
        work = [(r, c) for r in range(N_HALVES) for c in range(N_CHUNKS)]

        def attend_and_store(j):
            mixcat[j * BLOCK:(j + 1) * BLOCK, POOL_WIDTH:] = attend(j, s_blocks[j])

        extras = {}
        if with_ab:
            for j in range(N_BLOCKS):
                extras[ATTEND_SLOTS[j]] = functools.partial(attend_and_store, j)
            for g in range(N_POOL_GROUPS):
                extras[POOL_SLOTS[g]] = functools.partial(pool_group, g)

        def gate_up(k):
            r, c = work[k]
            return ffn_gate_up(h2_halves[r], inv_halves[r], c)

        if not with_c:
            work = []
            for k in sorted(extras):
                extras[k]()
        act_next = gate_up(0) if with_c else None
        acc = None
        for k, (r, c) in enumerate(work):
            act = act_next
            if k + 1 < len(work):
                act_next = gate_up(k + 1)
            part = ffn_down(c, act)
            acc = part if acc is None else acc + part
            if c == N_CHUNKS - 1:
                f = acc * (inv_halves[r] * inv_halves[r])
                y_tile[r * half:(r + 1) * half, :] = x1_halves[r] + _rmsnorm(f, gpost2_ref[...])
                acc = None
            if k in extras:
                extras[k]()
        if with_ab:
            mix_buf[...] = _dot(mixcat[...], wout_ref[...])
            stage_a_store(*next_proj, next_starts_seq)

        @pl.when(step >= 1)
        def _store_y():
            y_copy(step - 1, out_slot).start()

        return carry

    first_step()
    step_body(0, 0, with_c=False)
    lax.fori_loop(1, n_tiles, step_body, 0)
    step_body(n_tiles, 0, with_ab=False)
    y_copy(n_tiles - 2, (n_tiles - 1) % 2).wait()
    y_copy(n_tiles - 1, n_tiles % 2).wait()


def _layer(x, g_pre_mix, w_in, w_pool, pool_scale, rel_bias, sinks, w_out, g_post_mix, g_pre_ffn, w_gate, w_up, w_down,
           g_post_ffn):
    batch, seq, _ = x.shape
    tokens = batch * seq
    n_tiles = tokens // TILE
    tiles_per_seq = seq // TILE
    bucket_t = jnp.asarray(_bucket_table_t())
    x2 = x.reshape(tokens, D_MODEL)
    row = lambda a: a.reshape(1, -1)

    vmem = pl.BlockSpec(memory_space=pltpu.VMEM)
    smem = pl.BlockSpec(memory_space=pltpu.SMEM)
    hbm = pl.BlockSpec(memory_space=pl.ANY)
    out = pl.pallas_call(
        functools.partial(_layer_kernel, tiles_per_seq, n_tiles),
        out_shape=jax.ShapeDtypeStruct(x2.shape, x2.dtype),
        in_specs=[hbm, hbm, vmem, vmem, smem, smem, vmem, hbm, vmem, hbm, hbm, hbm, vmem, vmem, vmem],
        out_specs=hbm,
        scratch_shapes=[
            pltpu.VMEM((POOL_HALO + TILE, POOL_WIDTH), _F32),
            pltpu.VMEM((N_BLOCKS, GQA_GROUP * BLOCK, KV_WIDTH), _BF16),
            pltpu.VMEM((N_KV_HEADS, BLOCK + TILE, KV_WIDTH), _BF16),
            pltpu.VMEM((N_KV_HEADS, KV_WIDTH, BLOCK + TILE), _BF16),
            pltpu.VMEM((TILE, D_MODEL), _BF16),
            pltpu.VMEM((TILE, D_MODEL), _F32),
            pltpu.VMEM((2, N_KV_HEADS * 2 * BLOCK, GQA_GROUP * BLOCK), _F32),
            pltpu.VMEM((D_MODEL, IN_WIDTH), _BF16),
            pltpu.VMEM((N_POOL_GROUPS, POOL_GROUP_DIM, POOL_GROUP_DIM), _BF16),
            pltpu.VMEM((D_MODEL, D_MODEL), _BF16),
            pltpu.VMEM((D_MODEL, D_FF), _BF16),
            pltpu.VMEM((D_MODEL, D_FF), _BF16),
            pltpu.VMEM((D_FF, D_MODEL), _BF16),
            pltpu.VMEM((WEIGHT_STAGE_SLOTS, WEIGHT_STAGE_ROWS, D_FF), _F32),
            pltpu.SemaphoreType.DMA((WEIGHT_STAGE_SLOTS,)),
            pltpu.VMEM((X_RING_SLOTS, TILE, D_MODEL), _F32),
            pltpu.SemaphoreType.DMA((X_RING_SLOTS,)),
            pltpu.VMEM((2, TILE, D_MODEL), _F32),
            pltpu.SemaphoreType.DMA((2,)),
            pltpu.VMEM((2, D_MODEL, BLOCK), _F32),
        ],
        compiler_params=pltpu.CompilerParams(vmem_limit_bytes=VMEM_LIMIT_BYTES),
        name="hybrid_layer",
    )(x2, w_in, w_pool, row(pool_scale), rel_bias.T, sinks, bucket_t, w_out,
      row(g_post_mix), w_gate, w_up, w_down, row(g_post_ffn), row(g_pre_mix), row(g_pre_ffn))
    return out.reshape(x.shape)


def kernel(x, g_pre_mix, w_in, w_pool, pool_scale, rel_bias, sinks, w_out, g_post_mix, g_pre_ffn, w_gate, w_up, w_down, g_post_ffn):
    depth = g_pre_mix.shape[0]
    for l in range(depth):
        x = _layer(x, g_pre_mix[l], w_in[l], w_pool[l], pool_scale[l], rel_bias, sinks[l], w_out[l], g_post_mix[l],
                   g_pre_ffn[l], w_gate[l], w_up[l], w_down[l], g_post_ffn[l])
    return x
```

```python
import functools

import numpy as np
import jax
import jax.numpy as jnp
from jax import lax
from jax.experimental import pallas as pl
from jax.experimental.pallas import tpu as pltpu

D_MODEL = 1024
POOL_WIDTH = 512
POOL_WINDOWS = (2, 4, 8, 16)
N_POOL_GROUPS = len(POOL_WINDOWS)
POOL_GROUP_DIM = POOL_WIDTH // N_POOL_GROUPS
ATTN_WIDTH = 512
HEAD_DIM = 64
N_Q_HEADS = 8
N_KV_HEADS = 2
GQA_GROUP = N_Q_HEADS // N_KV_HEADS
WINDOW = 128
BLOCK = 128
N_BUCKETS = 32
MAX_EXACT = N_BUCKETS // 2
MAX_DISTANCE = 128
KV_WIDTH = N_KV_HEADS * HEAD_DIM
IN_WIDTH = POOL_WIDTH + ATTN_WIDTH + 2 * KV_WIDTH
D_FF = 2816
EPS = 1e-6
NEG_INF = -1e30
LOG2E = 1.4426950408889634

POOL_HALO = 16
TILE = 512
N_BLOCKS = TILE // BLOCK
FFN_CHUNK = 256
N_CHUNKS = D_FF // FFN_CHUNK
N_HALVES = 2
X_RING_SLOTS = 4
ATTEND_SLOTS = (0, 3, 6, 9)
POOL_SLOTS = (12, 14, 16, 18)
WEIGHT_STAGE_ROWS = 128
WEIGHT_STAGE_SLOTS = 3
WOUT_BLOCK_ROWS = HEAD_DIM
VMEM_LIMIT_BYTES = 63 * 1024 * 1024

_F32 = jnp.float32
_BF16 = jnp.bfloat16


def _dot(a, b):
    return jnp.dot(a, b, preferred_element_type=_F32)


def _rmsnorm(xf, g):
    ms = jnp.mean(xf * xf, axis=-1, keepdims=True)
    return xf * lax.rsqrt(ms + EPS) * g


def _bucket_table_t():
    qi = np.arange(BLOCK)[None, :]
    kj = np.arange(2 * BLOCK)[:, None]
    dist = qi + BLOCK - kj
    n = np.maximum(dist, 0)
    nf = np.maximum(n, 1).astype(np.float32)
    large = MAX_EXACT + (
        np.log(nf / np.float32(MAX_EXACT)) / np.float32(np.log(MAX_DISTANCE / MAX_EXACT)) * np.float32(N_BUCKETS - MAX_EXACT)
    ).astype(np.int32)
    large = np.minimum(large, N_BUCKETS - 1)
    bucket = np.where(n < MAX_EXACT, n, large)
    in_window = (dist >= 0) & (dist < WINDOW)
    return np.where(in_window, bucket, -1).astype(np.int32)


def _layer_kernel(tiles_per_seq, n_tiles,
                  x_hbm, win_hbm, wpool_f32_ref, pscale_ref, relb_t_ref, sinks_ref, bkt_ref, wout_hbm,
                  gpost_ref, wg_hbm, wu_hbm, wd_hbm, gpost2_ref, gin_row_ref, gffn_row_ref,
                  y_hbm, u_buf, q_stack, kcat, vt, mixcat, mix_buf, bias_t,
                  win_ref, wpool_ref, wout_ref, wg_ref, wu_ref, wd_ref, stage, stage_sem,
                  x_ring, x_sem, out_buf, out_sem, gain_cols):
    def tile_rows(tile):
        start = tile * TILE
        return pl.ds(start if isinstance(start, int) else pl.multiple_of(start, TILE), TILE)

    def x_copy(tile, slot):
        return pltpu.make_async_copy(x_hbm.at[tile_rows(tile), :], x_ring.at[slot], x_sem.at[slot])

    def y_copy(tile, slot):
        return pltpu.make_async_copy(out_buf.at[slot], y_hbm.at[tile_rows(tile), :], out_sem.at[slot])

    def stage_a_compute(x_ref):
        x = x_ref[...]
        inv_rms = lax.rsqrt(jnp.mean(x * x, axis=-1, keepdims=True) + EPS)
        xb = x.astype(_BF16)
        u = _dot(xb, win_ref[:, 0:POOL_WIDTH]) * inv_rms
        q = (_dot(xb, win_ref[:, POOL_WIDTH:POOL_WIDTH + ATTN_WIDTH]) * (inv_rms * LOG2E)).astype(_BF16)
        kv = _dot(xb, win_ref[:, POOL_WIDTH + ATTN_WIDTH:IN_WIDTH]) * inv_rms
        return u, q, kv

    def stage_a_store(u, q, kv, starts_seq):
        def history(tail):
            if starts_seq is None:
                return jnp.zeros(tail.shape, tail.dtype)
            return jnp.where(starts_seq, jnp.zeros(tail.shape, tail.dtype), tail[...])

        u_buf[0:POOL_HALO, :] = history(u_buf.at[TILE:TILE + POOL_HALO, :])
        u_buf[POOL_HALO:POOL_HALO + TILE, :] = u
        for j in range(N_BLOCKS):
            for g in range(GQA_GROUP):
                q_stack[j, g * BLOCK:(g + 1) * BLOCK, :] = q[j * BLOCK:(j + 1) * BLOCK, g * BLOCK:(g + 1) * BLOCK]
        k = kv[:, 0:KV_WIDTH]
        v = kv[:, KV_WIDTH:2 * KV_WIDTH]
        lane = lax.broadcasted_iota(jnp.int32, k.shape, 1)
        kcat[:, 0:BLOCK, :] = history(kcat.at[:, TILE:TILE + BLOCK, :])
        kcat[0, BLOCK:BLOCK + TILE, :] = jnp.where(lane < HEAD_DIM, k, 0.0).astype(_BF16)
        kcat[1, BLOCK:BLOCK + TILE, :] = jnp.where(lane >= HEAD_DIM, k, 0.0).astype(_BF16)
        v_t = v.T
        row = lax.broadcasted_iota(jnp.int32, v_t.shape, 0)
        vt[:, :, 0:BLOCK] = history(vt.at[:, :, TILE:TILE + BLOCK])
        vt[0, :, BLOCK:BLOCK + TILE] = jnp.where(row < HEAD_DIM, v_t, 0.0).astype(_BF16)
        vt[1, :, BLOCK:BLOCK + TILE] = jnp.where(row >= HEAD_DIM, v_t, 0.0).astype(_BF16)

    def prepare_weights():
        def stream(src, rows_per_chunk, n_chunks, width, src_row, consume, placements):
            per_iter = len(placements)
            n_iter = n_chunks // per_iter

            def copies(i, slot):
                return [pltpu.make_async_copy(src.at[pl.ds(src_row(i * per_iter + k), rows_per_chunk), :],
                                              stage.at[slot, r0:r0 + rows_per_chunk, l0:l0 + width],
                                              stage_sem.at[slot])
                        for k, (r0, l0) in enumerate(placements)]

            def start(i, slot):
                for c in copies(i, slot):
                    c.start()

            for i in range(min(WEIGHT_STAGE_SLOTS - 1, n_iter)):
                start(i, i)

            def body(i, carry):
                slot = lax.rem(i, WEIGHT_STAGE_SLOTS)
                ahead = i + WEIGHT_STAGE_SLOTS - 1

                @pl.when(ahead < n_iter)
                def _prefetch():
                    start(ahead, lax.rem(ahead, WEIGHT_STAGE_SLOTS))

                for c in copies(i, slot):
                    c.wait()
                for k, (r0, l0) in enumerate(placements):
                    consume(i * per_iter + k, stage[slot, r0:r0 + rows_per_chunk, l0:l0 + width])
                return carry

            lax.fori_loop(0, n_iter, body, 0)

        rows = WEIGHT_STAGE_ROWS

        def chunk_rows(i):
            return pl.ds(pl.multiple_of(i * rows, rows), rows)

        for n, gain_row_ref in enumerate((gin_row_ref, gffn_row_ref)):
            gain_cols[n] = jnp.broadcast_to(gain_row_ref[...], (BLOCK, D_MODEL)).T

        def gain_col(n, i):
            return gain_cols[n, chunk_rows(i), 0:1]

        def consume_in(i, blk):
            qb = blk[:, POOL_WIDTH:POOL_WIDTH + ATTN_WIDTH]
            heads = [qb[:, (h * GQA_GROUP + g) * HEAD_DIM:(h * GQA_GROUP + g + 1) * HEAD_DIM]
                     for g in range(GQA_GROUP) for h in range(N_KV_HEADS)]
            q_regrouped = jnp.concatenate(heads, axis=1) * (HEAD_DIM ** -0.5)
            full = jnp.concatenate([blk[:, :POOL_WIDTH], q_regrouped, blk[:, POOL_WIDTH + ATTN_WIDTH:]], axis=1)
            win_ref[chunk_rows(i), :] = (gain_col(0, i) * full).astype(_BF16)

        def consume_gated(dst):
            def consume(i, blk):
                dst[chunk_rows(i), :] = (gain_col(1, i) * blk).astype(_BF16)
            return consume

        def consume_down(i, blk):
            wd_ref[chunk_rows(i), :] = blk.astype(_BF16)

        def wout_src_row(i):
            j = i - POOL_WIDTH // WOUT_BLOCK_ROWS
            head = (j % N_KV_HEADS) * GQA_GROUP + j // N_KV_HEADS
            return jnp.where(j < 0, i, POOL_WIDTH // WOUT_BLOCK_ROWS + head) * WOUT_BLOCK_ROWS

        def consume_out(i, blk):
            dst = pl.ds(pl.multiple_of(i * WOUT_BLOCK_ROWS, WOUT_BLOCK_ROWS), WOUT_BLOCK_ROWS)
            wout_ref[dst, :] = blk.astype(_BF16)

        stream(win_hbm, rows, D_MODEL // rows, IN_WIDTH, lambda c: c * rows, consume_in, [(0, 0), (0, IN_WIDTH)])
        stream(wout_hbm, WOUT_BLOCK_ROWS, D_MODEL // WOUT_BLOCK_ROWS, D_MODEL, wout_src_row, consume_out,
               [(0, 0), (WOUT_BLOCK_ROWS, 0), (0, D_MODEL), (WOUT_BLOCK_ROWS, D_MODEL)])
        stream(wg_hbm, rows, D_MODEL // rows, D_FF, lambda c: c * rows, consume_gated(wg_ref), [(0, 0)])
        stream(wu_hbm, rows, D_MODEL // rows, D_FF, lambda c: c * rows, consume_gated(wu_ref), [(0, 0)])
        stream(wd_hbm, rows, D_FF // rows, D_MODEL, lambda c: c * rows, consume_down, [(0, 0), (0, D_MODEL)])
        wpool_ref[...] = wpool_f32_ref[...].astype(_BF16)

    def first_step():
        x_copy(0, 0).start()
        x_copy(1, 1).start()
        prepare_weights()
        x_copy(0, 0).wait()
        stage_a_store(*stage_a_compute(x_ring.at[0]), None)
        bkt = bkt_ref[...]
        key_is_prev = lax.broadcasted_iota(jnp.int32, bkt.shape, 0) < BLOCK
        for h in range(N_KV_HEADS):
            for g in range(GQA_GROUP):
                head = h * GQA_GROUP + g

                def body(bb, tab):
                    return jnp.where(bkt == bb, relb_t_ref[head, bb], tab)

                tab = lax.fori_loop(0, N_BUCKETS, body, jnp.zeros(bkt.shape, _F32))
                tab = jnp.where(bkt < 0, NEG_INF, tab * LOG2E)
                rows = slice(h * 2 * BLOCK, (h + 1) * 2 * BLOCK)
                cols = slice(g * BLOCK, (g + 1) * BLOCK)
                bias_t[0, rows, cols] = tab
                bias_t[1, rows, cols] = jnp.where(key_is_prev, NEG_INF, tab)

    def step_body(step, carry, with_ab=True, with_c=True):
        seq_tile = lax.rem(jnp.minimum(step, n_tiles - 1), tiles_per_seq)
        next_starts_seq = lax.rem(jnp.minimum(step + 1, n_tiles - 1), tiles_per_seq) == 0
        out_slot = lax.rem(step, 2)

        @pl.when(step + 2 < n_tiles)
        def _prefetch_x():
            x_copy(step + 2, lax.rem(step + 2, X_RING_SLOTS)).start()

        @pl.when(step + 1 < n_tiles)
        def _wait_x():
            x_copy(step + 1, lax.rem(step + 1, X_RING_SLOTS)).wait()

        @pl.when(step >= 3)
        def _wait_y():
            y_copy(step - 3, out_slot).wait()

        x_next = x_ring.at[lax.rem(step + 1, X_RING_SLOTS)]
        x_prev = x_ring.at[lax.rem(jnp.maximum(step - 1, 0), X_RING_SLOTS)]
        y_tile = out_buf.at[out_slot]

        def ffn_gate_up(h2, inv_rms2, c):
            cols = slice(c * FFN_CHUNK, (c + 1) * FFN_CHUNK)
            dg = _dot(h2, wg_ref[:, cols])
            du = _dot(h2, wu_ref[:, cols])
            return (dg * du * (1.0 / (1.0 + jnp.exp2(dg * (inv_rms2 * (-LOG2E)))))).astype(_BF16)

        def ffn_down(c, act):
            return _dot(act, wd_ref[c * FFN_CHUNK:(c + 1) * FFN_CHUNK, :])

        pos = seq_tile * TILE + lax.broadcasted_iota(jnp.int32, (TILE, 1), 0)

        def pool_group(g):
            w = POOL_WINDOWS[g]
            cols = slice(g * POOL_GROUP_DIM, (g + 1) * POOL_GROUP_DIM)
            ext = u_buf[:, cols]
            acc = ext
            lag = 1
            while lag < w:
                acc = acc + jnp.concatenate([acc[:lag], acc[:-lag]], axis=0)
                lag *= 2
            count = jnp.minimum(pos + 1, w).astype(_F32)
            pooled = acc[POOL_HALO:] / count - ext[POOL_HALO:]
            mixed = _dot(pooled.astype(_BF16), wpool_ref[g]) * pscale_ref[:, cols]
            mixcat[:, cols] = mixed.astype(_BF16)

        col = lax.broadcasted_iota(jnp.int32, (1, GQA_GROUP * BLOCK), 1)
        sink_rows = []
        for h in range(N_KV_HEADS):
            row = jnp.full((1, GQA_GROUP * BLOCK), sinks_ref[h * GQA_GROUP], _F32)
            for g in range(1, GQA_GROUP):
                row = jnp.where(col >= g * BLOCK, sinks_ref[h * GQA_GROUP + g], row)
            sink_rows.append(row * LOG2E)
        first_variant = jnp.where(seq_tile == 0, 1, 0)

        def logits(j):
            band = slice(j * BLOCK, (j + 2) * BLOCK)
            kb = jnp.concatenate([kcat[0, band, :], kcat[1, band, :]], axis=0)
            s = lax.dot_general(kb, q_stack[j], (((1,), (1,)), ((), ())), preferred_element_type=_F32)
            return s + (bias_t[first_variant] if j == 0 else bias_t[0])

        def attend(j, s):
            band = slice(j * BLOCK, (j + 2) * BLOCK)
            probs, inv_denoms = [], []
            for h in range(N_KV_HEADS):
                sh = s[h * 2 * BLOCK:(h + 1) * 2 * BLOCK, :]
                m = jnp.maximum(jnp.max(sh, axis=0, keepdims=True), sink_rows[h])
                p = jnp.exp2(sh - m)
                denom = jnp.sum(p, axis=0, keepdims=True) + jnp.exp2(sink_rows[h] - m)
                probs.append(p.astype(_BF16))
                inv_denoms.append(1.0 / denom)
            v_both = jnp.concatenate([vt[h, :, band] for h in range(N_KV_HEADS)], axis=1)
            o_t = _dot(v_both, jnp.concatenate(probs, axis=0))
            o_t = jnp.concatenate([o_t[h * HEAD_DIM:(h + 1) * HEAD_DIM] * inv_denoms[h] for h in range(N_KV_HEADS)], axis=0)
            blks = []
            for g in range(GQA_GROUP):
                blks.append(o_t[:, g * BLOCK:(g + 1) * BLOCK].T.astype(_BF16))
            return jnp.concatenate(blks, axis=1)

        half = TILE // N_HALVES
        x1_halves, h2_halves, inv_halves = [], [], []
        for r in range(N_HALVES if with_c else 0):
            rows = slice(r * half, (r + 1) * half)
            x1_r = x_prev[rows, :] + _rmsnorm(mix_buf[rows, :], gpost_ref[...])
            x1_halves.append(x1_r)
            inv_halves.append(lax.rsqrt(jnp.mean(x1_r * x1_r, axis=-1, keepdims=True) + EPS))
            h2_halves.append(x1_r.astype(_BF16))
        s_blocks = [logits(j) for j in range(N_BLOCKS)] if with_ab else []
        next_proj = stage_a_compute(x_next) if with_ab else None
        work = [(r, c) for r in range(N_HALVES) for c in range(N_CHUNKS)]

        def attend_and_store(j):
            mixcat[j * BLOCK:(j + 1) * BLOCK, POOL_WIDTH:] = attend(j, s_blocks[j])

        extras = {}
        if with_ab:
            for j in range(N_BLOCKS):
                extras[ATTEND_SLOTS[j]] = functools.partial(attend_and_store, j)
            for g in range(N_POOL_GROUPS):
                extras[POOL_SLOTS[g]] = functools.partial(pool_group, g)

        def gate_up(k):
            r, c = work[k]
            return ffn_gate_up(h2_halves[r], inv_halves[r], c)

        if not with_c:
            work = []
            for k in sorted(extras):
                extras[k]()
        act_next = gate_up(0) if with_c else None
        acc = None
        for k, (r, c) in enumerate(work):
            act = act_next
            if k + 1 < len(work):
                act_next = gate_up(k + 1)
            part = ffn_down(c, act)
            acc = part if acc is None else acc + part
            if c == N_CHUNKS - 1:
                f = acc * (inv_halves[r] * inv_halves[r])
                y_tile[r * half:(r + 1) * half, :] = x1_halves[r] + _rmsnorm(f, gpost2_ref[...])
                acc = None
            if k in extras:
                extras[k]()
        if with_ab:
            mix_buf[...] = _dot(mixcat[...], wout_ref[...])
            stage_a_store(*next_proj, next_starts_seq)

        @pl.when(step >= 1)
        def _store_y():
            y_copy(step - 1, out_slot).start()

        return carry

    first_step()
    step_body(0, 0, with_c=False)
    lax.fori_loop(1, n_tiles, step_body, 0)
    step_body(n_tiles, 0, with_ab=False)
    y_copy(n_tiles - 2, (n_tiles - 1) % 2).wait()
    y_copy(n_tiles - 1, n_tiles % 2).wait()


def _layer(x, g_pre_mix, w_in, w_pool, pool_scale, rel_bias, sinks, w_out, g_post_mix, g_pre_ffn, w_gate, w_up, w_down,
           g_post_ffn):
    batch, seq, _ = x.shape
    tokens = batch * seq
    n_tiles = tokens // TILE
    tiles_per_seq = seq // TILE
    bucket_t = jnp.asarray(_bucket_table_t())
    x2 = x.reshape(tokens, D_MODEL)
    row = lambda a: a.reshape(1, -1)

    vmem = pl.BlockSpec(memory_space=pltpu.VMEM)
    smem = pl.BlockSpec(memory_space=pltpu.SMEM)
    hbm = pl.BlockSpec(memory_space=pl.ANY)
    out = pl.pallas_call(
        functools.partial(_layer_kernel, tiles_per_seq, n_tiles),
        out_shape=jax.ShapeDtypeStruct(x2.shape, x2.dtype),
        in_specs=[hbm, hbm, vmem, vmem, smem, smem, vmem, hbm, vmem, hbm, hbm, hbm, vmem, vmem, vmem],
        out_specs=hbm,
        scratch_shapes=[
            pltpu.VMEM((POOL_HALO + TILE, POOL_WIDTH), _F32),
            pltpu.VMEM((N_BLOCKS, GQA_GROUP * BLOCK, KV_WIDTH), _BF16),
            pltpu.VMEM((N_KV_HEADS, BLOCK + TILE, KV_WIDTH), _BF16),
            pltpu.VMEM((N_KV_HEADS, KV_WIDTH, BLOCK + TILE), _BF16),
            pltpu.VMEM((TILE, D_MODEL), _BF16),
            pltpu.VMEM((TILE, D_MODEL), _F32),
            pltpu.VMEM((2, N_KV_HEADS * 2 * BLOCK, GQA_GROUP * BLOCK), _F32),
            pltpu.VMEM((D_MODEL, IN_WIDTH), _BF16),
            pltpu.VMEM((N_POOL_GROUPS, POOL_GROUP_DIM, POOL_GROUP_DIM), _BF16),
            pltpu.VMEM((D_MODEL, D_MODEL), _BF16),
            pltpu.VMEM((D_MODEL, D_FF), _BF16),
            pltpu.VMEM((D_MODEL, D_FF), _BF16),
            pltpu.VMEM((D_FF, D_MODEL), _BF16),
            pltpu.VMEM((WEIGHT_STAGE_SLOTS, WEIGHT_STAGE_ROWS, D_FF), _F32),
            pltpu.SemaphoreType.DMA((WEIGHT_STAGE_SLOTS,)),
            pltpu.VMEM((X_RING_SLOTS, TILE, D_MODEL), _F32),
            pltpu.SemaphoreType.DMA((X_RING_SLOTS,)),
            pltpu.VMEM((2, TILE, D_MODEL), _F32),
            pltpu.SemaphoreType.DMA((2,)),
            pltpu.VMEM((2, D_MODEL, BLOCK), _F32),
        ],
        compiler_params=pltpu.CompilerParams(vmem_limit_bytes=VMEM_LIMIT_BYTES),
        name="hybrid_layer",
    )(x2, w_in, w_pool, row(pool_scale), rel_bias.T, sinks, bucket_t, w_out,
      row(g_post_mix), w_gate, w_up, w_down, row(g_post_ffn), row(g_pre_mix), row(g_pre_ffn))
    return out.reshape(x.shape)


def kernel(x, g_pre_mix, w_in, w_pool, pool_scale, rel_bias, sinks, w_out, g_post_mix, g_pre_ffn, w_gate, w_up, w_down, g_post_ffn):
    depth = g_pre_mix.shape[0]
    for l in range(depth):
        x = _layer(x, g_pre_mix[l], w_in[l], w_pool[l], pool_scale[l], rel_bias, sinks[l], w_out[l], g_post_mix[l],
                   g_pre_ffn[l], w_gate[l], w_up[l], w_down[l], g_post_ffn[l])
    return x
```

```python
import functools

import numpy as np
import jax
import jax.numpy as jnp
from jax import lax
from jax.experimental import pallas as pl
from jax.experimental.pallas import tpu as pltpu

D_MODEL = 1024
POOL_WIDTH = 512
POOL_WINDOWS = (2, 4, 8, 16)
N_POOL_GROUPS = len(POOL_WINDOWS)
POOL_GROUP_DIM = POOL_WIDTH // N_POOL_GROUPS
ATTN_WIDTH = 512
HEAD_DIM = 64
N_Q_HEADS = 8
N_KV_HEADS = 2
GQA_GROUP = N_Q_HEADS // N_KV_HEADS
WINDOW = 128
BLOCK = 128
N_BUCKETS = 32
MAX_EXACT = N_BUCKETS // 2
MAX_DISTANCE = 128
KV_WIDTH = N_KV_HEADS * HEAD_DIM
IN_WIDTH = POOL_WIDTH + ATTN_WIDTH + 2 * KV_WIDTH
D_FF = 2816
EPS = 1e-6
NEG_INF = -1e30
LOG2E = 1.4426950408889634

POOL_HALO = 16
TILE = 512
N_BLOCKS = TILE // BLOCK
FFN_CHUNK = 256
N_CHUNKS = D_FF // FFN_CHUNK
N_HALVES = 2
X_RING_SLOTS = 4
ATTEND_SLOTS = (0, 4, 8, 12)
POOL_SLOTS = (14, 16, 18, 20)
WEIGHT_STAGE_ROWS = 128
WEIGHT_STAGE_SLOTS = 4
WOUT_BLOCK_ROWS = HEAD_DIM
VMEM_LIMIT_BYTES = 63 * 1024 * 1024

_F32 = jnp.float32
_BF16 = jnp.bfloat16


def _dot(a, b):
    return jnp.dot(a, b, preferred_element_type=_F32)


def _rmsnorm(xf, g):
    ms = jnp.mean(xf * xf, axis=-1, keepdims=True)
    return xf * lax.rsqrt(ms + EPS) * g


def _bucket_table_t():
    qi = np.arange(BLOCK)[None, :]
    kj = np.arange(2 * BLOCK)[:, None]
    dist = qi + BLOCK - kj
    n = np.maximum(dist, 0)
    nf = np.maximum(n, 1).astype(np.float32)
    large = MAX_EXACT + (
        np.log(nf / np.float32(MAX_EXACT)) / np.float32(np.log(MAX_DISTANCE / MAX_EXACT)) * np.float32(N_BUCKETS - MAX_EXACT)
    ).astype(np.int32)
    large = np.minimum(large, N_BUCKETS - 1)
    bucket = np.where(n < MAX_EXACT, n, large)
    in_window = (dist >= 0) & (dist < WINDOW)
    return np.where(in_window, bucket, -1).astype(np.int32)


def _layer_kernel(tiles_per_seq, n_tiles,
                  x_hbm, win_hbm, wpool_f32_ref, pscale_ref, relb_t_ref, sinks_ref, bkt_ref, wout_hbm,
                  gpost_ref, wg_hbm, wu_hbm, wd_hbm, gpost2_ref, gin_row_ref, gffn_row_ref,
                  y_hbm, u_buf, q_stack, kcat, vt, mixcat, mix_buf, bias_t,
                  win_ref, wpool_ref, wout_ref, wg_ref, wu_ref, wd_ref, stage, stage_sem,
                  x_ring, x_sem, out_buf, out_sem, gain_cols):
    def tile_rows(tile):
        start = tile * TILE
        return pl.ds(start if isinstance(start, int) else pl.multiple_of(start, TILE), TILE)

    def x_copy(tile, slot):
        return pltpu.make_async_copy(x_hbm.at[tile_rows(tile), :], x_ring.at[slot], x_sem.at[slot])

    def y_copy(tile, slot):
        return pltpu.make_async_copy(out_buf.at[slot], y_hbm.at[tile_rows(tile), :], out_sem.at[slot])

    def stage_a_compute(x_ref):
        x = x_ref[...]
        inv_rms = lax.rsqrt(jnp.mean(x * x, axis=-1, keepdims=True) + EPS)
        xb = x.astype(_BF16)
        u = _dot(xb, win_ref[:, 0:POOL_WIDTH]) * inv_rms
        q = (_dot(xb, win_ref[:, POOL_WIDTH:POOL_WIDTH + ATTN_WIDTH]) * (inv_rms * LOG2E)).astype(_BF16)
        kv = _dot(xb, win_ref[:, POOL_WIDTH + ATTN_WIDTH:IN_WIDTH]) * inv_rms
        return u, q, kv

    def stage_a_store(u, q, kv, starts_seq):
        def history(tail):
            if starts_seq is None:
                return jnp.zeros(tail.shape, tail.dtype)
            return jnp.where(starts_seq, jnp.zeros(tail.shape, tail.dtype), tail[...])

        u_buf[0:POOL_HALO, :] = history(u_buf.at[TILE:TILE + POOL_HALO, :])
        u_buf[POOL_HALO:POOL_HALO + TILE, :] = u
        for j in range(N_BLOCKS):
            for g in range(GQA_GROUP):
                q_stack[j, g * BLOCK:(g + 1) * BLOCK, :] = q[j * BLOCK:(j + 1) * BLOCK, g * BLOCK:(g + 1) * BLOCK]
        k = kv[:, 0:KV_WIDTH]
        v = kv[:, KV_WIDTH:2 * KV_WIDTH]
        lane = lax.broadcasted_iota(jnp.int32, k.shape, 1)
        kcat[:, 0:BLOCK, :] = history(kcat.at[:, TILE:TILE + BLOCK, :])
        kcat[0, BLOCK:BLOCK + TILE, :] = jnp.where(lane < HEAD_DIM, k, 0.0).astype(_BF16)
        kcat[1, BLOCK:BLOCK + TILE, :] = jnp.where(lane >= HEAD_DIM, k, 0.0).astype(_BF16)
        v_t = v.T
        row = lax.broadcasted_iota(jnp.int32, v_t.shape, 0)
        vt[:, :, 0:BLOCK] = history(vt.at[:, :, TILE:TILE + BLOCK])
        vt[0, :, BLOCK:BLOCK + TILE] = jnp.where(row < HEAD_DIM, v_t, 0.0).astype(_BF16)
        vt[1, :, BLOCK:BLOCK + TILE] = jnp.where(row >= HEAD_DIM, v_t, 0.0).astype(_BF16)

    def prepare_weights():
        def stream(src, rows_per_chunk, n_chunks, width, src_row, consume, placements):
            per_iter = len(placements)
            n_iter = n_chunks // per_iter

            def copies(i, slot):
                return [pltpu.make_async_copy(src.at[pl.ds(src_row(i * per_iter + k), rows_per_chunk), :],
                                              stage.at[slot, r0:r0 + rows_per_chunk, l0:l0 + width],
                                              stage_sem.at[slot])
                        for k, (r0, l0) in enumerate(placements)]

            def start(i, slot):
                for c in copies(i, slot):
                    c.start()

            for i in range(min(WEIGHT_STAGE_SLOTS - 1, n_iter)):
                start(i, i)

            def body(i, carry):
                slot = lax.rem(i, WEIGHT_STAGE_SLOTS)
                ahead = i + WEIGHT_STAGE_SLOTS - 1

                @pl.when(ahead < n_iter)
                def _prefetch():
                    start(ahead, lax.rem(ahead, WEIGHT_STAGE_SLOTS))

                for c in copies(i, slot):
                    c.wait()
                for k, (r0, l0) in enumerate(placements):
                    consume(i * per_iter + k, stage[slot, r0:r0 + rows_per_chunk, l0:l0 + width])
                return carry

            lax.fori_loop(0, n_iter, body, 0)

        rows = WEIGHT_STAGE_ROWS

        def chunk_rows(i):
            return pl.ds(pl.multiple_of(i * rows, rows), rows)

        for n, gain_row_ref in enumerate((gin_row_ref, gffn_row_ref)):
            gain_cols[n] = jnp.broadcast_to(gain_row_ref[...], (BLOCK, D_MODEL)).T

        def gain_col(n, i):
            return gain_cols[n, chunk_rows(i), 0:1]

        def consume_in(i, blk):
            qb = blk[:, POOL_WIDTH:POOL_WIDTH + ATTN_WIDTH]
            heads = [qb[:, (h * GQA_GROUP + g) * HEAD_DIM:(h * GQA_GROUP + g + 1) * HEAD_DIM]
                     for g in range(GQA_GROUP) for h in range(N_KV_HEADS)]
            q_regrouped = jnp.concatenate(heads, axis=1) * (HEAD_DIM ** -0.5)
            full = jnp.concatenate([blk[:, :POOL_WIDTH], q_regrouped, blk[:, POOL_WIDTH + ATTN_WIDTH:]], axis=1)
            win_ref[chunk_rows(i), :] = (gain_col(0, i) * full).astype(_BF16)

        def consume_gated(dst):
            def consume(i, blk):
                dst[chunk_rows(i), :] = (gain_col(1, i) * blk).astype(_BF16)
            return consume

        def consume_down(i, blk):
            wd_ref[chunk_rows(i), :] = blk.astype(_BF16)

        def wout_src_row(i):
            j = i - POOL_WIDTH // WOUT_BLOCK_ROWS
            head = (j % N_KV_HEADS) * GQA_GROUP + j // N_KV_HEADS
            return jnp.where(j < 0, i, POOL_WIDTH // WOUT_BLOCK_ROWS + head) * WOUT_BLOCK_ROWS

        def consume_out(i, blk):
            dst = pl.ds(pl.multiple_of(i * WOUT_BLOCK_ROWS, WOUT_BLOCK_ROWS), WOUT_BLOCK_ROWS)
            wout_ref[dst, :] = blk.astype(_BF16)

        stream(win_hbm, rows, D_MODEL // rows, IN_WIDTH, lambda c: c * rows, consume_in, [(0, 0), (0, IN_WIDTH)])
        stream(wout_hbm, WOUT_BLOCK_ROWS, D_MODEL // WOUT_BLOCK_ROWS, D_MODEL, wout_src_row, consume_out,
               [(0, 0), (WOUT_BLOCK_ROWS, 0), (0, D_MODEL), (WOUT_BLOCK_ROWS, D_MODEL)])
        stream(wg_hbm, rows, D_MODEL // rows, D_FF, lambda c: c * rows, consume_gated(wg_ref), [(0, 0)])
        stream(wu_hbm, rows, D_MODEL // rows, D_FF, lambda c: c * rows, consume_gated(wu_ref), [(0, 0)])
        stream(wd_hbm, rows, D_FF // rows, D_MODEL, lambda c: c * rows, consume_down, [(0, 0), (0, D_MODEL)])
        wpool_ref[...] = wpool_f32_ref[...].astype(_BF16)

    def first_step():
        x_copy(0, 0).start()
        x_copy(1, 1).start()
        prepare_weights()
        x_copy(0, 0).wait()
        stage_a_store(*stage_a_compute(x_ring.at[0]), None)
        bkt = bkt_ref[...]
        key_is_prev = lax.broadcasted_iota(jnp.int32, bkt.shape, 0) < BLOCK
        for h in range(N_KV_HEADS):
            for g in range(GQA_GROUP):
                head = h * GQA_GROUP + g

                def body(bb, tab):
                    return jnp.where(bkt == bb, relb_t_ref[head, bb], tab)

                tab = lax.fori_loop(0, N_BUCKETS, body, jnp.zeros(bkt.shape, _F32))
                tab = jnp.where(bkt < 0, NEG_INF, tab * LOG2E)
                rows = slice(h * 2 * BLOCK, (h + 1) * 2 * BLOCK)
                cols = slice(g * BLOCK, (g + 1) * BLOCK)
                bias_t[0, rows, cols] = tab
                bias_t[1, rows, cols] = jnp.where(key_is_prev, NEG_INF, tab)

    def step_body(step, carry, with_ab=True, with_c=True):
        seq_tile = lax.rem(jnp.minimum(step, n_tiles - 1), tiles_per_seq)
        next_starts_seq = lax.rem(jnp.minimum(step + 1, n_tiles - 1), tiles_per_seq) == 0
        out_slot = lax.rem(step, 2)

        @pl.when(step + 2 < n_tiles)
        def _prefetch_x():
            x_copy(step + 2, lax.rem(step + 2, X_RING_SLOTS)).start()

        @pl.when(step + 1 < n_tiles)
        def _wait_x():
            x_copy(step + 1, lax.rem(step + 1, X_RING_SLOTS)).wait()

        @pl.when(step >= 3)
        def _wait_y():
            y_copy(step - 3, out_slot).wait()

        x_next = x_ring.at[lax.rem(step + 1, X_RING_SLOTS)]
        x_prev = x_ring.at[lax.rem(jnp.maximum(step - 1, 0), X_RING_SLOTS)]
        y_tile = out_buf.at[out_slot]

        def ffn_gate_up(h2, inv_rms2, c):
            cols = slice(c * FFN_CHUNK, (c + 1) * FFN_CHUNK)
            dg = _dot(h2, wg_ref[:, cols])
            du = _dot(h2, wu_ref[:, cols])
            return (dg * du * (1.0 / (1.0 + jnp.exp2(dg * (inv_rms2 * (-LOG2E)))))).astype(_BF16)

        def ffn_down(c, act):
            return _dot(act, wd_ref[c * FFN_CHUNK:(c + 1) * FFN_CHUNK, :])

        pos = seq_tile * TILE + lax.broadcasted_iota(jnp.int32, (TILE, 1), 0)

        def pool_group(g):
            w = POOL_WINDOWS[g]
            cols = slice(g * POOL_GROUP_DIM, (g + 1) * POOL_GROUP_DIM)
            ext = u_buf[:, cols]
            acc = ext
            lag = 1
            while lag < w:
                acc = acc + jnp.concatenate([acc[:lag], acc[:-lag]], axis=0)
                lag *= 2
            count = jnp.minimum(pos + 1, w).astype(_F32)
            pooled = acc[POOL_HALO:] / count - ext[POOL_HALO:]
            mixed = _dot(pooled.astype(_BF16), wpool_ref[g]) * pscale_ref[:, cols]
            mixcat[:, cols] = mixed.astype(_BF16)

        col = lax.broadcasted_iota(jnp.int32, (1, GQA_GROUP * BLOCK), 1)
        sink_rows = []
        for h in range(N_KV_HEADS):
            row = jnp.full((1, GQA_GROUP * BLOCK), sinks_ref[h * GQA_GROUP], _F32)
            for g in range(1, GQA_GROUP):
                row = jnp.where(col >= g * BLOCK, sinks_ref[h * GQA_GROUP + g], row)
            sink_rows.append(row * LOG2E)
        first_variant = jnp.where(seq_tile == 0, 1, 0)

        def logits(j):
            band = slice(j * BLOCK, (j + 2) * BLOCK)
            kb = jnp.concatenate([kcat[0, band, :], kcat[1, band, :]], axis=0)
            s = lax.dot_general(kb, q_stack[j], (((1,), (1,)), ((), ())), preferred_element_type=_F32)
            return s + (bias_t[first_variant] if j == 0 else bias_t[0])

        def attend(j, s):
            band = slice(j * BLOCK, (j + 2) * BLOCK)
            probs, inv_denoms = [], []
            for h in range(N_KV_HEADS):
                sh = s[h * 2 * BLOCK:(h + 1) * 2 * BLOCK, :]
                m = jnp.maximum(jnp.max(sh, axis=0, keepdims=True), sink_rows[h])
                p = jnp.exp2(sh - m)
                denom = jnp.sum(p, axis=0, keepdims=True) + jnp.exp2(sink_rows[h] - m)
                probs.append(p.astype(_BF16))
                inv_denoms.append(1.0 / denom)
            v_both = jnp.concatenate([vt[h, :, band] for h in range(N_KV_HEADS)], axis=1)
            o_t = _dot(v_both, jnp.concatenate(probs, axis=0))
            o_t = jnp.concatenate([o_t[h * HEAD_DIM:(h + 1) * HEAD_DIM] * inv_denoms[h] for h in range(N_KV_HEADS)], axis=0)
            blks = []
            for g in range(GQA_GROUP):
                blks.append(o_t[:, g * BLOCK:(g + 1) * BLOCK].T.astype(_BF16))
            return jnp.concatenate(blks, axis=1)

        half = TILE // N_HALVES
        x1_halves, h2_halves, inv_halves = [], [], []
        for r in range(N_HALVES if with_c else 0):
            rows = slice(r * half, (r + 1) * half)
            x1_r = x_prev[rows, :] + _rmsnorm(mix_buf[rows, :], gpost_ref[...])
            x1_halves.append(x1_r)
            inv_halves.append(lax.rsqrt(jnp.mean(x1_r * x1_r, axis=-1, keepdims=True) + EPS))
            h2_halves.append(x1_r.astype(_BF16))
        next_proj = stage_a_compute(x_next) if with_ab else None
        s_blocks = [logits(j) for j in range(N_BLOCKS)] if with_ab else []
        work = [(r, c) for r in range(N_HALVES) for c in range(N_CHUNKS)]

        def attend_and_store(j):
            mixcat[j * BLOCK:(j + 1) * BLOCK, POOL_WIDTH:] = attend(j, s_blocks[j])

        extras = {}
        if with_ab:
            for j in range(N_BLOCKS):
                extras[ATTEND_SLOTS[j]] = functools.partial(attend_and_store, j)
            for g in range(N_POOL_GROUPS):
                extras[POOL_SLOTS[g]] = functools.partial(pool_group, g)

        def gate_up(k):
            r, c = work[k]
            return ffn_gate_up(h2_halves[r], inv_halves[r], c)

        if not with_c:
            work = []
            for k in sorted(extras):
                extras[k]()
        act_next = gate_up(0) if with_c else None
        acc = None
        for k, (r, c) in enumerate(work):
            act = act_next
            if k + 1 < len(work):
                act_next = gate_up(k + 1)
            part = ffn_down(c, act)
            acc = part if acc is None else acc + part
            if c == N_CHUNKS - 1:
                f = acc * (inv_halves[r] * inv_halves[r])
                y_tile[r * half:(r + 1) * half, :] = x1_halves[r] + _rmsnorm(f, gpost2_ref[...])
                acc = None
            if k in extras:
                extras[k]()
        if with_ab:
            mix_buf[...] = _dot(mixcat[...], wout_ref[...])
            stage_a_store(*next_proj, next_starts_seq)

        @pl.when(step >= 1)
        def _store_y():
            y_copy(step - 1, out_slot).start()

        return carry

    first_step()
    step_body(0, 0, with_c=False)
    lax.fori_loop(1, n_tiles, step_body, 0)
    step_body(n_tiles, 0, with_ab=False)
    y_copy(n_tiles - 2, (n_tiles - 1) % 2).wait()
    y_copy(n_tiles - 1, n_tiles % 2).wait()


def _layer(x, g_pre_mix, w_in, w_pool, pool_scale, rel_bias, sinks, w_out, g_post_mix, g_pre_ffn, w_gate, w_up, w_down,
           g_post_ffn):
    batch, seq, _ = x.shape
    tokens = batch * seq
    n_tiles = tokens // TILE
    tiles_per_seq = seq // TILE
    bucket_t = jnp.asarray(_bucket_table_t())
    x2 = x.reshape(tokens, D_MODEL)
    row = lambda a: a.reshape(1, -1)

    vmem = pl.BlockSpec(memory_space=pltpu.VMEM)
    smem = pl.BlockSpec(memory_space=pltpu.SMEM)
    hbm = pl.BlockSpec(memory_space=pl.ANY)
    out = pl.pallas_call(
        functools.partial(_layer_kernel, tiles_per_seq, n_tiles),
        out_shape=jax.ShapeDtypeStruct(x2.shape, x2.dtype),
        in_specs=[hbm, hbm, vmem, vmem, smem, smem, vmem, hbm, vmem, hbm, hbm, hbm, vmem, vmem, vmem],
        out_specs=hbm,
        scratch_shapes=[
            pltpu.VMEM((POOL_HALO + TILE, POOL_WIDTH), _F32),
            pltpu.VMEM((N_BLOCKS, GQA_GROUP * BLOCK, KV_WIDTH), _BF16),
            pltpu.VMEM((N_KV_HEADS, BLOCK + TILE, KV_WIDTH), _BF16),
            pltpu.VMEM((N_KV_HEADS, KV_WIDTH, BLOCK + TILE), _BF16),
            pltpu.VMEM((TILE, D_MODEL), _BF16),
            pltpu.VMEM((TILE, D_MODEL), _F32),
            pltpu.VMEM((2, N_KV_HEADS * 2 * BLOCK, GQA_GROUP * BLOCK), _F32),
            pltpu.VMEM((D_MODEL, IN_WIDTH), _BF16),
            pltpu.VMEM((N_POOL_GROUPS, POOL_GROUP_DIM, POOL_GROUP_DIM), _BF16),
            pltpu.VMEM((D_MODEL, D_MODEL), _BF16),
            pltpu.VMEM((D_MODEL, D_FF), _BF16),
            pltpu.VMEM((D_MODEL, D_FF), _BF16),
            pltpu.VMEM((D_FF, D_MODEL), _BF16),
            pltpu.VMEM((WEIGHT_STAGE_SLOTS, WEIGHT_STAGE_ROWS, D_FF), _F32),
            pltpu.SemaphoreType.DMA((WEIGHT_STAGE_SLOTS,)),
            pltpu.VMEM((X_RING_SLOTS, TILE, D_MODEL), _F32),
            pltpu.SemaphoreType.DMA((X_RING_SLOTS,)),
            pltpu.VMEM((2, TILE, D_MODEL), _F32),
            pltpu.SemaphoreType.DMA((2,)),
            pltpu.VMEM((2, D_MODEL, BLOCK), _F32),
        ],
        compiler_params=pltpu.CompilerParams(vmem_limit_bytes=VMEM_LIMIT_BYTES),
        name="hybrid_layer",
    )(x2, w_in, w_pool, row(pool_scale), rel_bias.T, sinks, bucket_t, w_out,
      row(g_post_mix), w_gate, w_up, w_down, row(g_post_ffn), row(g_pre_mix), row(g_pre_ffn))
    return out.reshape(x.shape)


def kernel(x, g_pre_mix, w_in, w_pool, pool_scale, rel_bias, sinks, w_out, g_post_mix, g_pre_ffn, w_gate, w_up, w_down, g_post_ffn):
    depth = g_pre_mix.shape[0]
    for l in range(depth):
        x = _layer(x, g_pre_mix[l], w_in[l], w_pool[l], pool_scale[l], rel_bias, sinks[l], w_out[l], g_post_mix[l],
                   g_pre_ffn[l], w_gate[l], w_up[l], w_down[l], g_post_ffn[l])
    return x
```

```python
import functools

import numpy as np
import jax
import jax.numpy as jnp
from jax import lax
from jax.experimental import pallas as pl
from jax.experimental.pallas import tpu as pltpu

D_MODEL = 1024
POOL_WIDTH = 512
POOL_WINDOWS = (2, 4, 8, 16)
N_POOL_GROUPS = len(POOL_WINDOWS)
POOL_GROUP_DIM = POOL_WIDTH // N_POOL_GROUPS
ATTN_WIDTH = 512
HEAD_DIM = 64
N_Q_HEADS = 8
N_KV_HEADS = 2
GQA_GROUP = N_Q_HEADS // N_KV_HEADS
WINDOW = 128
BLOCK = 128
N_BUCKETS = 32
MAX_EXACT = N_BUCKETS // 2
MAX_DISTANCE = 128
KV_WIDTH = N_KV_HEADS * HEAD_DIM
IN_WIDTH = POOL_WIDTH + ATTN_WIDTH + 2 * KV_WIDTH
D_FF = 2816
EPS = 1e-6
NEG_INF = -1e30
LOG2E = 1.4426950408889634

POOL_HALO = 16
TILE = 512
N_BLOCKS = TILE // BLOCK
FFN_CHUNK = 256
N_CHUNKS = D_FF // FFN_CHUNK
N_HALVES = 2
X_RING_SLOTS = 4
ATTEND_SLOTS = (0, 3, 6, 9)
POOL_SLOTS = (12, 14, 16, 18)
WEIGHT_STAGE_ROWS = 128
WEIGHT_STAGE_SLOTS = 4
WOUT_BLOCK_ROWS = HEAD_DIM
VMEM_LIMIT_BYTES = 63 * 1024 * 1024

_F32 = jnp.float32
_BF16 = jnp.bfloat16


def _dot(a, b):
    return jnp.dot(a, b, preferred_element_type=_F32)


def _rmsnorm(xf, g):
    ms = jnp.mean(xf * xf, axis=-1, keepdims=True)
    return xf * lax.rsqrt(ms + EPS) * g


def _bucket_table_t():
    qi = np.arange(BLOCK)[None, :]
    kj = np.arange(2 * BLOCK)[:, None]
    dist = qi + BLOCK - kj
    n = np.maximum(dist, 0)
    nf = np.maximum(n, 1).astype(np.float32)
    large = MAX_EXACT + (
        np.log(nf / np.float32(MAX_EXACT)) / np.float32(np.log(MAX_DISTANCE / MAX_EXACT)) * np.float32(N_BUCKETS - MAX_EXACT)
    ).astype(np.int32)
    large = np.minimum(large, N_BUCKETS - 1)
    bucket = np.where(n < MAX_EXACT, n, large)
    in_window = (dist >= 0) & (dist < WINDOW)
    return np.where(in_window, bucket, -1).astype(np.int32)


def _layer_kernel(tiles_per_seq, n_tiles,
                  x_hbm, win_hbm, wpool_f32_ref, pscale_ref, relb_t_ref, sinks_ref, bkt_ref, wout_hbm,
                  gpost_ref, wg_hbm, wu_hbm, wd_hbm, gpost2_ref, gin_row_ref, gffn_row_ref,
                  y_hbm, u_buf, q_stack, kcat, vt, mixcat, mix_buf, bias_t,
                  win_ref, wpool_ref, wout_ref, wg_ref, wu_ref, wd_ref, stage, stage_sem,
                  x_ring, x_sem, out_buf, out_sem, gain_cols):
    def tile_rows(tile):
        start = tile * TILE
        return pl.ds(start if isinstance(start, int) else pl.multiple_of(start, TILE), TILE)

    def x_copy(tile, slot):
        return pltpu.make_async_copy(x_hbm.at[tile_rows(tile), :], x_ring.at[slot], x_sem.at[slot])

    def y_copy(tile, slot):
        return pltpu.make_async_copy(out_buf.at[slot], y_hbm.at[tile_rows(tile), :], out_sem.at[slot])

    def stage_a_compute(x_ref):
        x = x_ref[...]
        inv_rms = lax.rsqrt(jnp.mean(x * x, axis=-1, keepdims=True) + EPS)
        xb = x.astype(_BF16)
        u = _dot(xb, win_ref[:, 0:POOL_WIDTH]) * inv_rms
        q = (_dot(xb, win_ref[:, POOL_WIDTH:POOL_WIDTH + ATTN_WIDTH]) * (inv_rms * LOG2E)).astype(_BF16)
        kv = _dot(xb, win_ref[:, POOL_WIDTH + ATTN_WIDTH:IN_WIDTH]) * inv_rms
        return u, q, kv

    def stage_a_store(u, q, kv, starts_seq):
        def history(tail):
            if starts_seq is None:
                return jnp.zeros(tail.shape, tail.dtype)
            return jnp.where(starts_seq, jnp.zeros(tail.shape, tail.dtype), tail[...])

        u_buf[0:POOL_HALO, :] = history(u_buf.at[TILE:TILE + POOL_HALO, :])
        u_buf[POOL_HALO:POOL_HALO + TILE, :] = u
        for j in range(N_BLOCKS):
            for g in range(GQA_GROUP):
                q_stack[j, g * BLOCK:(g + 1) * BLOCK, :] = q[j * BLOCK:(j + 1) * BLOCK, g * BLOCK:(g + 1) * BLOCK]
        k = kv[:, 0:KV_WIDTH]
        v = kv[:, KV_WIDTH:2 * KV_WIDTH]
        lane = lax.broadcasted_iota(jnp.int32, k.shape, 1)
        kcat[:, 0:BLOCK, :] = history(kcat.at[:, TILE:TILE + BLOCK, :])
        kcat[0, BLOCK:BLOCK + TILE, :] = jnp.where(lane < HEAD_DIM, k, 0.0).astype(_BF16)
        kcat[1, BLOCK:BLOCK + TILE, :] = jnp.where(lane >= HEAD_DIM, k, 0.0).astype(_BF16)
        v_t = v.T
        row = lax.broadcasted_iota(jnp.int32, v_t.shape, 0)
        vt[:, :, 0:BLOCK] = history(vt.at[:, :, TILE:TILE + BLOCK])
        vt[0, :, BLOCK:BLOCK + TILE] = jnp.where(row < HEAD_DIM, v_t, 0.0).astype(_BF16)
        vt[1, :, BLOCK:BLOCK + TILE] = jnp.where(row >= HEAD_DIM, v_t, 0.0).astype(_BF16)

    def prepare_weights():
        def stream(src, rows_per_chunk, n_chunks, width, src_row, consume, placements):
            per_iter = len(placements)
            n_iter = n_chunks // per_iter

            def copies(i, slot):
                return [pltpu.make_async_copy(src.at[pl.ds(src_row(i * per_iter + k), rows_per_chunk), :],
                                              stage.at[slot, r0:r0 + rows_per_chunk, l0:l0 + width],
                                              stage_sem.at[slot])
                        for k, (r0, l0) in enumerate(placements)]

            def start(i, slot):
                for c in copies(i, slot):
                    c.start()

            for i in range(min(WEIGHT_STAGE_SLOTS - 1, n_iter)):
                start(i, i)

            def body(i, carry):
                slot = lax.rem(i, WEIGHT_STAGE_SLOTS)
                ahead = i + WEIGHT_STAGE_SLOTS - 1

                @pl.when(ahead < n_iter)
                def _prefetch():
                    start(ahead, lax.rem(ahead, WEIGHT_STAGE_SLOTS))

                for c in copies(i, slot):
                    c.wait()
                for k, (r0, l0) in enumerate(placements):
                    consume(i * per_iter + k, stage[slot, r0:r0 + rows_per_chunk, l0:l0 + width])
                return carry

            lax.fori_loop(0, n_iter, body, 0)

        rows = WEIGHT_STAGE_ROWS

        def chunk_rows(i):
            return pl.ds(pl.multiple_of(i * rows, rows), rows)

        for n, gain_row_ref in enumerate((gin_row_ref, gffn_row_ref)):
            gain_cols[n] = jnp.broadcast_to(gain_row_ref[...], (BLOCK, D_MODEL)).T

        def gain_col(n, i):
            return gain_cols[n, chunk_rows(i), 0:1]

        def consume_in(i, blk):
            qb = blk[:, POOL_WIDTH:POOL_WIDTH + ATTN_WIDTH]
            heads = [qb[:, (h * GQA_GROUP + g) * HEAD_DIM:(h * GQA_GROUP + g + 1) * HEAD_DIM]
                     for g in range(GQA_GROUP) for h in range(N_KV_HEADS)]
            q_regrouped = jnp.concatenate(heads, axis=1) * (HEAD_DIM ** -0.5)
            full = jnp.concatenate([blk[:, :POOL_WIDTH], q_regrouped, blk[:, POOL_WIDTH + ATTN_WIDTH:]], axis=1)
            win_ref[chunk_rows(i), :] = (gain_col(0, i) * full).astype(_BF16)

        def consume_gated(dst):
            def consume(i, blk):
                dst[chunk_rows(i), :] = (gain_col(1, i) * blk).astype(_BF16)
            return consume

        def consume_down(i, blk):
            wd_ref[chunk_rows(i), :] = blk.astype(_BF16)

        def wout_src_row(i):
            j = i - POOL_WIDTH // WOUT_BLOCK_ROWS
            head = (j % N_KV_HEADS) * GQA_GROUP + j // N_KV_HEADS
            return jnp.where(j < 0, i, POOL_WIDTH // WOUT_BLOCK_ROWS + head) * WOUT_BLOCK_ROWS

        def consume_out(i, blk):
            dst = pl.ds(pl.multiple_of(i * WOUT_BLOCK_ROWS, WOUT_BLOCK_ROWS), WOUT_BLOCK_ROWS)
            wout_ref[dst, :] = blk.astype(_BF16)

        stream(win_hbm, rows, D_MODEL // rows, IN_WIDTH, lambda c: c * rows, consume_in, [(0, 0), (0, IN_WIDTH)])
        stream(wout_hbm, WOUT_BLOCK_ROWS, D_MODEL // WOUT_BLOCK_ROWS, D_MODEL, wout_src_row, consume_out,
               [(0, 0), (WOUT_BLOCK_ROWS, 0), (0, D_MODEL), (WOUT_BLOCK_ROWS, D_MODEL)])
        stream(wg_hbm, rows, D_MODEL // rows, D_FF, lambda c: c * rows, consume_gated(wg_ref), [(0, 0)])
        stream(wu_hbm, rows, D_MODEL // rows, D_FF, lambda c: c * rows, consume_gated(wu_ref), [(0, 0)])
        stream(wd_hbm, rows, D_FF // rows, D_MODEL, lambda c: c * rows, consume_down, [(0, 0), (0, D_MODEL)])
        wpool_ref[...] = wpool_f32_ref[...].astype(_BF16)

    def first_step():
        x_copy(0, 0).start()
        x_copy(1, 1).start()
        prepare_weights()
        x_copy(0, 0).wait()
        stage_a_store(*stage_a_compute(x_ring.at[0]), None)
        bkt = bkt_ref[...]
        key_is_prev = lax.broadcasted_iota(jnp.int32, bkt.shape, 0) < BLOCK
        for h in range(N_KV_HEADS):
            for g in range(GQA_GROUP):
                head = h * GQA_GROUP + g

                def body(bb, tab):
                    return jnp.where(bkt == bb, relb_t_ref[head, bb], tab)

                tab = lax.fori_loop(0, N_BUCKETS, body, jnp.zeros(bkt.shape, _F32))
                tab = jnp.where(bkt < 0, NEG_INF, tab * LOG2E)
                rows = slice(h * 2 * BLOCK, (h + 1) * 2 * BLOCK)
                cols = slice(g * BLOCK, (g + 1) * BLOCK)
                bias_t[0, rows, cols] = tab
                bias_t[1, rows, cols] = jnp.where(key_is_prev, NEG_INF, tab)

    def step_body(step, carry, with_ab=True, with_c=True):
        seq_tile = lax.rem(jnp.minimum(step, n_tiles - 1), tiles_per_seq)
        next_starts_seq = lax.rem(jnp.minimum(step + 1, n_tiles - 1), tiles_per_seq) == 0
        out_slot = lax.rem(step, 2)

        @pl.when(step + 2 < n_tiles)
        def _prefetch_x():
            x_copy(step + 2, lax.rem(step + 2, X_RING_SLOTS)).start()

        @pl.when(step + 1 < n_tiles)
        def _wait_x():
            x_copy(step + 1, lax.rem(step + 1, X_RING_SLOTS)).wait()

        @pl.when(step >= 3)
        def _wait_y():
            y_copy(step - 3, out_slot).wait()

        x_next = x_ring.at[lax.rem(step + 1, X_RING_SLOTS)]
        x_prev = x_ring.at[lax.rem(jnp.maximum(step - 1, 0), X_RING_SLOTS)]
        y_tile = out_buf.at[out_slot]

        def ffn_gate_up(h2, inv_rms2, c):
            cols = slice(c * FFN_CHUNK, (c + 1) * FFN_CHUNK)
            dg = _dot(h2, wg_ref[:, cols])
            du = _dot(h2, wu_ref[:, cols])
            return (dg * du * (1.0 / (1.0 + jnp.exp2(dg * (inv_rms2 * (-LOG2E)))))).astype(_BF16)

        def ffn_down(c, act):
            return _dot(act, wd_ref[c * FFN_CHUNK:(c + 1) * FFN_CHUNK, :])


        def pool_group(g):
            w = POOL_WINDOWS[g]
            cols = slice(g * POOL_GROUP_DIM, (g + 1) * POOL_GROUP_DIM)
            ext = u_buf[:, cols]
            acc = ext
            lag = 1
            while lag < w:
                acc = acc + jnp.concatenate([acc[:lag], acc[:-lag]], axis=0)
                lag *= 2
            head_rows = lax.broadcasted_iota(jnp.int32, (POOL_HALO, 1), 0)
            head_scale = jnp.where(seq_tile == 0, 1.0 / jnp.minimum(head_rows + 1, w).astype(_F32), 1.0 / w)
            sums = acc[POOL_HALO:]
            means = jnp.concatenate([sums[:POOL_HALO] * head_scale, sums[POOL_HALO:] * (1.0 / w)], axis=0)
            pooled = means - ext[POOL_HALO:]
            mixed = _dot(pooled.astype(_BF16), wpool_ref[g]) * pscale_ref[:, cols]
            mixcat[:, cols] = mixed.astype(_BF16)

        col = lax.broadcasted_iota(jnp.int32, (1, GQA_GROUP * BLOCK), 1)
        sink_rows = []
        for h in range(N_KV_HEADS):
            row = jnp.full((1, GQA_GROUP * BLOCK), sinks_ref[h * GQA_GROUP], _F32)
            for g in range(1, GQA_GROUP):
                row = jnp.where(col >= g * BLOCK, sinks_ref[h * GQA_GROUP + g], row)
            sink_rows.append(row * LOG2E)
        first_variant = jnp.where(seq_tile == 0, 1, 0)

        def logits(j):
            band = slice(j * BLOCK, (j + 2) * BLOCK)
            kb = jnp.concatenate([kcat[0, band, :], kcat[1, band, :]], axis=0)
            s = lax.dot_general(kb, q_stack[j], (((1,), (1,)), ((), ())), preferred_element_type=_F32)
            return s + (bias_t[first_variant] if j == 0 else bias_t[0])

        def attend(j, s):
            band = slice(j * BLOCK, (j + 2) * BLOCK)
            probs, inv_denoms = [], []
            for h in range(N_KV_HEADS):
                sh = s[h * 2 * BLOCK:(h + 1) * 2 * BLOCK, :]
                m = jnp.maximum(jnp.max(sh, axis=0, keepdims=True), sink_rows[h])
                p = jnp.exp2(sh - m)
                denom = jnp.sum(p, axis=0, keepdims=True) + jnp.exp2(sink_rows[h] - m)
                probs.append(p.astype(_BF16))
                inv_denoms.append(1.0 / denom)
            v_both = jnp.concatenate([vt[h, :, band] for h in range(N_KV_HEADS)], axis=1)
            o_t = _dot(v_both, jnp.concatenate(probs, axis=0))
            o_t = jnp.concatenate([o_t[h * HEAD_DIM:(h + 1) * HEAD_DIM] * inv_denoms[h] for h in range(N_KV_HEADS)], axis=0)
            blks = []
            for g in range(GQA_GROUP):
                blks.append(o_t[:, g * BLOCK:(g + 1) * BLOCK].T.astype(_BF16))
            return jnp.concatenate(blks, axis=1)

        half = TILE // N_HALVES
        x1_halves, h2_halves, inv_halves = [], [], []
        for r in range(N_HALVES if with_c else 0):
            rows = slice(r * half, (r + 1) * half)
            x1_r = x_prev[rows, :] + _rmsnorm(mix_buf[rows, :], gpost_ref[...])
            x1_halves.append(x1_r)
            inv_halves.append(lax.rsqrt(jnp.mean(x1_r * x1_r, axis=-1, keepdims=True) + EPS))
            h2_halves.append(x1_r.astype(_BF16))
        next_proj = stage_a_compute(x_next) if with_ab else None
        s_blocks = [logits(j) for j in range(N_BLOCKS)] if with_ab else []
        work = [(r, c) for r in range(N_HALVES) for c in range(N_CHUNKS)]

        def attend_and_store(j):
            mixcat[j * BLOCK:(j + 1) * BLOCK, POOL_WIDTH:] = attend(j, s_blocks[j])

        extras = {}
        if with_ab:
            for j in range(N_BLOCKS):
                extras[ATTEND_SLOTS[j]] = functools.partial(attend_and_store, j)
            for g in range(N_POOL_GROUPS):
                extras[POOL_SLOTS[g]] = functools.partial(pool_group, g)

        def gate_up(k):
            r, c = work[k]
            return ffn_gate_up(h2_halves[r], inv_halves[r], c)

        if not with_c:
            work = []
            for k in sorted(extras):
                extras[k]()
        act_next = gate_up(0) if with_c else None
        acc = None
        for k, (r, c) in enumerate(work):
            act = act_next
            if k + 1 < len(work):
                act_next = gate_up(k + 1)
            part = ffn_down(c, act)
            acc = part if acc is None else acc + part
            if c == N_CHUNKS - 1:
                f = acc * (inv_halves[r] * inv_halves[r])
                y_tile[r * half:(r + 1) * half, :] = x1_halves[r] + _rmsnorm(f, gpost2_ref[...])
                acc = None
            if k in extras:
                extras[k]()
        if with_ab:
            mix_buf[...] = _dot(mixcat[...], wout_ref[...])
            stage_a_store(*next_proj, next_starts_seq)

        @pl.when(step >= 1)
        def _store_y():
            y_copy(step - 1, out_slot).start()

        return carry

    first_step()
    step_body(0, 0, with_c=False)
    lax.fori_loop(1, n_tiles, step_body, 0)
    step_body(n_tiles, 0, with_ab=False)
    y_copy(n_tiles - 2, (n_tiles - 1) % 2).wait()
    y_copy(n_tiles - 1, n_tiles % 2).wait()


def _layer(x, g_pre_mix, w_in, w_pool, pool_scale, rel_bias, sinks, w_out, g_post_mix, g_pre_ffn, w_gate, w_up, w_down,
           g_post_ffn):
    batch, seq, _ = x.shape
    tokens = batch * seq
    n_tiles = tokens // TILE
    tiles_per_seq = seq // TILE
    bucket_t = jnp.asarray(_bucket_table_t())
    x2 = x.reshape(tokens, D_MODEL)
    row = lambda a: a.reshape(1, -1)

    vmem = pl.BlockSpec(memory_space=pltpu.VMEM)
    smem = pl.BlockSpec(memory_space=pltpu.SMEM)
    hbm = pl.BlockSpec(memory_space=pl.ANY)
    out = pl.pallas_call(
        functools.partial(_layer_kernel, tiles_per_seq, n_tiles),
        out_shape=jax.ShapeDtypeStruct(x2.shape, x2.dtype),
        in_specs=[hbm, hbm, vmem, vmem, smem, smem, vmem, hbm, vmem, hbm, hbm, hbm, vmem, vmem, vmem],
        out_specs=hbm,
        scratch_shapes=[
            pltpu.VMEM((POOL_HALO + TILE, POOL_WIDTH), _F32),
            pltpu.VMEM((N_BLOCKS, GQA_GROUP * BLOCK, KV_WIDTH), _BF16),
            pltpu.VMEM((N_KV_HEADS, BLOCK + TILE, KV_WIDTH), _BF16),
            pltpu.VMEM((N_KV_HEADS, KV_WIDTH, BLOCK + TILE), _BF16),
            pltpu.VMEM((TILE, D_MODEL), _BF16),
            pltpu.VMEM((TILE, D_MODEL), _F32),
            pltpu.VMEM((2, N_KV_HEADS * 2 * BLOCK, GQA_GROUP * BLOCK), _F32),
            pltpu.VMEM((D_MODEL, IN_WIDTH), _BF16),
            pltpu.VMEM((N_POOL_GROUPS, POOL_GROUP_DIM, POOL_GROUP_DIM), _BF16),
            pltpu.VMEM((D_MODEL, D_MODEL), _BF16),
            pltpu.VMEM((D_MODEL, D_FF), _BF16),
            pltpu.VMEM((D_MODEL, D_FF), _BF16),
            pltpu.VMEM((D_FF, D_MODEL), _BF16),
            pltpu.VMEM((WEIGHT_STAGE_SLOTS, WEIGHT_STAGE_ROWS, D_FF), _F32),
            pltpu.SemaphoreType.DMA((WEIGHT_STAGE_SLOTS,)),
            pltpu.VMEM((X_RING_SLOTS, TILE, D_MODEL), _F32),
            pltpu.SemaphoreType.DMA((X_RING_SLOTS,)),
            pltpu.VMEM((2, TILE, D_MODEL), _F32),
            pltpu.SemaphoreType.DMA((2,)),
            pltpu.VMEM((2, D_MODEL, BLOCK), _F32),
        ],
        compiler_params=pltpu.CompilerParams(vmem_limit_bytes=VMEM_LIMIT_BYTES),
        name="hybrid_layer",
    )(x2, w_in, w_pool, row(pool_scale), rel_bias.T, sinks, bucket_t, w_out,
      row(g_post_mix), w_gate, w_up, w_down, row(g_post_ffn), row(g_pre_mix), row(g_pre_ffn))
    return out.reshape(x.shape)


def kernel(x, g_pre_mix, w_in, w_pool, pool_scale, rel_bias, sinks, w_out, g_post_mix, g_pre_ffn, w_gate, w_up, w_down, g_post_ffn):
    depth = g_pre_mix.shape[0]
    for l in range(depth):
        x = _layer(x, g_pre_mix[l], w_in[l], w_pool[l], pool_scale[l], rel_bias, sinks[l], w_out[l], g_post_mix[l],
                   g_pre_ffn[l], w_gate[l], w_up[l], w_down[l], g_post_ffn[l])
    return x
```

```python
import functools

import numpy as np
import jax
import jax.numpy as jnp
from jax import lax
from jax.experimental import pallas as pl
from jax.experimental.pallas import tpu as pltpu

D_MODEL = 1024
POOL_WIDTH = 512
POOL_WINDOWS = (2, 4, 8, 16)
N_POOL_GROUPS = len(POOL_WINDOWS)
POOL_GROUP_DIM = POOL_WIDTH // N_POOL_GROUPS
ATTN_WIDTH = 512
HEAD_DIM = 64
N_Q_HEADS = 8
N_KV_HEADS = 2
GQA_GROUP = N_Q_HEADS // N_KV_HEADS
WINDOW = 128
BLOCK = 128
N_BUCKETS = 32
MAX_EXACT = N_BUCKETS // 2
MAX_DISTANCE = 128
KV_WIDTH = N_KV_HEADS * HEAD_DIM
IN_WIDTH = POOL_WIDTH + ATTN_WIDTH + 2 * KV_WIDTH
D_FF = 2816
EPS = 1e-6
NEG_INF = -1e30
LOG2E = 1.4426950408889634

POOL_HALO = 16
TILE = 512
N_BLOCKS = TILE // BLOCK
FFN_CHUNK = 256
N_CHUNKS = D_FF // FFN_CHUNK
N_HALVES = 2
X_RING_SLOTS = 4
ATTEND_SLOTS = (0, 3, 6, 9)
POOL_SLOTS = (12, 14, 16, 18)
WEIGHT_STAGE_ROWS = 128
WEIGHT_STAGE_SLOTS = 4
WOUT_BLOCK_ROWS = HEAD_DIM
VMEM_LIMIT_BYTES = 63 * 1024 * 1024

_F32 = jnp.float32
_BF16 = jnp.bfloat16


def _dot(a, b):
    return jnp.dot(a, b, preferred_element_type=_F32)


def _rmsnorm(xf, g):
    ms = jnp.mean(xf * xf, axis=-1, keepdims=True)
    return xf * lax.rsqrt(ms + EPS) * g


def _bucket_table_t():
    qi = np.arange(BLOCK)[None, :]
    kj = np.arange(2 * BLOCK)[:, None]
    dist = qi + BLOCK - kj
    n = np.maximum(dist, 0)
    nf = np.maximum(n, 1).astype(np.float32)
    large = MAX_EXACT + (
        np.log(nf / np.float32(MAX_EXACT)) / np.float32(np.log(MAX_DISTANCE / MAX_EXACT)) * np.float32(N_BUCKETS - MAX_EXACT)
    ).astype(np.int32)
    large = np.minimum(large, N_BUCKETS - 1)
    bucket = np.where(n < MAX_EXACT, n, large)
    in_window = (dist >= 0) & (dist < WINDOW)
    return np.where(in_window, bucket, -1).astype(np.int32)


def _layer_kernel(tiles_per_seq, n_tiles,
                  x_hbm, win_hbm, wpool_f32_ref, pscale_ref, relb_t_ref, sinks_ref, bkt_ref, wout_hbm,
                  gpost_ref, wg_hbm, wu_hbm, wd_hbm, gpost2_ref, gin_row_ref, gffn_row_ref,
                  y_hbm, u_buf, q_stack, kcat, vt, mixcat, mix_buf, bias_t,
                  win_ref, wpool_ref, wout_ref, wg_ref, wu_ref, wd_ref, stage, stage_sem,
                  x_ring, x_sem, out_buf, out_sem, gain_cols):
    def tile_rows(tile):
        start = tile * TILE
        return pl.ds(start if isinstance(start, int) else pl.multiple_of(start, TILE), TILE)

    def x_copy(tile, slot):
        return pltpu.make_async_copy(x_hbm.at[tile_rows(tile), :], x_ring.at[slot], x_sem.at[slot])

    def y_copy(tile, slot):
        return pltpu.make_async_copy(out_buf.at[slot], y_hbm.at[tile_rows(tile), :], out_sem.at[slot])

    def stage_a_compute(x_ref):
        x = x_ref[...]
        inv_rms = lax.rsqrt(jnp.mean(x * x, axis=-1, keepdims=True) + EPS)
        xb = x.astype(_BF16)
        u = _dot(xb, win_ref[:, 0:POOL_WIDTH]) * inv_rms
        q = (_dot(xb, win_ref[:, POOL_WIDTH:POOL_WIDTH + ATTN_WIDTH]) * (inv_rms * LOG2E)).astype(_BF16)
        kv = _dot(xb, win_ref[:, POOL_WIDTH + ATTN_WIDTH:IN_WIDTH]) * inv_rms
        return u, q, kv

    def stage_a_store(u, q, kv, starts_seq):
        def history(tail):
            if starts_seq is None:
                return jnp.zeros(tail.shape, tail.dtype)
            return jnp.where(starts_seq, jnp.zeros(tail.shape, tail.dtype), tail[...])

        u_buf[0:POOL_HALO, :] = history(u_buf.at[TILE:TILE + POOL_HALO, :])
        u_buf[POOL_HALO:POOL_HALO + TILE, :] = u
        for j in range(N_BLOCKS):
            for g in range(GQA_GROUP):
                q_stack[j, g * BLOCK:(g + 1) * BLOCK, :] = q[j * BLOCK:(j + 1) * BLOCK, g * BLOCK:(g + 1) * BLOCK]
        k = kv[:, 0:KV_WIDTH]
        v = kv[:, KV_WIDTH:2 * KV_WIDTH]
        lane = lax.broadcasted_iota(jnp.int32, k.shape, 1)
        kcat[:, 0:BLOCK, :] = history(kcat.at[:, TILE:TILE + BLOCK, :])
        kcat[0, BLOCK:BLOCK + TILE, :] = jnp.where(lane < HEAD_DIM, k, 0.0).astype(_BF16)
        kcat[1, BLOCK:BLOCK + TILE, :] = jnp.where(lane >= HEAD_DIM, k, 0.0).astype(_BF16)
        v_t = v.T
        row = lax.broadcasted_iota(jnp.int32, v_t.shape, 0)
        vt[:, :, 0:BLOCK] = history(vt.at[:, :, TILE:TILE + BLOCK])
        vt[0, :, BLOCK:BLOCK + TILE] = jnp.where(row < HEAD_DIM, v_t, 0.0).astype(_BF16)
        vt[1, :, BLOCK:BLOCK + TILE] = jnp.where(row >= HEAD_DIM, v_t, 0.0).astype(_BF16)

    def prepare_weights():
        def stream(src, rows_per_chunk, n_chunks, width, src_row, consume, placements):
            per_iter = len(placements)
            n_iter = n_chunks // per_iter

            def copies(i, slot):
                return [pltpu.make_async_copy(src.at[pl.ds(src_row(i * per_iter + k), rows_per_chunk), :],
                                              stage.at[slot, r0:r0 + rows_per_chunk, l0:l0 + width],
                                              stage_sem.at[slot])
                        for k, (r0, l0) in enumerate(placements)]

            def start(i, slot):
                for c in copies(i, slot):
                    c.start()

            for i in range(min(WEIGHT_STAGE_SLOTS - 1, n_iter)):
                start(i, i)

            def body(i, carry):
                slot = lax.rem(i, WEIGHT_STAGE_SLOTS)
                ahead = i + WEIGHT_STAGE_SLOTS - 1

                @pl.when(ahead < n_iter)
                def _prefetch():
                    start(ahead, lax.rem(ahead, WEIGHT_STAGE_SLOTS))

                for c in copies(i, slot):
                    c.wait()
                for k, (r0, l0) in enumerate(placements):
                    consume(i * per_iter + k, stage[slot, r0:r0 + rows_per_chunk, l0:l0 + width])
                return carry

            lax.fori_loop(0, n_iter, body, 0)

        rows = WEIGHT_STAGE_ROWS

        def chunk_rows(i):
            return pl.ds(pl.multiple_of(i * rows, rows), rows)

        for n, gain_row_ref in enumerate((gin_row_ref, gffn_row_ref)):
            gain_cols[n] = jnp.broadcast_to(gain_row_ref[...], (BLOCK, D_MODEL)).T

        def gain_col(n, i):
            return gain_cols[n, chunk_rows(i), 0:1]

        def consume_in(i, blk):
            qb = blk[:, POOL_WIDTH:POOL_WIDTH + ATTN_WIDTH]
            heads = [qb[:, (h * GQA_GROUP + g) * HEAD_DIM:(h * GQA_GROUP + g + 1) * HEAD_DIM]
                     for g in range(GQA_GROUP) for h in range(N_KV_HEADS)]
            q_regrouped = jnp.concatenate(heads, axis=1) * (HEAD_DIM ** -0.5)
            full = jnp.concatenate([blk[:, :POOL_WIDTH], q_regrouped, blk[:, POOL_WIDTH + ATTN_WIDTH:]], axis=1)
            win_ref[chunk_rows(i), :] = (gain_col(0, i) * full).astype(_BF16)

        def consume_gated(dst):
            def consume(i, blk):
                dst[chunk_rows(i), :] = (gain_col(1, i) * blk).astype(_BF16)
            return consume

        def consume_down(i, blk):
            wd_ref[chunk_rows(i), :] = blk.astype(_BF16)

        def wout_src_row(i):
            j = i - POOL_WIDTH // WOUT_BLOCK_ROWS
            head = (j % N_KV_HEADS) * GQA_GROUP + j // N_KV_HEADS
            return jnp.where(j < 0, i, POOL_WIDTH // WOUT_BLOCK_ROWS + head) * WOUT_BLOCK_ROWS

        def consume_out(i, blk):
            dst = pl.ds(pl.multiple_of(i * WOUT_BLOCK_ROWS, WOUT_BLOCK_ROWS), WOUT_BLOCK_ROWS)
            wout_ref[dst, :] = blk.astype(_BF16)

        stream(win_hbm, rows, D_MODEL // rows, IN_WIDTH, lambda c: c * rows, consume_in, [(0, 0), (0, IN_WIDTH)])
        stream(wout_hbm, WOUT_BLOCK_ROWS, D_MODEL // WOUT_BLOCK_ROWS, D_MODEL, wout_src_row, consume_out,
               [(0, 0), (WOUT_BLOCK_ROWS, 0), (0, D_MODEL), (WOUT_BLOCK_ROWS, D_MODEL)])
        stream(wg_hbm, rows, D_MODEL // rows, D_FF, lambda c: c * rows, consume_gated(wg_ref), [(0, 0)])
        stream(wu_hbm, rows, D_MODEL // rows, D_FF, lambda c: c * rows, consume_gated(wu_ref), [(0, 0)])
        stream(wd_hbm, rows, D_FF // rows, D_MODEL, lambda c: c * rows, consume_down, [(0, 0), (0, D_MODEL)])
        wpool_ref[...] = wpool_f32_ref[...].astype(_BF16)

    def first_step():
        x_copy(0, 0).start()
        x_copy(1, 1).start()
        prepare_weights()
        x_copy(0, 0).wait()
        stage_a_store(*stage_a_compute(x_ring.at[0]), None)
        bkt = bkt_ref[...]
        key_is_prev = lax.broadcasted_iota(jnp.int32, bkt.shape, 0) < BLOCK
        for h in range(N_KV_HEADS):
            for g in range(GQA_GROUP):
                head = h * GQA_GROUP + g

                def body(bb, tab):
                    return jnp.where(bkt == bb, relb_t_ref[head, bb], tab)

                tab = lax.fori_loop(0, N_BUCKETS, body, jnp.zeros(bkt.shape, _F32))
                tab = jnp.where(bkt < 0, NEG_INF, tab * LOG2E)
                rows = slice(h * 2 * BLOCK, (h + 1) * 2 * BLOCK)
                cols = slice(g * BLOCK, (g + 1) * BLOCK)
                bias_t[0, rows, cols] = tab
                bias_t[1, rows, cols] = jnp.where(key_is_prev, NEG_INF, tab)

    def step_body(step, carry, with_ab=True, with_c=True):
        seq_tile = lax.rem(jnp.minimum(step, n_tiles - 1), tiles_per_seq)
        next_starts_seq = lax.rem(jnp.minimum(step + 1, n_tiles - 1), tiles_per_seq) == 0
        out_slot = lax.rem(step, 2)

        @pl.when(step + 2 < n_tiles)
        def _prefetch_x():
            x_copy(step + 2, lax.rem(step + 2, X_RING_SLOTS)).start()

        @pl.when(step + 1 < n_tiles)
        def _wait_x():
            x_copy(step + 1, lax.rem(step + 1, X_RING_SLOTS)).wait()

        @pl.when(step >= 3)
        def _wait_y():
            y_copy(step - 3, out_slot).wait()

        x_next = x_ring.at[lax.rem(step + 1, X_RING_SLOTS)]
        x_prev = x_ring.at[lax.rem(jnp.maximum(step - 1, 0), X_RING_SLOTS)]
        y_tile = out_buf.at[out_slot]

        def ffn_gate_up(h2, sigmoid_scale, c):
            cols = slice(c * FFN_CHUNK, (c + 1) * FFN_CHUNK)
            dg = _dot(h2, wg_ref[:, cols])
            du = _dot(h2, wu_ref[:, cols])
            return (dg * du * (1.0 / (1.0 + jnp.exp2(dg * sigmoid_scale)))).astype(_BF16)

        def ffn_down(c, act):
            return _dot(act, wd_ref[c * FFN_CHUNK:(c + 1) * FFN_CHUNK, :])


        def pool_group(g):
            w = POOL_WINDOWS[g]
            cols = slice(g * POOL_GROUP_DIM, (g + 1) * POOL_GROUP_DIM)
            ext = u_buf[:, cols]
            acc = ext
            lag = 1
            while lag < w:
                acc = acc + jnp.concatenate([acc[:lag], acc[:-lag]], axis=0)
                lag *= 2
            head_rows = lax.broadcasted_iota(jnp.int32, (POOL_HALO, 1), 0)
            head_scale = jnp.where(seq_tile == 0, 1.0 / jnp.minimum(head_rows + 1, w).astype(_F32), 1.0 / w)
            sums = acc[POOL_HALO:]
            means = jnp.concatenate([sums[:POOL_HALO] * head_scale, sums[POOL_HALO:] * (1.0 / w)], axis=0)
            pooled = means - ext[POOL_HALO:]
            mixed = _dot(pooled.astype(_BF16), wpool_ref[g]) * pscale_ref[:, cols]
            mixcat[:, cols] = mixed.astype(_BF16)

        col = lax.broadcasted_iota(jnp.int32, (1, GQA_GROUP * BLOCK), 1)
        sink_rows = []
        for h in range(N_KV_HEADS):
            row = jnp.full((1, GQA_GROUP * BLOCK), sinks_ref[h * GQA_GROUP], _F32)
            for g in range(1, GQA_GROUP):
                row = jnp.where(col >= g * BLOCK, sinks_ref[h * GQA_GROUP + g], row)
            sink_rows.append(row * LOG2E)
        first_variant = jnp.where(seq_tile == 0, 1, 0)

        def logits(j):
            band = slice(j * BLOCK, (j + 2) * BLOCK)
            kb = jnp.concatenate([kcat[0, band, :], kcat[1, band, :]], axis=0)
            s = lax.dot_general(kb, q_stack[j], (((1,), (1,)), ((), ())), preferred_element_type=_F32)
            return s + (bias_t[first_variant] if j == 0 else bias_t[0])

        def attend(j, s):
            band = slice(j * BLOCK, (j + 2) * BLOCK)
            probs, inv_denoms = [], []
            for h in range(N_KV_HEADS):
                sh = s[h * 2 * BLOCK:(h + 1) * 2 * BLOCK, :]
                m = jnp.maximum(jnp.max(sh, axis=0, keepdims=True), sink_rows[h])
                p = jnp.exp2(sh - m)
                denom = jnp.sum(p, axis=0, keepdims=True) + jnp.exp2(sink_rows[h] - m)
                probs.append(p.astype(_BF16))
                inv_denoms.append(1.0 / denom)
            v_both = jnp.concatenate([vt[h, :, band] for h in range(N_KV_HEADS)], axis=1)
            o_t = _dot(v_both, jnp.concatenate(probs, axis=0))
            o_t = jnp.concatenate([o_t[h * HEAD_DIM:(h + 1) * HEAD_DIM] * inv_denoms[h] for h in range(N_KV_HEADS)], axis=0)
            blks = []
            for g in range(GQA_GROUP):
                blks.append(o_t[:, g * BLOCK:(g + 1) * BLOCK].T.astype(_BF16))
            return jnp.concatenate(blks, axis=1)

        half = TILE // N_HALVES
        x1_halves, h2_halves, inv_halves = [], [], []
        for r in range(N_HALVES if with_c else 0):
            rows = slice(r * half, (r + 1) * half)
            x1_r = x_prev[rows, :] + _rmsnorm(mix_buf[rows, :], gpost_ref[...])
            x1_halves.append(x1_r)
            inv_halves.append(lax.rsqrt(jnp.mean(x1_r * x1_r, axis=-1, keepdims=True) + EPS))
            h2_halves.append(x1_r.astype(_BF16))
        sigmoid_scales = [inv * (-LOG2E) for inv in inv_halves]
        next_proj = stage_a_compute(x_next) if with_ab else None
        s_blocks = [logits(j) for j in range(N_BLOCKS)] if with_ab else []
        work = [(r, c) for r in range(N_HALVES) for c in range(N_CHUNKS)]

        def attend_and_store(j):
            mixcat[j * BLOCK:(j + 1) * BLOCK, POOL_WIDTH:] = attend(j, s_blocks[j])

        extras = {}
        if with_ab:
            for j in range(N_BLOCKS):
                extras[ATTEND_SLOTS[j]] = functools.partial(attend_and_store, j)
            for g in range(N_POOL_GROUPS):
                extras[POOL_SLOTS[g]] = functools.partial(pool_group, g)

        def gate_up(k):
            r, c = work[k]
            return ffn_gate_up(h2_halves[r], sigmoid_scales[r], c)

        if not with_c:
            work = []
            for k in sorted(extras):
                extras[k]()
        act_next = gate_up(0) if with_c else None
        acc = None
        for k, (r, c) in enumerate(work):
            act = act_next
            if k + 1 < len(work):
                act_next = gate_up(k + 1)
            part = ffn_down(c, act)
            acc = part if acc is None else acc + part
            if c == N_CHUNKS - 1:
                f = acc * (inv_halves[r] * inv_halves[r])
                y_tile[r * half:(r + 1) * half, :] = x1_halves[r] + _rmsnorm(f, gpost2_ref[...])
                acc = None
            if k in extras:
                extras[k]()
        if with_ab:
            mix_buf[...] = _dot(mixcat[...], wout_ref[...])
            stage_a_store(*next_proj, next_starts_seq)

        @pl.when(step >= 1)
        def _store_y():
            y_copy(step - 1, out_slot).start()

        return carry

    first_step()
    step_body(0, 0, with_c=False)
    lax.fori_loop(1, n_tiles, step_body, 0)
    step_body(n_tiles, 0, with_ab=False)
    y_copy(n_tiles - 2, (n_tiles - 1) % 2).wait()
    y_copy(n_tiles - 1, n_tiles % 2).wait()


def _layer(x, g_pre_mix, w_in, w_pool, pool_scale, rel_bias, sinks, w_out, g_post_mix, g_pre_ffn, w_gate, w_up, w_down,
           g_post_ffn):
    batch, seq, _ = x.shape
    tokens = batch * seq
    n_tiles = tokens // TILE
    tiles_per_seq = seq // TILE
    bucket_t = jnp.asarray(_bucket_table_t())
    x2 = x.reshape(tokens, D_MODEL)
    row = lambda a: a.reshape(1, -1)

    vmem = pl.BlockSpec(memory_space=pltpu.VMEM)
    smem = pl.BlockSpec(memory_space=pltpu.SMEM)
    hbm = pl.BlockSpec(memory_space=pl.ANY)
    out = pl.pallas_call(
        functools.partial(_layer_kernel, tiles_per_seq, n_tiles),
        out_shape=jax.ShapeDtypeStruct(x2.shape, x2.dtype),
        in_specs=[hbm, hbm, vmem, vmem, smem, smem, vmem, hbm, vmem, hbm, hbm, hbm, vmem, vmem, vmem],
        out_specs=hbm,
        scratch_shapes=[
            pltpu.VMEM((POOL_HALO + TILE, POOL_WIDTH), _F32),
            pltpu.VMEM((N_BLOCKS, GQA_GROUP * BLOCK, KV_WIDTH), _BF16),
            pltpu.VMEM((N_KV_HEADS, BLOCK + TILE, KV_WIDTH), _BF16),
            pltpu.VMEM((N_KV_HEADS, KV_WIDTH, BLOCK + TILE), _BF16),
            pltpu.VMEM((TILE, D_MODEL), _BF16),
            pltpu.VMEM((TILE, D_MODEL), _F32),
            pltpu.VMEM((2, N_KV_HEADS * 2 * BLOCK, GQA_GROUP * BLOCK), _F32),
            pltpu.VMEM((D_MODEL, IN_WIDTH), _BF16),
            pltpu.VMEM((N_POOL_GROUPS, POOL_GROUP_DIM, POOL_GROUP_DIM), _BF16),
            pltpu.VMEM((D_MODEL, D_MODEL), _BF16),
            pltpu.VMEM((D_MODEL, D_FF), _BF16),
            pltpu.VMEM((D_MODEL, D_FF), _BF16),
            pltpu.VMEM((D_FF, D_MODEL), _BF16),
            pltpu.VMEM((WEIGHT_STAGE_SLOTS, WEIGHT_STAGE_ROWS, D_FF), _F32),
            pltpu.SemaphoreType.DMA((WEIGHT_STAGE_SLOTS,)),
            pltpu.VMEM((X_RING_SLOTS, TILE, D_MODEL), _F32),
            pltpu.SemaphoreType.DMA((X_RING_SLOTS,)),
            pltpu.VMEM((2, TILE, D_MODEL), _F32),
            pltpu.SemaphoreType.DMA((2,)),
            pltpu.VMEM((2, D_MODEL, BLOCK), _F32),
        ],
        compiler_params=pltpu.CompilerParams(vmem_limit_bytes=VMEM_LIMIT_BYTES),
        name="hybrid_layer",
    )(x2, w_in, w_pool, row(pool_scale), rel_bias.T, sinks, bucket_t, w_out,
      row(g_post_mix), w_gate, w_up, w_down, row(g_post_ffn), row(g_pre_mix), row(g_pre_ffn))
    return out.reshape(x.shape)


def kernel(x, g_pre_mix, w_in, w_pool, pool_scale, rel_bias, sinks, w_out, g_post_mix, g_pre_ffn, w_gate, w_up, w_down, g_post_ffn):
    depth = g_pre_mix.shape[0]
    for l in range(depth):
        x = _layer(x, g_pre_mix[l], w_in[l], w_pool[l], pool_scale[l], rel_bias, sinks[l], w_out[l], g_post_mix[l],
                   g_pre_ffn[l], w_gate[l], w_up[l], w_down[l], g_post_ffn[l])
    return x
```
